```python
import math
import jax, jax.numpy as jnp
from jax import lax
import numpy as np

D_MODEL = 4096
BATCH = 2
SEQ = 4096
DEPTH = 1

CHUNK = 64

SSD_EXPAND = 2
SSD_D_INNER = SSD_EXPAND * D_MODEL
SSD_HEAD_DIM = 64
SSD_HEADS = SSD_D_INNER // SSD_HEAD_DIM
SSD_GROUPS = 8
SSD_HPG = SSD_HEADS // SSD_GROUPS
SSD_STATE = 128
SSD_CONV = 4
SSD_CONV_DIM = SSD_D_INNER + 2 * SSD_GROUPS * SSD_STATE
SSD_NORM_EPS = 1e-5

RWKV_D = D_MODEL
RWKV_HEAD_DIM = 64
RWKV_HEADS = RWKV_D // RWKV_HEAD_DIM
W_RANK = max(32, int(round(1.8 * math.sqrt(D_MODEL) / 32)) * 32)
A_RANK = max(32, int(round(1.8 * math.sqrt(D_MODEL) / 32)) * 32)
G_RANK = max(32, int(round(0.6 * D_MODEL ** 0.8 / 32)) * 32)
DECAY_SCALE = math.exp(-0.5)
RWKV_GN_EPS = RWKV_HEAD_DIM * 1e-5
L2_EPS = 1e-12

SSD_IN_COLS = SSD_D_INNER + SSD_CONV_DIM + SSD_HEADS
RWKV_IN_COLS = 3 * RWKV_D + W_RANK + A_RANK + G_RANK
GATE_COLS = 2 * D_MODEL
IN_COLS = SSD_IN_COLS + RWKV_IN_COLS + GATE_COLS

D_FF = 4 * D_MODEL
PLE_DIM = 256
NORM_EPS = 1e-6

kernel_name = 'hybrid_ssd_rwkv7_gated_merge_block'


def rms_norm(x, g, eps=NORM_EPS):
    xf = x.astype(jnp.float32)
    y = xf * lax.rsqrt(jnp.mean(xf * xf, axis=-1, keepdims=True) + eps)
    return (y * g.astype(jnp.float32)).astype(x.dtype)


def split_cols(u, sizes):
    points, acc = [], 0
    for s in sizes[:-1]:
        acc += s
        points.append(acc)
    return jnp.split(u, points, axis=-1)


def causal_depthwise_conv(u, w, bias):
    ch = u.shape[-1]
    out = lax.conv_general_dilated(u, w[:, None, :].astype(u.dtype), window_strides=(1,),
                                   padding=[(w.shape[0] - 1, 0)],
                                   dimension_numbers=('NWC', 'WIO', 'NWC'),
                                   feature_group_count=ch)
    return out + bias.astype(u.dtype)


def token_shift(u):
    return jnp.pad(u, ((0, 0), (1, 0), (0, 0)))[:, :-1]


def ssd_chunked_scan(xs, a, bm, cm):
    bsz, t, g, r, pd = xs.shape
    n = bm.shape[-1]
    nc = t // CHUNK
    xs = xs.reshape(bsz, nc, CHUNK, g, r, pd)
    a = a.reshape(bsz, nc, CHUNK, g, r)
    bm = bm.reshape(bsz, nc, CHUNK, g, n)
    cm = cm.reshape(bsz, nc, CHUNK, g, n)
    a_cs = jnp.cumsum(a, axis=2)
    causal = jnp.tril(jnp.ones((CHUNK, CHUNK), dtype=bool))[None, None, :, :, None, None]
    seg = a_cs[:, :, :, None] - a_cs[:, :, None, :]
    decay = jnp.exp(jnp.where(causal, seg, -jnp.inf))
    cb = jnp.einsum('bclgn,bcsgn->bclsg', cm, bm)
    y_diag = jnp.einsum('bclsgr,bcsgrp->bclgrp', cb[..., None] * decay, xs)

    def step(state, inp):
        c_c, b_c, x_c, acs_c = inp
        y_off = jnp.einsum('blgn,bgrpn->blgrp', c_c, state) * jnp.exp(acs_c)[..., None]
        to_end = jnp.exp(acs_c[:, -1:] - acs_c)
        state = (state * jnp.exp(acs_c[:, -1])[..., None, None]
                 + jnp.einsum('blgn,blgrp->bgrpn', b_c, x_c * to_end[..., None]))
        return state, y_off

    s0 = jnp.zeros((bsz, g, r, pd, n), jnp.float32)
    _, y_off = lax.scan(step, s0, (jnp.moveaxis(cm, 1, 0), jnp.moveaxis(bm, 1, 0),
                                   jnp.moveaxis(xs, 1, 0), jnp.moveaxis(a_cs, 1, 0)))
    y = y_diag + jnp.moveaxis(y_off, 0, 1)
    return y.reshape(bsz, t, g, r, pd)


def ssd_mixer(z, xbc, dt_raw, conv_w, conv_b, dt_bias, a_log, d_skip, norm_g):
    f32 = jnp.float32
    bsz, t, _ = xbc.shape
    xbc = jax.nn.silu(causal_depthwise_conv(xbc, conv_w, conv_b)).astype(f32)
    xs, bm, cm = split_cols(xbc, (SSD_D_INNER, SSD_GROUPS * SSD_STATE, SSD_GROUPS * SSD_STATE))
    xs = xs.reshape(bsz, t, SSD_GROUPS, SSD_HPG, SSD_HEAD_DIM)
    bm = bm.reshape(bsz, t, SSD_GROUPS, SSD_STATE)
    cm = cm.reshape(bsz, t, SSD_GROUPS, SSD_STATE)
    dt = jax.nn.softplus(dt_raw.astype(f32) + dt_bias.astype(f32)).reshape(bsz, t, SSD_GROUPS, SSD_HPG)
    a = -jnp.exp(a_log.astype(f32)).reshape(SSD_GROUPS, SSD_HPG)
    y = ssd_chunked_scan(xs * dt[..., None], dt * a, bm, cm)
    y = y + d_skip.astype(f32).reshape(SSD_GROUPS, SSD_HPG)[:, :, None] * xs
    y = y.reshape(bsz, t, SSD_D_INNER) * jax.nn.silu(z.astype(f32))
    yg = y.reshape(bsz, t, SSD_GROUPS, SSD_D_INNER // SSD_GROUPS)
    yg = yg * lax.rsqrt(jnp.mean(yg * yg, axis=-1, keepdims=True) + SSD_NORM_EPS)
    return (yg.reshape(bsz, t, SSD_D_INNER) * norm_g.astype(f32)).astype(z.dtype)


def wkv7_scan(r, w, k, v, a_vec, b_vec):
    bsz, t, nh, nd = r.shape

    def step(state, inp):
        r_t, w_t, k_t, v_t, a_t, b_t = inp
        sa = jnp.einsum('bhvk,bhk->bhv', state, a_t)
        state = (state * w_t[:, :, None, :] + sa[..., None] * b_t[:, :, None, :]
                 + v_t[..., None] * k_t[:, :, None, :])
        return state, jnp.einsum('bhvk,bhk->bhv', state, r_t)

    s0 = jnp.zeros((bsz, nh, nd, nd), jnp.float32)
    _, y = lax.scan(step, s0, (jnp.moveaxis(r, 1, 0), jnp.moveaxis(w, 1, 0), jnp.moveaxis(k, 1, 0),
                               jnp.moveaxis(v, 1, 0), jnp.moveaxis(a_vec, 1, 0), jnp.moveaxis(b_vec, 1, 0)))
    return jnp.moveaxis(y, 0, 1)


def rwkv7_mixer(u, mu, w0, w2, a0, a2, g2, k_k, k_a, r_k, ln_g, ln_b):
    f32 = jnp.float32
    bsz, t, _ = u.shape
    u = u.astype(f32)
    u = u + (token_shift(u) - u) * mu.astype(f32)
    r, k, v, w_lo, a_lo, g_lo = split_cols(u, (RWKV_D, RWKV_D, RWKV_D, W_RANK, A_RANK, G_RANK))
    w = jnp.exp(-DECAY_SCALE * jax.nn.sigmoid(w0.astype(f32) + jnp.tanh(w_lo) @ w2.astype(f32)))
    a = jax.nn.sigmoid(a0.astype(f32) + a_lo @ a2.astype(f32))
    g = jax.nn.sigmoid(g_lo) @ g2.astype(f32)

    def heads(q):
        return q.reshape(bsz, t, RWKV_HEADS, RWKV_HEAD_DIM)

    kk = heads(k * k_k.astype(f32))
    kk = kk / jnp.maximum(jnp.sqrt(jnp.sum(kk * kk, axis=-1, keepdims=True)), L2_EPS)
    k = k * (1.0 + (a - 1.0) * k_a.astype(f32))
    rh, kh, vh, ah = heads(r), heads(k), heads(v), heads(a)
    y = wkv7_scan(rh, heads(w), kh, vh, -kk, kk * ah)
    mean = jnp.mean(y, axis=-1, keepdims=True)
    var = jnp.mean(jnp.square(y - mean), axis=-1, keepdims=True)
    y = ((y - mean) * lax.rsqrt(var + RWKV_GN_EPS)).reshape(bsz, t, RWKV_D)
    y = y * ln_g.astype(f32) + ln_b.astype(f32)
    bonus = jnp.sum(rh * kh * r_k.astype(f32), axis=-1, keepdims=True) * vh
    return (y + bonus.reshape(bsz, t, RWKV_D)) * g


def setup_inputs(seed: int = 0) -> dict:
    key = jax.random.key(seed)
    ks = jax.random.split(key, 32)
    f32 = jnp.float32

    def nrm(kk, shape, scale):
        return scale * jax.random.normal(kk, shape, f32)

    dt = jnp.exp(jax.random.uniform(ks[6], (DEPTH, SSD_HEADS), f32, math.log(1e-3), math.log(1e-1)))
    return {
        'x': nrm(ks[0], (BATCH, SEQ, D_MODEL), 1.0),
        'p': nrm(ks[1], (DEPTH, BATCH, SEQ, PLE_DIM), 1.0),
        'norm_mix_g': 1.0 + nrm(ks[2], (DEPTH, D_MODEL), 0.01),
        'w_in': nrm(ks[3], (DEPTH, D_MODEL, IN_COLS), D_MODEL ** -0.5),
        'ssd_conv_w': nrm(ks[4], (DEPTH, SSD_CONV, SSD_CONV_DIM), SSD_CONV ** -0.5),
        'ssd_conv_b': nrm(ks[5], (DEPTH, SSD_CONV_DIM), 0.01),
        'ssd_dt_bias': dt + jnp.log(-jnp.expm1(-dt)),
        'ssd_a_log': jnp.log(jax.random.uniform(ks[7], (DEPTH, SSD_HEADS), f32, 1.0, 16.0)),
        'ssd_d': 1.0 + nrm(ks[8], (DEPTH, SSD_HEADS), 0.01),
        'ssd_norm_g': 1.0 + nrm(ks[9], (DEPTH, SSD_D_INNER), 0.01),
        'rwkv_mu': jax.random.uniform(ks[10], (DEPTH, RWKV_IN_COLS), f32),
        'rwkv_w0': nrm(ks[11], (DEPTH, RWKV_D), 1.0),
        'rwkv_w2': nrm(ks[12], (DEPTH, W_RANK, RWKV_D), W_RANK ** -0.5),
        'rwkv_a0': nrm(ks[13], (DEPTH, RWKV_D), 0.1),
        'rwkv_a2': nrm(ks[14], (DEPTH, A_RANK, RWKV_D), A_RANK ** -0.5),
        'rwkv_g2': nrm(ks[15], (DEPTH, G_RANK, RWKV_D), G_RANK ** -0.5),
        'rwkv_k_k': 0.85 + nrm(ks[16], (DEPTH, RWKV_D), 0.01),
        'rwkv_k_a': 1.0 + nrm(ks[17], (DEPTH, RWKV_D), 0.01),
        'rwkv_r_k': nrm(ks[18], (DEPTH, RWKV_HEADS, RWKV_HEAD_DIM), 0.1),
        'rwkv_ln_g': 1.0 + nrm(ks[19], (DEPTH, RWKV_D), 0.01),
        'rwkv_ln_b': nrm(ks[20], (DEPTH, RWKV_D), 0.01),
        'w_branch_ssd': nrm(ks[21], (DEPTH, SSD_D_INNER, D_MODEL), SSD_D_INNER ** -0.5),
        'w_branch_rwkv': nrm(ks[22], (DEPTH, RWKV_D, D_MODEL), RWKV_D ** -0.5),
        'w_out': nrm(ks[23], (DEPTH, D_MODEL, D_MODEL), D_MODEL ** -0.5),
        'norm_ffn_g': 1.0 + nrm(ks[24], (DEPTH, D_MODEL), 0.01),
        'w_ff1': nrm(ks[25], (DEPTH, D_MODEL, D_FF), D_MODEL ** -0.5),
        'w_ff2': nrm(ks[26], (DEPTH, D_FF, D_MODEL), D_FF ** -0.5),
        'norm_ple_g': 1.0 + nrm(ks[27], (DEPTH, D_MODEL), 0.01),
        'w_ple_gate': nrm(ks[28], (DEPTH, D_MODEL, D_MODEL), D_MODEL ** -0.5),
        'w_ple_proj': nrm(ks[29], (DEPTH, PLE_DIM, D_MODEL), PLE_DIM ** -0.5),
        'ple_post_g': 1.0 + nrm(ks[30], (DEPTH, D_MODEL), 0.01),
        'final_norm_g': 1.0 + nrm(ks[31], (D_MODEL,), 0.01),
    }


def reference(x, p, norm_mix_g, w_in, ssd_conv_w, ssd_conv_b, ssd_dt_bias, ssd_a_log, ssd_d,
              ssd_norm_g, rwkv_mu, rwkv_w0, rwkv_w2, rwkv_a0, rwkv_a2, rwkv_g2, rwkv_k_k, rwkv_k_a,
              rwkv_r_k, rwkv_ln_g, rwkv_ln_b, w_branch_ssd, w_branch_rwkv, w_out, norm_ffn_g,
              w_ff1, w_ff2, norm_ple_g, w_ple_gate, w_ple_proj, ple_post_g, final_norm_g):
    for i in range(DEPTH):
        h = rms_norm(x, norm_mix_g[i])
        proj = h @ w_in[i]
        z, xbc, dt_raw, rwkv_in, gate_ssd, gate_rwkv = split_cols(
            proj, (SSD_D_INNER, SSD_CONV_DIM, SSD_HEADS, RWKV_IN_COLS, D_MODEL, D_MODEL))
        u_ssd = ssd_mixer(z, xbc, dt_raw, ssd_conv_w[i], ssd_conv_b[i], ssd_dt_bias[i],
                          ssd_a_log[i], ssd_d[i], ssd_norm_g[i])
        u_rwkv = rwkv7_mixer(rwkv_in, rwkv_mu[i], rwkv_w0[i], rwkv_w2[i], rwkv_a0[i], rwkv_a2[i],
                             rwkv_g2[i], rwkv_k_k[i], rwkv_k_a[i], rwkv_r_k[i], rwkv_ln_g[i],
                             rwkv_ln_b[i]).astype(x.dtype)
        merged = (jax.nn.sigmoid(gate_ssd) * (u_ssd @ w_branch_ssd[i])
                  + jax.nn.sigmoid(gate_rwkv) * (u_rwkv @ w_branch_rwkv[i]))
        x = x + merged @ w_out[i]
        h = rms_norm(x, norm_ffn_g[i])
        x = x + jnp.square(jax.nn.relu(h @ w_ff1[i])) @ w_ff2[i]
        h = rms_norm(x, norm_ple_g[i])
        gate = jax.nn.sigmoid(h @ w_ple_gate[i])
        e = rms_norm(p[i] @ w_ple_proj[i], ple_post_g[i])
        x = x + gate * e
    return rms_norm(x, final_norm_g)
```

```python
import functools
import math

import jax
import jax.numpy as jnp
from jax import lax
from jax.experimental import pallas as pl
from jax.experimental.pallas import tpu as pltpu

F32 = jnp.float32
BF16 = jnp.bfloat16

D_MODEL = 4096
CHUNK = 64
SSD_D_INNER = 8192
SSD_HEAD_DIM = 64
SSD_HEADS = 128
SSD_GROUPS = 8
SSD_HPG = 16
SSD_STATE = 128
SSD_GROUP_COLS = SSD_D_INNER // SSD_GROUPS
SSD_CONV_DIM = SSD_D_INNER + 2 * SSD_GROUPS * SSD_STATE
SSD_NORM_EPS = 1e-5
RWKV_HEAD_DIM = 64
W_RANK = 128
A_RANK = 128
G_RANK = 480
G_RANK_PAD = 512
DECAY_SCALE = math.exp(-0.5)
RWKV_GN_EPS = RWKV_HEAD_DIM * 1e-5
L2_EPS = 1e-12
NORM_EPS = 1e-6
PLE_DIM = 256

LANES = 128
VMEM_LIMIT_BYTES = 56 * 1024 * 1024


def _params(*sem):
    return pltpu.CompilerParams(dimension_semantics=sem, vmem_limit_bytes=VMEM_LIMIT_BYTES)


def _split3(x):
    hi = x.astype(BF16)
    r1 = x - hi.astype(F32)
    mid = r1.astype(BF16)
    lo = (r1 - mid.astype(F32)).astype(BF16)
    return hi, mid, lo


def _dot(a, b, dims=(((1,), (0,)), ((), ()))):
    return lax.dot_general(a.astype(BF16), b.astype(BF16), dims, preferred_element_type=F32)


def _dot_nt(a, b):
    return _dot(a, b, (((1,), (1,)), ((), ())))


def _dot_tn(a, b):
    return _dot(a, b, (((0,), (0,)), ((), ())))


def _cumsum_rows(tri_bf16, x):
    x0, x1, x2 = _split3(x)
    d = lambda q: jnp.dot(tri_bf16, q, preferred_element_type=F32)
    return d(x0) + d(x1) + d(x2)


def _sigmoid(x):
    return 1.0 / (1.0 + jnp.exp(-x))


def _silu(x):
    return x * _sigmoid(x)


def _tri64():
    r = lax.broadcasted_iota(jnp.int32, (CHUNK, CHUNK), 0)
    c = lax.broadcasted_iota(jnp.int32, (CHUNK, CHUNK), 1)
    return (r >= c).astype(BF16)


def _stack_heads(x, top):
    return jnp.concatenate([jnp.where(top, x, 0.0), jnp.where(top, 0.0, x)], axis=0)


def _rmsnorm_kernel(x_ref, g_ref, o_ref, *, eps):
    x = x_ref[...]
    y = x * lax.rsqrt(jnp.mean(x * x, axis=-1, keepdims=True) + eps)
    o_ref[...] = (y * g_ref[...]).astype(o_ref.dtype)


def _rmsnorm(x, g, out_dtype, tm=256):
    m, d = x.shape
    return pl.pallas_call(
        functools.partial(_rmsnorm_kernel, eps=NORM_EPS),
        grid=(m // tm,),
        in_specs=[pl.BlockSpec((tm, d), lambda i: (i, 0)), pl.BlockSpec((1, d), lambda i: (0, 0))],
        out_specs=pl.BlockSpec((tm, d), lambda i: (i, 0)),
        out_shape=jax.ShapeDtypeStruct((m, d), out_dtype),
        compiler_params=_params("parallel"),
    )(x, g.reshape(1, d))


def _act(x, kind):
    if kind is None:
        return x
    if kind == "relu2":
        r = jnp.maximum(x, 0.0)
        return r * r
    if kind == "sigmoid":
        return _sigmoid(x)
    if kind == "tanh":
        return jnp.tanh(x)
    raise ValueError(kind)


def _mm_kernel(*refs, nk, act, has_mul, mul_act, has_res):
    x_ref, w_ref = refs[0], refs[1]
    pos = 2
    mul_ref = res_ref = None
    if has_mul:
        mul_ref = refs[pos]
        pos += 1
    if has_res:
        res_ref = refs[pos]
        pos += 1
    o_ref = refs[pos]
    acc_ref = refs[pos + 1] if nk > 1 else None

    part = jnp.dot(x_ref[...], w_ref[...], preferred_element_type=F32)

    def finish(acc):
        out = _act(acc, act)
        if has_mul:
            out = out * _act(mul_ref[...].astype(F32), mul_act)
        if has_res:
            out = res_ref[...].astype(F32) + out
        o_ref[...] = out.astype(o_ref.dtype)

    if nk == 1:
        finish(part)
    else:
        k = pl.program_id(2)

        @pl.when(k == 0)
        def _():
            acc_ref[...] = part

        @pl.when(k > 0)
        def _():
            acc_ref[...] += part

        @pl.when(k == nk - 1)
        def _():
            finish(acc_ref[...])


def _mm(x, w, *, out_dtype, tm, tn, tk=None, act=None, mul=None, mul_act=None, mul_col0=0,
        res=None, res_col0=0):
    m, kdim = x.shape
    n = w.shape[1]
    tk = kdim if tk is None else tk
    tm, tn = min(tm, m), min(tn, n)
    nk = kdim // tk
    assert m % tm == 0 and n % tn == 0 and kdim % tk == 0 and mul_col0 % tn == 0 and res_col0 % tn == 0
    in_specs = [pl.BlockSpec((tm, tk), lambda i, j, k: (i, k)),
                pl.BlockSpec((tk, tn), lambda i, j, k: (k, j))]
    args = [x, w]
    if mul is not None:
        off = mul_col0 // tn
        in_specs.append(pl.BlockSpec((tm, tn), lambda i, j, k, off=off: (i, j + off)))
        args.append(mul)
    if res is not None:
        off = res_col0 // tn
        in_specs.append(pl.BlockSpec((tm, tn), lambda i, j, k, off=off: (i, j + off)))
        args.append(res)
    return pl.pallas_call(
        functools.partial(_mm_kernel, nk=nk, act=act, has_mul=mul is not None, mul_act=mul_act,
                          has_res=res is not None),
        grid=(m // tm, n // tn, nk),
        in_specs=in_specs,
        out_specs=pl.BlockSpec((tm, tn), lambda i, j, k: (i, j)),
        out_shape=jax.ShapeDtypeStruct((m, n), out_dtype),
        scratch_shapes=[pltpu.VMEM((tm, tn), F32)] if nk > 1 else [],
        compiler_params=_params("parallel", "parallel", "arbitrary"),
    )(*args)


def _ple_embed_kernel(p_ref, w_ref, g_ref, o_ref):
    e = jnp.dot(p_ref[...], w_ref[...], preferred_element_type=F32)
    y = e * lax.rsqrt(jnp.mean(e * e, axis=-1, keepdims=True) + NORM_EPS)
    o_ref[...] = (y * g_ref[...]).astype(o_ref.dtype)


def _ple_embed(p, w, g, tm=256):
    m, kdim = p.shape
    n = w.shape[1]
    return pl.pallas_call(
        _ple_embed_kernel,
        grid=(m // tm,),
        in_specs=[pl.BlockSpec((tm, kdim), lambda i: (i, 0)), pl.BlockSpec((kdim, n), lambda i: (0, 0)),
                  pl.BlockSpec((1, n), lambda i: (0, 0))],
        out_specs=pl.BlockSpec((tm, n), lambda i: (i, 0)),
        out_shape=jax.ShapeDtypeStruct((m, n), F32),
        compiler_params=_params("parallel"),
    )(p, w, g.reshape(1, n))


def _ssd_kernel(xs_ref, bm_ref, cm_ref, dt_ref, z_ref, alog_ref, d_ref, ng_ref, o_ref, st_ref, *, n_chunks):
    @pl.when(pl.program_id(2) == 0)
    def _():
        st_ref[...] = jnp.zeros_like(st_ref)

    gc = SSD_GROUP_COLS
    expand = ((lax.broadcasted_iota(jnp.int32, (SSD_HPG, gc), 1) >> 6)
              == lax.broadcasted_iota(jnp.int32, (SSD_HPG, gc), 0)).astype(BF16)
    tri = _tri64()
    li = lax.broadcasted_iota(jnp.int32, (CHUNK, gc), 0)
    si = lax.broadcasted_iota(jnp.int32, (CHUNK, gc), 1) & (CHUNK - 1)
    eye_t = li == si
    causal_t = li >= si
    top = lax.broadcasted_iota(jnp.int32, (CHUNK, LANES), 1) < SSD_HEAD_DIM
    a_row = -jnp.exp(alog_ref[0])
    d_row = d_ref[0]
    ng_row = ng_ref[...]

    def chunk(c, carry):
        sl = pl.ds(pl.multiple_of(c * CHUNK, CHUNK), CHUNK)
        xs = xs_ref[0, sl, :]
        bm = bm_ref[0, sl, :]
        cm = cm_ref[0, sl, :]
        dt = dt_ref[0, 0, sl, :]
        d0, d1, d2 = _split3(dt)
        de = lambda q: jnp.dot(q, expand, preferred_element_type=F32)
        dt_exp = de(d0) + de(d1) + de(d2)
        acs = _cumsum_rows(tri, dt_exp * a_row)
        rowpart = jnp.sum(jnp.where(eye_t, acs, 0.0), axis=0, keepdims=True)
        decay = jnp.exp(jnp.where(causal_t, acs - rowpart, -jnp.inf))
        xdt = xs * dt_exp
        cb2 = _dot_nt(cm, jnp.concatenate([bm, bm], axis=0))
        st = st_ref[...]
        last = acs[CHUNK - 1:CHUNK, :]
        y_off = _dot(cm, st) * jnp.exp(acs)
        parts = []
        for j in range(gc // LANES):
            lanes = slice(j * LANES, (j + 1) * LANES)
            parts.append(_dot(cb2 * decay[:, lanes], _stack_heads(xdt[:, lanes], top)))
        y = jnp.concatenate(parts, axis=1) + y_off + d_row * xs
        st_ref[...] = st * jnp.exp(last) + _dot_tn(bm, xdt * jnp.exp(last - acs))
        y = y * _silu(z_ref[0, sl, :])
        y = y * lax.rsqrt(jnp.mean(y * y, axis=-1, keepdims=True) + SSD_NORM_EPS)
        o_ref[0, sl, :] = (y * ng_row).astype(o_ref.dtype)
        return carry

    lax.fori_loop(0, n_chunks, chunk, 0)


def _ssd(xbc_act, dt_g, z_src, alog_exp, d_exp, norm_g, *, bsz, t, tb=256):
    gc = SSD_GROUP_COLS
    xs_blocks = SSD_D_INNER // gc
    bm_blk0 = SSD_D_INNER // SSD_STATE
    cm_blk0 = bm_blk0 + SSD_GROUPS
    return pl.pallas_call(
        functools.partial(_ssd_kernel, n_chunks=tb // CHUNK),
        grid=(bsz, SSD_GROUPS, t // tb),
        in_specs=[pl.BlockSpec((1, tb, gc), lambda b, g, ti: (b, ti, g)),
                  pl.BlockSpec((1, tb, SSD_STATE), lambda b, g, ti: (b, ti, bm_blk0 + g)),
                  pl.BlockSpec((1, tb, SSD_STATE), lambda b, g, ti: (b, ti, cm_blk0 + g)),
                  pl.BlockSpec((1, 1, tb, SSD_HPG), lambda b, g, ti: (b, g, ti, 0)),
                  pl.BlockSpec((1, tb, gc), lambda b, g, ti: (b, ti, g)),
                  pl.BlockSpec((1, 1, gc), lambda b, g, ti: (g, 0, 0)),
                  pl.BlockSpec((1, 1, gc), lambda b, g, ti: (g, 0, 0)),
                  pl.BlockSpec((1, gc), lambda b, g, ti: (0, g))],
        out_specs=pl.BlockSpec((1, tb, gc), lambda b, g, ti: (b, ti, g)),
        out_shape=jax.ShapeDtypeStruct((bsz, t, SSD_D_INNER), BF16),
        scratch_shapes=[pltpu.VMEM((SSD_STATE, gc), F32)],
        compiler_params=_params("parallel", "parallel", "arbitrary"),
    )(xbc_act, xbc_act, xbc_act, dt_g, z_src, alog_exp, d_exp, norm_g.reshape(1, SSD_D_INNER))


def _wkv_kernel(r_ref, lw_ref, k_ref, v_ref, kk_ref, a_ref, g_ref, rk_ref, lng_ref, lnb_ref, o_ref, s_ref,
                *, n_chunks):
    @pl.when(pl.program_id(2) == 0)
    def _():
        s_ref[...] = jnp.zeros_like(s_ref)

    row = lax.broadcasted_iota(jnp.int32, (LANES, LANES), 0)
    col = lax.broadcasted_iota(jnp.int32, (LANES, LANES), 1)
    same = (row >> 6) == (col >> 6)
    strict = same & ((row & 63) > (col & 63))
    incl = same & ((row & 63) >= (col & 63))
    eye = (row == col).astype(F32)
    tri = _tri64()
    top = lax.broadcasted_iota(jnp.int32, (CHUNK, LANES), 1) < RWKV_HEAD_DIM
    rk_row = rk_ref[...]
    lng_row = lng_ref[...]
    lnb_row = lnb_ref[...]
    inv_n = 1.0 / RWKV_HEAD_DIM

    def head_sum(x):
        s0 = jnp.sum(jnp.where(top, x, 0.0), axis=-1, keepdims=True)
        s1 = jnp.sum(jnp.where(top, 0.0, x), axis=-1, keepdims=True)
        return jnp.where(top, s0, s1)

    def chunk(c, carry):
        sl = pl.ds(pl.multiple_of(c * CHUNK, CHUNK), CHUNK)
        r = r_ref[0, sl, :]
        lw = lw_ref[0, sl, :]
        k = k_ref[0, sl, :]
        v = v_ref[0, sl, :]
        kk = kk_ref[0, sl, :]
        a = -kk
        b = kk * a_ref[0, sl, :]
        cum = _cumsum_rows(tri, lw)
        tot = cum[CHUNK - 1:CHUNK, :]
        e_neg = jnp.exp(-cum)
        e_end = jnp.exp(tot - cum)
        a_s = _stack_heads(a * jnp.exp(cum - lw), top)
        r_s = _stack_heads(r * jnp.exp(cum), top)
        b_s = _stack_heads(b * e_neg, top)
        k_s = _stack_heads(k * e_neg, top)
        v_s = _stack_heads(v, top)
        bh_s = _stack_heads(b * e_end, top)
        kh_s = _stack_heads(k * e_end, top)
        n = jnp.where(strict, _dot_nt(a_s, b_s), 0.0)
        ak = jnp.where(strict, _dot_nt(a_s, k_s), 0.0)
        rb = jnp.where(incl, _dot_nt(r_s, b_s), 0.0)
        rkm = jnp.where(incl, _dot_nt(r_s, k_s), 0.0)
        tinv = eye + n
        npow = n
        for _ in range(5):
            npow = _dot(npow, npow)
            tinv = tinv + _dot(tinv, npow)
        s = s_ref[...]
        p = _dot(tinv, a_s)
        q = _dot(tinv, _dot(ak, v_s))
        u = _dot_nt(p, s) + q
        y_st = _dot_nt(r_s, s) + _dot(rb, u) + _dot(rkm, v_s)
        s_new = s * jnp.exp(tot) + _dot_tn(u, bh_s) + _dot_tn(v_s, kh_s)
        s_ref[...] = jnp.where(same, s_new, 0.0)
        y = y_st[:CHUNK] + y_st[CHUNK:]
        mean = head_sum(y) * inv_n
        yc = y - mean
        var = head_sum(yc * yc) * inv_n
        yn = yc * lax.rsqrt(var + RWKV_GN_EPS) * lng_row + lnb_row
        bonus = head_sum(r * k * rk_row) * v
        o_ref[0, sl, :] = ((yn + bonus) * g_ref[0, sl, :]).astype(o_ref.dtype)
        return carry

    lax.fori_loop(0, n_chunks, chunk, 0)


def _wkv(r, lw, k, v, kk, a, g, r_k, ln_g, ln_b, *, tb=256):
    bsz, t, d = r.shape
    seq = pl.BlockSpec((1, tb, LANES), lambda b, h, ti: (b, ti, h))
    chan = pl.BlockSpec((1, LANES), lambda b, h, ti: (0, h))
    return pl.pallas_call(
        functools.partial(_wkv_kernel, n_chunks=tb // CHUNK),
        grid=(bsz, d // LANES, t // tb),
        in_specs=[seq] * 7 + [chan] * 3,
        out_specs=seq,
        out_shape=jax.ShapeDtypeStruct((bsz, t, d), BF16),
        scratch_shapes=[pltpu.VMEM((LANES, LANES), F32)],
        compiler_params=_params("parallel", "parallel", "arbitrary"),
    )(r, lw, k, v, kk, a, g, r_k.reshape(1, d), ln_g.reshape(1, d), ln_b.reshape(1, d))


def _token_shift(u):
    return jnp.pad(u, ((0, 0), (1, 0), (0, 0)))[:, :-1]


def kernel(x, p, norm_mix_g, w_in, ssd_conv_w, ssd_conv_b, ssd_dt_bias, ssd_a_log, ssd_d, ssd_norm_g, rwkv_mu, rwkv_w0, rwkv_w2, rwkv_a0, rwkv_a2, rwkv_g2, rwkv_k_k, rwkv_k_a, rwkv_r_k, rwkv_ln_g, rwkv_ln_b, w_branch_ssd, w_branch_rwkv, w_out, norm_ffn_g, w_ff1, w_ff2, norm_ple_g, w_ple_gate, w_ple_proj, ple_post_g, final_norm_g):
    bsz, t, d = x.shape
    m = bsz * t
    depth = w_in.shape[0]
    xf = x.reshape(m, d)
    for i in range(depth):
        c_z = 0
        c_xbc = c_z + SSD_D_INNER
        c_dt = c_xbc + SSD_CONV_DIM
        c_r = c_dt + SSD_HEADS
        c_wlo = c_r + 3 * d
        c_alo = c_wlo + W_RANK
        c_glo = c_alo + A_RANK
        c_gate = c_glo + G_RANK
        wi = w_in[i]
        w_zx = wi[:, c_z:c_dt].astype(BF16)
        w_rkv = wi[:, c_r:c_wlo].astype(BF16)
        w_gates = wi[:, c_gate:].astype(BF16)
        w_small = jnp.concatenate(
            [wi[:, c_dt:c_r], wi[:, c_wlo:c_gate],
             jnp.zeros((d, G_RANK_PAD - G_RANK), wi.dtype)], axis=1).astype(BF16)
        n_small = w_small.shape[1]

        h = _rmsnorm(xf, norm_mix_g[i], BF16)
        zx = _mm(h, w_zx, out_dtype=F32, tm=1024, tn=512)
        rkv = _mm(h, w_rkv, out_dtype=F32, tm=1024, tn=512)
        gates = _mm(h, w_gates, out_dtype=F32, tm=1024, tn=512)
        small = _mm(h, w_small, out_dtype=F32, tm=1024, tn=n_small)

        zx3 = zx.reshape(bsz, t, -1)
        xbc = zx3[:, :, SSD_D_INNER:]
        cw = ssd_conv_w[i]
        conv = ssd_conv_b[i] + sum(
            cw[j] * jnp.pad(xbc, ((0, 0), (cw.shape[0] - 1 - j, 0), (0, 0)))[:, :t] for j in range(cw.shape[0]))
        xbc_act = conv * jax.nn.sigmoid(conv)
        dt = jax.nn.softplus(small[:, :SSD_HEADS] + ssd_dt_bias[i])
        dt_g = jnp.moveaxis(dt.reshape(bsz, t, SSD_GROUPS, SSD_HPG), 2, 1)
        u_ssd = _ssd(xbc_act, dt_g, zx3,
                     jnp.repeat(ssd_a_log[i], SSD_HEAD_DIM).reshape(SSD_GROUPS, 1, SSD_GROUP_COLS),
                     jnp.repeat(ssd_d[i], SSD_HEAD_DIM).reshape(SSD_GROUPS, 1, SSD_GROUP_COLS),
                     ssd_norm_g[i], bsz=bsz, t=t)

        mu = rwkv_mu[i]
        rkv3 = rkv.reshape(bsz, t, 3 * d)
        rkv3 = rkv3 + (_token_shift(rkv3) - rkv3) * mu[:3 * d]
        lo3 = small[:, SSD_HEADS:].reshape(bsz, t, -1)
        mu_lo = jnp.concatenate([mu[3 * d:], jnp.zeros((G_RANK_PAD - G_RANK,), F32)])
        lo3 = (lo3 + (_token_shift(lo3) - lo3) * mu_lo).reshape(m, -1)
        w_lo, a_lo, g_lo = lo3[:, :W_RANK], lo3[:, W_RANK:W_RANK + A_RANK], lo3[:, W_RANK + A_RANK:]
        g2p = jnp.concatenate([rwkv_g2[i], jnp.zeros((G_RANK_PAD - G_RANK, d), F32)], axis=0).astype(BF16)
        w_pre = _mm(jnp.tanh(w_lo).astype(BF16), rwkv_w2[i].astype(BF16), out_dtype=F32, tm=1024, tn=1024)
        a_pre = _mm(a_lo.astype(BF16), rwkv_a2[i].astype(BF16), out_dtype=F32, tm=1024, tn=1024)
        g_gate = _mm(jax.nn.sigmoid(g_lo).astype(BF16), g2p, out_dtype=F32, tm=1024, tn=1024)
        lw = -DECAY_SCALE * jax.nn.sigmoid(rwkv_w0[i] + w_pre)
        a_gate = jax.nn.sigmoid(rwkv_a0[i] + a_pre)
        r_, k_, v_ = rkv3[..., :d].reshape(m, d), rkv3[..., d:2 * d].reshape(m, d), rkv3[..., 2 * d:]
        kk = (k_ * rwkv_k_k[i]).reshape(m, d // RWKV_HEAD_DIM, RWKV_HEAD_DIM)
        kk = kk / jnp.maximum(jnp.sqrt(jnp.sum(kk * kk, axis=-1, keepdims=True)), L2_EPS)
        k_mod = k_ * (1.0 + (a_gate - 1.0) * rwkv_k_a[i])
        s3 = lambda q: q.reshape(bsz, t, d)
        u_rwkv = _wkv(s3(r_), s3(lw), s3(k_mod), v_, s3(kk), s3(a_gate), s3(g_gate),
                      rwkv_r_k[i], rwkv_ln_g[i], rwkv_ln_b[i])

        part = _mm(u_ssd.reshape(m, SSD_D_INNER), w_branch_ssd[i].astype(BF16), out_dtype=F32, tm=1024, tn=512,
                   tk=4096, mul=gates, mul_act="sigmoid", mul_col0=0)
        merged = _mm(u_rwkv.reshape(m, d), w_branch_rwkv[i].astype(BF16), out_dtype=BF16, tm=1024, tn=512,
                     mul=gates, mul_act="sigmoid", mul_col0=d, res=part)
        xf = _mm(merged, w_out[i].astype(BF16), out_dtype=F32, tm=1024, tn=512, res=xf)

        h = _rmsnorm(xf, norm_ffn_g[i], BF16)
        ff = _mm(h, w_ff1[i].astype(BF16), out_dtype=BF16, tm=1024, tn=512, act="relu2")
        xf = _mm(ff, w_ff2[i].astype(BF16), out_dtype=F32, tm=1024, tn=512, tk=4096, res=xf)

        h = _rmsnorm(xf, norm_ple_g[i], BF16)
        e = _ple_embed(p[i].reshape(m, PLE_DIM).astype(BF16), w_ple_proj[i].astype(BF16), ple_post_g[i])
        xf = _mm(h, w_ple_gate[i].astype(BF16), out_dtype=F32, tm=1024, tn=512, act="sigmoid", mul=e, res=xf)
    return _rmsnorm(xf, final_norm_g, F32).reshape(bsz, t, d)
```

```python
import functools
import math

import jax
import jax.numpy as jnp
from jax import lax
from jax.experimental import pallas as pl
from jax.experimental.pallas import tpu as pltpu

F32 = jnp.float32
BF16 = jnp.bfloat16

D_MODEL = 4096
CHUNK = 64
SSD_D_INNER = 8192
SSD_HEAD_DIM = 64
SSD_HEADS = 128
SSD_GROUPS = 8
SSD_HPG = 16
SSD_STATE = 128
SSD_GROUP_COLS = SSD_D_INNER // SSD_GROUPS
SSD_CONV_DIM = SSD_D_INNER + 2 * SSD_GROUPS * SSD_STATE
SSD_NORM_EPS = 1e-5
RWKV_HEAD_DIM = 64
W_RANK = 128
A_RANK = 128
G_RANK = 480
G_RANK_PAD = 512
DECAY_SCALE = math.exp(-0.5)
RWKV_GN_EPS = RWKV_HEAD_DIM * 1e-5
L2_EPS = 1e-12
NORM_EPS = 1e-6
PLE_DIM = 256

LANES = 128
VMEM_LIMIT_BYTES = 56 * 1024 * 1024


def _params(*sem):
    return pltpu.CompilerParams(dimension_semantics=sem, vmem_limit_bytes=VMEM_LIMIT_BYTES)


def _split3(x):
    hi = x.astype(BF16)
    r1 = x - hi.astype(F32)
    mid = r1.astype(BF16)
    lo = (r1 - mid.astype(F32)).astype(BF16)
    return hi, mid, lo


def _dot(a, b, dims=(((1,), (0,)), ((), ()))):
    return lax.dot_general(a.astype(BF16), b.astype(BF16), dims, preferred_element_type=F32)


def _dot_nt(a, b):
    return _dot(a, b, (((1,), (1,)), ((), ())))


def _dot_tn(a, b):
    return _dot(a, b, (((0,), (0,)), ((), ())))


def _cumsum_rows(tri_bf16, x):
    x0, x1, x2 = _split3(x)
    d = lambda q: jnp.dot(tri_bf16, q, preferred_element_type=F32)
    return d(x0) + d(x1) + d(x2)


def _sigmoid(x):
    return 1.0 / (1.0 + jnp.exp(-x))


def _silu(x):
    return x * _sigmoid(x)


def _tri64():
    r = lax.broadcasted_iota(jnp.int32, (CHUNK, CHUNK), 0)
    c = lax.broadcasted_iota(jnp.int32, (CHUNK, CHUNK), 1)
    return (r >= c).astype(BF16)


def _stack_heads(x, top):
    return jnp.concatenate([jnp.where(top, x, 0.0), jnp.where(top, 0.0, x)], axis=0)


def _rmsnorm_kernel(x_ref, g_ref, o_ref, *, eps):
    x = x_ref[...]
    y = x * lax.rsqrt(jnp.mean(x * x, axis=-1, keepdims=True) + eps)
    o_ref[...] = (y * g_ref[...]).astype(o_ref.dtype)


def _rmsnorm(x, g, out_dtype, tm=256):
    m, d = x.shape
    return pl.pallas_call(
        functools.partial(_rmsnorm_kernel, eps=NORM_EPS),
        grid=(m // tm,),
        in_specs=[pl.BlockSpec((tm, d), lambda i: (i, 0)), pl.BlockSpec((1, d), lambda i: (0, 0))],
        out_specs=pl.BlockSpec((tm, d), lambda i: (i, 0)),
        out_shape=jax.ShapeDtypeStruct((m, d), out_dtype),
        name="rmsnorm",
        compiler_params=_params("parallel"),
    )(x, g.reshape(1, d))


def _act(x, kind):
    if kind is None:
        return x
    if kind == "relu2":
        r = jnp.maximum(x, 0.0)
        return r * r
    if kind == "sigmoid":
        return _sigmoid(x)
    if kind == "tanh":
        return jnp.tanh(x)
    raise ValueError(kind)


def _mm_kernel(*refs, nk, act, has_mul, mul_act, has_res):
    x_ref, w_ref = refs[0], refs[1]
    pos = 2
    mul_ref = res_ref = None
    if has_mul:
        mul_ref = refs[pos]
        pos += 1
    if has_res:
        res_ref = refs[pos]
        pos += 1
    o_ref = refs[pos]
    acc_ref = refs[pos + 1] if nk > 1 else None

    part = jnp.dot(x_ref[...], w_ref[...], preferred_element_type=F32)

    def finish(acc):
        out = _act(acc, act)
        if has_mul:
            out = out * _act(mul_ref[...].astype(F32), mul_act)
        if has_res:
            out = res_ref[...].astype(F32) + out
        o_ref[...] = out.astype(o_ref.dtype)

    if nk == 1:
        finish(part)
    else:
        k = pl.program_id(2)

        @pl.when(k == 0)
        def _():
            acc_ref[...] = part

        @pl.when(k > 0)
        def _():
            acc_ref[...] += part

        @pl.when(k == nk - 1)
        def _():
            finish(acc_ref[...])


def _mm(x, w, *, name, out_dtype, tm, tn, tk=None, act=None, mul=None, mul_act=None, mul_col0=0,
        res=None, res_col0=0):
    m, kdim = x.shape
    n = w.shape[1]
    tk = kdim if tk is None else tk
    tm, tn = min(tm, m), min(tn, n)
    nk = kdim // tk
    assert m % tm == 0 and n % tn == 0 and kdim % tk == 0 and mul_col0 % tn == 0 and res_col0 % tn == 0
    in_specs = [pl.BlockSpec((tm, tk), lambda i, j, k: (i, k)),
                pl.BlockSpec((tk, tn), lambda i, j, k: (k, j))]
    args = [x, w]
    if mul is not None:
        off = mul_col0 // tn
        in_specs.append(pl.BlockSpec((tm, tn), lambda i, j, k, off=off: (i, j + off)))
        args.append(mul)
    if res is not None:
        off = res_col0 // tn
        in_specs.append(pl.BlockSpec((tm, tn), lambda i, j, k, off=off: (i, j + off)))
        args.append(res)
    return pl.pallas_call(
        functools.partial(_mm_kernel, nk=nk, act=act, has_mul=mul is not None, mul_act=mul_act,
                          has_res=res is not None),
        grid=(m // tm, n // tn, nk),
        in_specs=in_specs,
        out_specs=pl.BlockSpec((tm, tn), lambda i, j, k: (i, j)),
        out_shape=jax.ShapeDtypeStruct((m, n), out_dtype),
        scratch_shapes=[pltpu.VMEM((tm, tn), F32)] if nk > 1 else [],
        name=name,
        compiler_params=_params("parallel", "parallel", "arbitrary"),
    )(*args)


def _ple_embed_kernel(p_ref, w_ref, g_ref, o_ref):
    e = jnp.dot(p_ref[...], w_ref[...], preferred_element_type=F32)
    y = e * lax.rsqrt(jnp.mean(e * e, axis=-1, keepdims=True) + NORM_EPS)
    o_ref[...] = (y * g_ref[...]).astype(o_ref.dtype)


def _ple_embed(p, w, g, tm=256):
    m, kdim = p.shape
    n = w.shape[1]
    return pl.pallas_call(
        _ple_embed_kernel,
        grid=(m // tm,),
        in_specs=[pl.BlockSpec((tm, kdim), lambda i: (i, 0)), pl.BlockSpec((kdim, n), lambda i: (0, 0)),
                  pl.BlockSpec((1, n), lambda i: (0, 0))],
        out_specs=pl.BlockSpec((tm, n), lambda i: (i, 0)),
        out_shape=jax.ShapeDtypeStruct((m, n), F32),
        name="ple_embed",
        compiler_params=_params("parallel"),
    )(p, w, g.reshape(1, n))


def _ssd_kernel(xs_ref, bm_ref, cm_ref, dt_ref, z_ref, alog_ref, d_ref, ng_ref, o_ref, st_ref, *, n_chunks):
    @pl.when(pl.program_id(2) == 0)
    def _():
        st_ref[...] = jnp.zeros_like(st_ref)

    gc = SSD_GROUP_COLS
    expand = ((lax.broadcasted_iota(jnp.int32, (SSD_HPG, gc), 1) >> 6)
              == lax.broadcasted_iota(jnp.int32, (SSD_HPG, gc), 0)).astype(BF16)
    tri = _tri64()
    li = lax.broadcasted_iota(jnp.int32, (CHUNK, gc), 0)
    si = lax.broadcasted_iota(jnp.int32, (CHUNK, gc), 1) & (CHUNK - 1)
    eye_t = li == si
    causal_t = li >= si
    top = lax.broadcasted_iota(jnp.int32, (CHUNK, LANES), 1) < SSD_HEAD_DIM
    a_row = -jnp.exp(alog_ref[0])
    d_row = d_ref[0]
    ng_row = ng_ref[...]

    def chunk(c, carry):
        sl = pl.ds(pl.multiple_of(c * CHUNK, CHUNK), CHUNK)
        xs = xs_ref[0, sl, :]
        bm = bm_ref[0, sl, :]
        cm = cm_ref[0, sl, :]
        dt = dt_ref[0, 0, sl, :]
        d0, d1, d2 = _split3(dt)
        de = lambda q: jnp.dot(q, expand, preferred_element_type=F32)
        dt_exp = de(d0) + de(d1) + de(d2)
        acs = _cumsum_rows(tri, dt_exp * a_row)
        rowpart = jnp.sum(jnp.where(eye_t, acs, 0.0), axis=0, keepdims=True)
        decay = jnp.exp(jnp.where(causal_t, acs - rowpart, -jnp.inf))
        xdt = xs * dt_exp
        cb2 = _dot_nt(cm, jnp.concatenate([bm, bm], axis=0))
        st = st_ref[...]
        last = acs[CHUNK - 1:CHUNK, :]
        y_off = _dot(cm, st) * jnp.exp(acs)
        parts = []
        for j in range(gc // LANES):
            lanes = slice(j * LANES, (j + 1) * LANES)
            parts.append(_dot(cb2 * decay[:, lanes], _stack_heads(xdt[:, lanes], top)))
        y = jnp.concatenate(parts, axis=1) + y_off + d_row * xs
        st_ref[...] = st * jnp.exp(last) + _dot_tn(bm, xdt * jnp.exp(last - acs))
        y = y * _silu(z_ref[0, sl, :])
        y = y * lax.rsqrt(jnp.mean(y * y, axis=-1, keepdims=True) + SSD_NORM_EPS)
        o_ref[0, sl, :] = (y * ng_row).astype(o_ref.dtype)
        return carry

    lax.fori_loop(0, n_chunks, chunk, 0)


def _ssd(xbc_act, dt_g, z_src, alog_exp, d_exp, norm_g, *, bsz, t, tb=256):
    gc = SSD_GROUP_COLS
    xs_blocks = SSD_D_INNER // gc
    bm_blk0 = SSD_D_INNER // SSD_STATE
    cm_blk0 = bm_blk0 + SSD_GROUPS
    return pl.pallas_call(
        functools.partial(_ssd_kernel, n_chunks=tb // CHUNK),
        grid=(bsz, SSD_GROUPS, t // tb),
        in_specs=[pl.BlockSpec((1, tb, gc), lambda b, g, ti: (b, ti, g)),
                  pl.BlockSpec((1, tb, SSD_STATE), lambda b, g, ti: (b, ti, bm_blk0 + g)),
                  pl.BlockSpec((1, tb, SSD_STATE), lambda b, g, ti: (b, ti, cm_blk0 + g)),
                  pl.BlockSpec((1, 1, tb, SSD_HPG), lambda b, g, ti: (b, g, ti, 0)),
                  pl.BlockSpec((1, tb, gc), lambda b, g, ti: (b, ti, g)),
                  pl.BlockSpec((1, 1, gc), lambda b, g, ti: (g, 0, 0)),
                  pl.BlockSpec((1, 1, gc), lambda b, g, ti: (g, 0, 0)),
                  pl.BlockSpec((1, gc), lambda b, g, ti: (0, g))],
        out_specs=pl.BlockSpec((1, tb, gc), lambda b, g, ti: (b, ti, g)),
        out_shape=jax.ShapeDtypeStruct((bsz, t, SSD_D_INNER), BF16),
        scratch_shapes=[pltpu.VMEM((SSD_STATE, gc), F32)],
        name="ssd_scan",
        compiler_params=_params("parallel", "parallel", "arbitrary"),
    )(xbc_act, xbc_act, xbc_act, dt_g, z_src, alog_exp, d_exp, norm_g.reshape(1, SSD_D_INNER))


def _wkv_kernel(r_ref, lw_ref, k_ref, v_ref, kk_ref, a_ref, g_ref, rk_ref, lng_ref, lnb_ref, o_ref, s_ref,
                *, n_chunks, n_pairs):
    @pl.when(pl.program_id(2) == 0)
    def _():
        s_ref[...] = jnp.zeros_like(s_ref)

    row = lax.broadcasted_iota(jnp.int32, (LANES, LANES), 0)
    col = lax.broadcasted_iota(jnp.int32, (LANES, LANES), 1)
    same = (row >> 6) == (col >> 6)
    strict = same & ((row & 63) > (col & 63))
    incl = same & ((row & 63) >= (col & 63))
    eye = (row == col).astype(F32)
    tb = n_chunks * CHUNK
    trow = lax.broadcasted_iota(jnp.int32, (tb, tb), 0)
    tcol = lax.broadcasted_iota(jnp.int32, (tb, tb), 1)
    tri_blk = ((trow >> 6) == (tcol >> 6)) & (trow >= tcol)
    top = lax.broadcasted_iota(jnp.int32, (CHUNK, LANES), 1) < RWKV_HEAD_DIM
    inv_n = 1.0 / RWKV_HEAD_DIM
    chains = [(j, c) for j in range(n_pairs) for c in range(n_chunks)]

    def head_sum(x):
        s0 = jnp.sum(jnp.where(top, x, 0.0), axis=-1, keepdims=True)
        s1 = jnp.sum(jnp.where(top, 0.0, x), axis=-1, keepdims=True)
        return jnp.where(top, s0, s1)

    def piece(ref, j, c):
        return ref[0, c * CHUNK:(c + 1) * CHUNK, j * LANES:(j + 1) * LANES]

    def fold(x):
        return x[:CHUNK] + x[CHUNK:]

    cum_all = _cumsum_rows(tri_blk.astype(BF16), lw_ref[0])

    a_s, r_s, b_s, k_s, v_s, bh_s, kh_s, wtot = [], [], [], [], [], [], [], []
    for j, c in chains:
        cum = cum_all[c * CHUNK:(c + 1) * CHUNK, j * LANES:(j + 1) * LANES]
        lw = piece(lw_ref, j, c)
        k = piece(k_ref, j, c)
        kk = piece(kk_ref, j, c)
        b = kk * piece(a_ref, j, c)
        tot = cum[CHUNK - 1:CHUNK, :]
        e_neg = jnp.exp(-cum)
        e_end = jnp.exp(tot - cum)
        a_s.append(_stack_heads(-kk * jnp.exp(cum - lw), top).astype(BF16))
        r_s.append(_stack_heads(piece(r_ref, j, c) * jnp.exp(cum), top))
        b_s.append(_stack_heads(b * e_neg, top).astype(BF16))
        k_s.append(_stack_heads(k * e_neg, top).astype(BF16))
        v_s.append(_stack_heads(piece(v_ref, j, c), top).astype(BF16))
        bh_s.append(_stack_heads(b * e_end, top).astype(BF16))
        kh_s.append(_stack_heads(k * e_end, top).astype(BF16))
        wtot.append(jnp.exp(tot))
    nb = range(len(chains))
    r_b = [x.astype(BF16) for x in r_s]
    n = [jnp.where(strict, _dot_nt(a_s[i], b_s[i]), 0.0) for i in nb]
    ak = [jnp.where(strict, _dot_nt(a_s[i], k_s[i]), 0.0) for i in nb]
    rb = [jnp.where(incl, _dot_nt(r_b[i], b_s[i]), 0.0).astype(BF16) for i in nb]
    rkm = [jnp.where(incl, _dot_nt(r_b[i], k_s[i]), 0.0) for i in nb]
    tinv = [eye + x for x in n]
    npow = n
    for _ in range(5):
        npow = [_dot(x, x) for x in npow]
        tinv = [tinv[i] + _dot(tinv[i], npow[i]) for i in nb]
    akv = [_dot(ak[i], v_s[i]) for i in nb]
    p = [_dot(tinv[i], a_s[i]).astype(BF16) for i in nb]
    q = [_dot(tinv[i], akv[i]).astype(BF16) for i in nb]
    r_eff = [fold(r_s[i] + _dot(rb[i], p[i])).astype(BF16) for i in nb]
    y0 = [fold(_dot(rb[i], q[i]) + _dot(rkm[i], v_s[i])) for i in nb]
    g_eff = [_dot_tn(p[i], bh_s[i]).astype(BF16) for i in nb]
    s1 = [_dot_tn(q[i], bh_s[i]) + _dot_tn(v_s[i], kh_s[i]) for i in nb]

    for j in range(n_pairs):
        lanes = slice(j * LANES, (j + 1) * LANES)
        rk_row = rk_ref[:, lanes]
        lng_row = lng_ref[:, lanes]
        lnb_row = lnb_ref[:, lanes]
        s = s_ref[j]
        for c in range(n_chunks):
            i = j * n_chunks + c
            y = _dot_nt(r_eff[i], s) + y0[i]
            s = s * wtot[i] + _dot(s, g_eff[i]) + s1[i]
            mean = head_sum(y) * inv_n
            yc = y - mean
            var = head_sum(yc * yc) * inv_n
            yn = yc * lax.rsqrt(var + RWKV_GN_EPS) * lng_row + lnb_row
            v = piece(v_ref, j, c)
            bonus = head_sum(piece(r_ref, j, c) * piece(k_ref, j, c) * rk_row) * v
            o_ref[0, c * CHUNK:(c + 1) * CHUNK, lanes] = ((yn + bonus) * piece(g_ref, j, c)).astype(o_ref.dtype)
        s_ref[j] = s


def _wkv(r, lw, k, v, kk, a, g, r_k, ln_g, ln_b, *, tb=256, pairs=2):
    bsz, t, d = r.shape
    wl = pairs * LANES
    seq = pl.BlockSpec((1, tb, wl), lambda b, h, ti: (b, ti, h))
    chan = pl.BlockSpec((1, wl), lambda b, h, ti: (0, h))
    return pl.pallas_call(
        functools.partial(_wkv_kernel, n_chunks=tb // CHUNK, n_pairs=pairs),
        grid=(bsz, d // wl, t // tb),
        in_specs=[seq] * 7 + [chan] * 3,
        out_specs=seq,
        out_shape=jax.ShapeDtypeStruct((bsz, t, d), BF16),
        scratch_shapes=[pltpu.VMEM((pairs, LANES, LANES), F32)],
        name="wkv7_scan",
        compiler_params=_params("parallel", "parallel", "arbitrary"),
    )(r, lw, k, v, kk, a, g, r_k.reshape(1, d), ln_g.reshape(1, d), ln_b.reshape(1, d))


def _token_shift(u):
    return jnp.pad(u, ((0, 0), (1, 0), (0, 0)))[:, :-1]


def kernel(x, p, norm_mix_g, w_in, ssd_conv_w, ssd_conv_b, ssd_dt_bias, ssd_a_log, ssd_d, ssd_norm_g, rwkv_mu, rwkv_w0, rwkv_w2, rwkv_a0, rwkv_a2, rwkv_g2, rwkv_k_k, rwkv_k_a, rwkv_r_k, rwkv_ln_g, rwkv_ln_b, w_branch_ssd, w_branch_rwkv, w_out, norm_ffn_g, w_ff1, w_ff2, norm_ple_g, w_ple_gate, w_ple_proj, ple_post_g, final_norm_g):
    bsz, t, d = x.shape
    m = bsz * t
    depth = w_in.shape[0]
    xf = x.reshape(m, d)
    for i in range(depth):
        c_z = 0
        c_xbc = c_z + SSD_D_INNER
        c_dt = c_xbc + SSD_CONV_DIM
        c_r = c_dt + SSD_HEADS
        c_wlo = c_r + 3 * d
        c_alo = c_wlo + W_RANK
        c_glo = c_alo + A_RANK
        c_gate = c_glo + G_RANK
        wi = w_in[i]
        w_zx = wi[:, c_z:c_dt].astype(BF16)
        w_rkv = wi[:, c_r:c_wlo].astype(BF16)
        w_gates = wi[:, c_gate:].astype(BF16)
        w_small = jnp.concatenate(
            [wi[:, c_dt:c_r], wi[:, c_wlo:c_gate],
             jnp.zeros((d, G_RANK_PAD - G_RANK), wi.dtype)], axis=1).astype(BF16)
        n_small = w_small.shape[1]

        h = _rmsnorm(xf, norm_mix_g[i], BF16)
        zx = _mm(h, w_zx, name="proj_zx", out_dtype=F32, tm=1024, tn=512)
        rkv = _mm(h, w_rkv, name="proj_rkv", out_dtype=F32, tm=1024, tn=512)
        gates = _mm(h, w_gates, name="proj_gates", out_dtype=F32, tm=1024, tn=512)
        small = _mm(h, w_small, name="proj_small", out_dtype=F32, tm=1024, tn=n_small)

        zx3 = zx.reshape(bsz, t, -1)
        xbc = zx3[:, :, SSD_D_INNER:]
        cw = ssd_conv_w[i]
        conv = ssd_conv_b[i] + sum(
            cw[j] * jnp.pad(xbc, ((0, 0), (cw.shape[0] - 1 - j, 0), (0, 0)))[:, :t] for j in range(cw.shape[0]))
        xbc_act = conv * jax.nn.sigmoid(conv)
        dt = jax.nn.softplus(small[:, :SSD_HEADS] + ssd_dt_bias[i])
        dt_g = jnp.moveaxis(dt.reshape(bsz, t, SSD_GROUPS, SSD_HPG), 2, 1)
        u_ssd = _ssd(xbc_act, dt_g, zx3,
                     jnp.repeat(ssd_a_log[i], SSD_HEAD_DIM).reshape(SSD_GROUPS, 1, SSD_GROUP_COLS),
                     jnp.repeat(ssd_d[i], SSD_HEAD_DIM).reshape(SSD_GROUPS, 1, SSD_GROUP_COLS),
                     ssd_norm_g[i], bsz=bsz, t=t)

        mu = rwkv_mu[i]
        rkv3 = rkv.reshape(bsz, t, 3 * d)
        rkv3 = rkv3 + (_token_shift(rkv3) - rkv3) * mu[:3 * d]
        lo3 = small[:, SSD_HEADS:].reshape(bsz, t, -1)
        mu_lo = jnp.concatenate([mu[3 * d:], jnp.zeros((G_RANK_PAD - G_RANK,), F32)])
        lo3 = (lo3 + (_token_shift(lo3) - lo3) * mu_lo).reshape(m, -1)
        w_lo, a_lo, g_lo = lo3[:, :W_RANK], lo3[:, W_RANK:W_RANK + A_RANK], lo3[:, W_RANK + A_RANK:]
        g2p = jnp.concatenate([rwkv_g2[i], jnp.zeros((G_RANK_PAD - G_RANK, d), F32)], axis=0).astype(BF16)
        w_pre = _mm(jnp.tanh(w_lo).astype(BF16), rwkv_w2[i].astype(BF16), name="lora_w", out_dtype=F32,
                    tm=1024, tn=1024)
        a_pre = _mm(a_lo.astype(BF16), rwkv_a2[i].astype(BF16), name="lora_a", out_dtype=F32, tm=1024, tn=1024)
        g_gate = _mm(jax.nn.sigmoid(g_lo).astype(BF16), g2p, name="lora_g", out_dtype=F32, tm=1024, tn=1024)
        lw = -DECAY_SCALE * jax.nn.sigmoid(rwkv_w0[i] + w_pre)
        a_gate = jax.nn.sigmoid(rwkv_a0[i] + a_pre)
        r_, k_, v_ = rkv3[..., :d].reshape(m, d), rkv3[..., d:2 * d].reshape(m, d), rkv3[..., 2 * d:]
        kk = (k_ * rwkv_k_k[i]).reshape(m, d // RWKV_HEAD_DIM, RWKV_HEAD_DIM)
        kk = kk / jnp.maximum(jnp.sqrt(jnp.sum(kk * kk, axis=-1, keepdims=True)), L2_EPS)
        k_mod = k_ * (1.0 + (a_gate - 1.0) * rwkv_k_a[i])
        s3 = lambda q: q.reshape(bsz, t, d)
        u_rwkv = _wkv(s3(r_), s3(lw), s3(k_mod), v_, s3(kk), s3(a_gate), s3(g_gate),
                      rwkv_r_k[i], rwkv_ln_g[i], rwkv_ln_b[i])

        part = _mm(u_ssd.reshape(m, SSD_D_INNER), w_branch_ssd[i].astype(BF16), name="branch_ssd", out_dtype=F32,
                   tm=1024, tn=512,
                   tk=4096, mul=gates, mul_act="sigmoid", mul_col0=0)
        merged = _mm(u_rwkv.reshape(m, d), w_branch_rwkv[i].astype(BF16), name="branch_rwkv", out_dtype=BF16,
                     tm=1024, tn=512,
                     mul=gates, mul_act="sigmoid", mul_col0=d, res=part)
        xf = _mm(merged, w_out[i].astype(BF16), name="out_proj", out_dtype=F32, tm=1024, tn=512, res=xf)

        h = _rmsnorm(xf, norm_ffn_g[i], BF16)
        ff = _mm(h, w_ff1[i].astype(BF16), name="ffn_up", out_dtype=BF16, tm=1024, tn=512, act="relu2")
        xf = _mm(ff, w_ff2[i].astype(BF16), name="ffn_down", out_dtype=F32, tm=1024, tn=512, tk=4096, res=xf)

        h = _rmsnorm(xf, norm_ple_g[i], BF16)
        e = _ple_embed(p[i].reshape(m, PLE_DIM).astype(BF16), w_ple_proj[i].astype(BF16), ple_post_g[i])
        xf = _mm(h, w_ple_gate[i].astype(BF16), name="ple_gate", out_dtype=F32, tm=1024, tn=512, act="sigmoid",
                 mul=e, res=xf)
    return _rmsnorm(xf, final_norm_g, F32).reshape(bsz, t, d)
```

```python
import functools
import math

import jax
import jax.numpy as jnp
from jax import lax
from jax.experimental import pallas as pl
from jax.experimental.pallas import tpu as pltpu

F32 = jnp.float32
BF16 = jnp.bfloat16

CHUNK = 64
SSD_D_INNER = 8192
SSD_HEAD_DIM = 64
SSD_HEADS = 128
SSD_GROUPS = 8
SSD_HPG = 16
SSD_STATE = 128
SSD_CONV = 4
SSD_GROUP_COLS = SSD_D_INNER // SSD_GROUPS
SSD_CONV_DIM = SSD_D_INNER + 2 * SSD_GROUPS * SSD_STATE
SSD_NORM_EPS = 1e-5
RWKV_HEAD_DIM = 64
W_RANK = 128
A_RANK = 128
G_RANK = 480
G_RANK_PAD = 512
DECAY_SCALE = math.exp(-0.5)
RWKV_GN_EPS = RWKV_HEAD_DIM * 1e-5
L2_EPS = 1e-12
NORM_EPS = 1e-6
PLE_DIM = 256

SMALL_COLS = G_RANK_PAD + SSD_HEADS + W_RANK + A_RANK
SMALL_DT_BLK = G_RANK_PAD // 128
SMALL_WLO_BLK = SMALL_DT_BLK + 1
SMALL_ALO_BLK = SMALL_DT_BLK + 2

LANES = 128
SUBLANES = 8
VMEM_LIMIT_BYTES = 56 * 1024 * 1024


def _params(*sem):
    return pltpu.CompilerParams(dimension_semantics=sem, vmem_limit_bytes=VMEM_LIMIT_BYTES)


def _split3(x):
    hi = x.astype(BF16)
    r1 = x - hi.astype(F32)
    mid = r1.astype(BF16)
    lo = (r1 - mid.astype(F32)).astype(BF16)
    return hi, mid, lo


def _dot(a, b, dims=(((1,), (0,)), ((), ()))):
    return lax.dot_general(a.astype(BF16), b.astype(BF16), dims, preferred_element_type=F32)


def _dot_nt(a, b):
    return _dot(a, b, (((1,), (1,)), ((), ())))


def _dot_tn(a, b):
    return _dot(a, b, (((0,), (0,)), ((), ())))


def _cumsum_rows(tri_bf16, x):
    x0, x1, x2 = _split3(x)
    d = lambda q: jnp.dot(tri_bf16, q, preferred_element_type=F32)
    return d(x0) + d(x1) + d(x2)


def _sigmoid(x):
    return 1.0 / (1.0 + jnp.exp(-x))


def _silu(x):
    return x * _sigmoid(x)


def _softplus(x):
    return jnp.maximum(x, 0.0) + jnp.log1p(jnp.exp(-jnp.abs(x)))


def _tri64():
    r = lax.broadcasted_iota(jnp.int32, (CHUNK, CHUNK), 0)
    c = lax.broadcasted_iota(jnp.int32, (CHUNK, CHUNK), 1)
    return (r >= c).astype(BF16)


def _stack_heads(x, top):
    return jnp.concatenate([jnp.where(top, x, 0.0), jnp.where(top, 0.0, x)], axis=0)


def _head_sum(x, top):
    s0 = jnp.sum(jnp.where(top, x, 0.0), axis=-1, keepdims=True)
    s1 = jnp.sum(jnp.where(top, 0.0, x), axis=-1, keepdims=True)
    return jnp.where(top, s0, s1)


def _rmsnorm_kernel(x_ref, g_ref, o_ref, *, eps):
    x = x_ref[...]
    y = x * lax.rsqrt(jnp.mean(x * x, axis=-1, keepdims=True) + eps)
    o_ref[...] = (y * g_ref[...]).astype(o_ref.dtype)


def _rmsnorm(x, g, out_dtype, tm=256):
    m, d = x.shape
    return pl.pallas_call(
        functools.partial(_rmsnorm_kernel, eps=NORM_EPS),
        grid=(m // tm,),
        in_specs=[pl.BlockSpec((tm, d), lambda i: (i, 0)), pl.BlockSpec((1, d), lambda i: (0, 0))],
        out_specs=pl.BlockSpec((tm, d), lambda i: (i, 0)),
        out_shape=jax.ShapeDtypeStruct((m, d), out_dtype),
        name="rmsnorm",
        compiler_params=_params("parallel"),
    )(x, g.reshape(1, d))


def _act(x, kind):
    if kind is None:
        return x
    if kind == "relu2":
        r = jnp.maximum(x, 0.0)
        return r * r
    if kind == "sigmoid":
        return _sigmoid(x)
    raise ValueError(kind)


def _mm_kernel(*refs, nk, act, has_mul, mul_act, has_res):
    x_ref, w_ref = refs[0], refs[1]
    pos = 2
    mul_ref = res_ref = None
    if has_mul:
        mul_ref = refs[pos]
        pos += 1
    if has_res:
        res_ref = refs[pos]
        pos += 1
    o_ref = refs[pos]
    acc_ref = refs[pos + 1] if nk > 1 else None

    part = jnp.dot(x_ref[...], w_ref[...], preferred_element_type=F32)

    def finish(acc):
        out = _act(acc, act)
        if has_mul:
            out = out * _act(mul_ref[...].astype(F32), mul_act)
        if has_res:
            out = res_ref[...].astype(F32) + out
        o_ref[...] = out.astype(o_ref.dtype)

    if nk == 1:
        finish(part)
    else:
        k = pl.program_id(2)

        @pl.when(k == 0)
        def _():
            acc_ref[...] = part

        @pl.when(k > 0)
        def _():
            acc_ref[...] += part

        @pl.when(k == nk - 1)
        def _():
            finish(acc_ref[...])


def _mm(x, w, *, name, out_dtype, tm, tn, tk=None, act=None, mul=None, mul_act=None, mul_col0=0,
        res=None, res_col0=0):
    m, kdim = x.shape
    n = w.shape[1]
    tk = kdim if tk is None else tk
    tm, tn = min(tm, m), min(tn, n)
    nk = kdim // tk
    assert m % tm == 0 and n % tn == 0 and kdim % tk == 0 and mul_col0 % tn == 0 and res_col0 % tn == 0
    in_specs = [pl.BlockSpec((tm, tk), lambda i, j, k: (i, k)),
                pl.BlockSpec((tk, tn), lambda i, j, k: (k, j))]
    args = [x, w]
    if mul is not None:
        off = mul_col0 // tn
        in_specs.append(pl.BlockSpec((tm, tn), lambda i, j, k, off=off: (i, j + off)))
        args.append(mul)
    if res is not None:
        off = res_col0 // tn
        in_specs.append(pl.BlockSpec((tm, tn), lambda i, j, k, off=off: (i, j + off)))
        args.append(res)
    return pl.pallas_call(
        functools.partial(_mm_kernel, nk=nk, act=act, has_mul=mul is not None, mul_act=mul_act,
                          has_res=res is not None),
        grid=(m // tm, n // tn, nk),
        in_specs=in_specs,
        out_specs=pl.BlockSpec((tm, tn), lambda i, j, k: (i, j)),
        out_shape=jax.ShapeDtypeStruct((m, n), out_dtype),
        scratch_shapes=[pltpu.VMEM((tm, tn), F32)] if nk > 1 else [],
        name=name,
        compiler_params=_params("parallel", "parallel", "arbitrary"),
    )(*args)


def _ple_embed_kernel(p_ref, w_ref, g_ref, o_ref):
    e = jnp.dot(p_ref[...], w_ref[...], preferred_element_type=F32)
    y = e * lax.rsqrt(jnp.mean(e * e, axis=-1, keepdims=True) + NORM_EPS)
    o_ref[...] = (y * g_ref[...]).astype(o_ref.dtype)


def _ple_embed(p, w, g, tm=256):
    m, kdim = p.shape
    n = w.shape[1]
    return pl.pallas_call(
        _ple_embed_kernel,
        grid=(m // tm,),
        in_specs=[pl.BlockSpec((tm, kdim), lambda i: (i, 0)), pl.BlockSpec((kdim, n), lambda i: (0, 0)),
                  pl.BlockSpec((1, n), lambda i: (0, 0))],
        out_specs=pl.BlockSpec((tm, n), lambda i: (i, 0)),
        out_shape=jax.ShapeDtypeStruct((m, n), F32),
        name="ple_embed",
        compiler_params=_params("parallel"),
    )(p, w, g.reshape(1, n))


def _ssd_kernel(xs_ref, bm_ref, cm_ref, dt_ref, z_ref, wxs_ref, wbm_ref, wcm_ref, bxs_ref, bbm_ref, bcm_ref,
                dtb_ref, alog_ref, d_ref, ng_ref, o_ref,
                st_ref, hxs_ref, hbm_ref, hcm_ref, axs_ref, abm_ref, acm_ref, *, n_chunks):
    tb = n_chunks * CHUNK
    hist = SUBLANES

    @pl.when(pl.program_id(2) == 0)
    def _():
        st_ref[...] = jnp.zeros_like(st_ref)
        for h_ref in (hxs_ref, hbm_ref, hcm_ref):
            h_ref[0:hist, :] = jnp.zeros((hist, h_ref.shape[1]), F32)

    for raw_ref, h_ref, w_ref, b_ref, act_ref in ((xs_ref, hxs_ref, wxs_ref, bxs_ref, axs_ref),
                                                  (bm_ref, hbm_ref, wbm_ref, bbm_ref, abm_ref),
                                                  (cm_ref, hcm_ref, wcm_ref, bcm_ref, acm_ref)):
        h_ref[hist:hist + tb, :] = raw_ref[0]
        for c in range(n_chunks):
            acc = b_ref[...] + w_ref[SSD_CONV - 1:SSD_CONV, :] * raw_ref[0, c * CHUNK:(c + 1) * CHUNK, :]
            for j in range(SSD_CONV - 1):
                start = hist + c * CHUNK - (SSD_CONV - 1) + j
                acc = acc + w_ref[j:j + 1, :] * h_ref[pl.ds(start, CHUNK), :]
            act_ref[c * CHUNK:(c + 1) * CHUNK, :] = _silu(acc)
        h_ref[0:hist, :] = raw_ref[0, tb - hist:tb, :]

    gc = SSD_GROUP_COLS
    g = pl.program_id(1)
    expand = (lax.broadcasted_iota(jnp.int32, (SSD_HEADS, gc), 0)
              == (lax.broadcasted_iota(jnp.int32, (SSD_HEADS, gc), 1) >> 6) + g * SSD_HPG).astype(BF16)
    tri = _tri64()
    li = lax.broadcasted_iota(jnp.int32, (CHUNK, gc), 0)
    si = lax.broadcasted_iota(jnp.int32, (CHUNK, gc), 1) & (CHUNK - 1)
    eye_t = li == si
    causal_t = li >= si
    top = lax.broadcasted_iota(jnp.int32, (CHUNK, LANES), 1) < SSD_HEAD_DIM
    a_row = -jnp.exp(alog_ref[0])
    d_row = d_ref[0]
    ng_row = ng_ref[...]
    dtb_row = dtb_ref[...]

    st = st_ref[...]
    for c in range(n_chunks):
        sl = slice(c * CHUNK, (c + 1) * CHUNK)
        xs = axs_ref[sl, :]
        bm = abm_ref[sl, :]
        cm = acm_ref[sl, :]
        dt = _softplus(dt_ref[0, sl, :] + dtb_row)
        d0, d1, d2 = _split3(dt)
        de = lambda q: jnp.dot(q, expand, preferred_element_type=F32)
        dt_exp = de(d0) + de(d1) + de(d2)
        acs = _cumsum_rows(tri, dt_exp * a_row)
        rowpart = jnp.sum(jnp.where(eye_t, acs, 0.0), axis=0, keepdims=True)
        decay = jnp.exp(jnp.where(causal_t, acs - rowpart, -jnp.inf))
        xdt = xs * dt_exp
        cb2 = _dot_nt(cm, jnp.concatenate([bm, bm], axis=0))
        last = acs[CHUNK - 1:CHUNK, :]
        y_off = _dot(cm, st) * jnp.exp(acs)
        parts = []
        for j in range(gc // LANES):
            lanes = slice(j * LANES, (j + 1) * LANES)
            parts.append(_dot(cb2 * decay[:, lanes], _stack_heads(xdt[:, lanes], top)))
        y = jnp.concatenate(parts, axis=1) + y_off + d_row * xs
        st = st * jnp.exp(last) + _dot_tn(bm, xdt * jnp.exp(last - acs))
        y = y * _silu(z_ref[0, sl, :])
        y = y * lax.rsqrt(jnp.mean(y * y, axis=-1, keepdims=True) + SSD_NORM_EPS)
        o_ref[0, sl, :] = (y * ng_row).astype(o_ref.dtype)
    st_ref[...] = st


def _ssd(zx, small, conv_w, conv_b, dt_bias, a_log, d_skip, norm_g, *, tb=256):
    bsz, t, _ = zx.shape
    gc = SSD_GROUP_COLS
    xs_blk0 = SSD_D_INNER // gc
    bm_blk0 = 2 * SSD_D_INNER // SSD_STATE
    cm_blk0 = bm_blk0 + SSD_GROUPS
    wbm_blk0 = SSD_D_INNER // SSD_STATE
    wcm_blk0 = wbm_blk0 + SSD_GROUPS
    conv_b = conv_b.reshape(1, SSD_CONV_DIM)
    per_group = lambda v: jnp.repeat(v, SSD_HEAD_DIM).reshape(SSD_GROUPS, 1, gc)
    seq = lambda w, f: pl.BlockSpec((1, tb, w), lambda b, g, ti: (b, ti, f(g)))
    chan = lambda r, w, f: pl.BlockSpec((r, w), lambda b, g, ti: (0, f(g)))
    grp = pl.BlockSpec((1, 1, gc), lambda b, g, ti: (g, 0, 0))
    return pl.pallas_call(
        functools.partial(_ssd_kernel, n_chunks=tb // CHUNK),
        grid=(bsz, SSD_GROUPS, t // tb),
        in_specs=[seq(gc, lambda g: xs_blk0 + g), seq(SSD_STATE, lambda g: bm_blk0 + g),
                  seq(SSD_STATE, lambda g: cm_blk0 + g), seq(SSD_HEADS, lambda g: SMALL_DT_BLK),
                  seq(gc, lambda g: g),
                  chan(SSD_CONV, gc, lambda g: g), chan(SSD_CONV, SSD_STATE, lambda g: wbm_blk0 + g),
                  chan(SSD_CONV, SSD_STATE, lambda g: wcm_blk0 + g),
                  chan(1, gc, lambda g: g), chan(1, SSD_STATE, lambda g: wbm_blk0 + g),
                  chan(1, SSD_STATE, lambda g: wcm_blk0 + g),
                  chan(1, SSD_HEADS, lambda g: 0), grp, grp, chan(1, gc, lambda g: g)],
        out_specs=seq(gc, lambda g: g),
        out_shape=jax.ShapeDtypeStruct((bsz, t, SSD_D_INNER), BF16),
        scratch_shapes=[pltpu.VMEM((SSD_STATE, gc), F32),
                        pltpu.VMEM((SUBLANES + tb, gc), F32), pltpu.VMEM((SUBLANES + tb, SSD_STATE), F32),
                        pltpu.VMEM((SUBLANES + tb, SSD_STATE), F32),
                        pltpu.VMEM((tb, gc), F32), pltpu.VMEM((tb, SSD_STATE), F32), pltpu.VMEM((tb, SSD_STATE), F32)],
        name="ssd_scan",
        compiler_params=_params("parallel", "parallel", "arbitrary"),
    )(zx, zx, zx, small, zx, conv_w, conv_w, conv_w, conv_b, conv_b, conv_b, dt_bias.reshape(1, SSD_HEADS),
      per_group(a_log), per_group(d_skip), norm_g.reshape(1, SSD_D_INNER))


def _wkv_kernel(r_ref, k_ref, v_ref, glo_ref, wlo_ref, alo_ref,
                mur_ref, muk_ref, muv_ref, mug_ref, muw_ref, mua_ref,
                w2_ref, a2_ref, g2_ref, w0_ref, a0_ref, kk_ref, ka_ref, rk_ref, lng_ref, lnb_ref,
                o_ref,
                s_ref, prev_ref, plo_ref, rr_ref, kr_ref, vv_ref, lw_ref, kn_ref, ag_ref, gg_ref,
                *, n_chunks, n_pairs):
    tb = n_chunks * CHUNK
    wl = n_pairs * LANES
    glo_w = glo_ref.shape[2]

    @pl.when(pl.program_id(2) == 0)
    def _():
        s_ref[...] = jnp.zeros_like(s_ref)
        prev_ref[...] = jnp.zeros_like(prev_ref)
        plo_ref[...] = jnp.zeros_like(plo_ref)

    first_row = lax.broadcasted_iota(jnp.int32, (tb, 1), 0) == 0

    def lerp(x, prev_row, mu):
        shifted = jnp.where(first_row, prev_row, pltpu.roll(x, 1, 0))
        return x + (shifted - x) * mu

    r_raw, k_raw, v_raw = r_ref[0], k_ref[0], v_ref[0]
    glo_raw, wlo_raw, alo_raw = glo_ref[0], wlo_ref[0], alo_ref[0]
    r = lerp(r_raw, prev_ref[0:1, :], mur_ref[...])
    k = lerp(k_raw, prev_ref[1:2, :], muk_ref[...])
    v = lerp(v_raw, prev_ref[2:3, :], muv_ref[...])
    g_lo = lerp(glo_raw, plo_ref[0:1, 0:glo_w], mug_ref[...])
    w_lo = lerp(wlo_raw, plo_ref[0:1, glo_w:glo_w + W_RANK], muw_ref[...])
    a_lo = lerp(alo_raw, plo_ref[0:1, glo_w + W_RANK:], mua_ref[...])
    prev_ref[0:1, :] = r_raw[tb - 1:tb, :]
    prev_ref[1:2, :] = k_raw[tb - 1:tb, :]
    prev_ref[2:3, :] = v_raw[tb - 1:tb, :]
    plo_ref[0:1, 0:glo_w] = glo_raw[tb - 1:tb, :]
    plo_ref[0:1, glo_w:glo_w + W_RANK] = wlo_raw[tb - 1:tb, :]
    plo_ref[0:1, glo_w + W_RANK:] = alo_raw[tb - 1:tb, :]

    lw = -DECAY_SCALE * _sigmoid(w0_ref[...] + jnp.dot(jnp.tanh(w_lo).astype(BF16), w2_ref[...],
                                                       preferred_element_type=F32))
    a_gate = _sigmoid(a0_ref[...] + jnp.dot(a_lo.astype(BF16), a2_ref[...], preferred_element_type=F32))
    gg_ref[...] = jnp.dot(_sigmoid(g_lo).astype(BF16), g2_ref[...], preferred_element_type=F32)
    top_blk = lax.broadcasted_iota(jnp.int32, (tb, LANES), 1) < RWKV_HEAD_DIM
    kk = k * kk_ref[...]
    for j in range(n_pairs):
        lanes = slice(j * LANES, (j + 1) * LANES)
        kj = kk[:, lanes]
        kn_ref[:, lanes] = kj / jnp.maximum(jnp.sqrt(_head_sum(kj * kj, top_blk)), L2_EPS)
    rr_ref[...] = r
    kr_ref[...] = k * (1.0 + (a_gate - 1.0) * ka_ref[...])
    vv_ref[...] = v
    lw_ref[...] = lw
    ag_ref[...] = a_gate

    row = lax.broadcasted_iota(jnp.int32, (LANES, LANES), 0)
    col = lax.broadcasted_iota(jnp.int32, (LANES, LANES), 1)
    same = (row >> 6) == (col >> 6)
    strict = same & ((row & 63) > (col & 63))
    incl = same & ((row & 63) >= (col & 63))
    eye = (row == col).astype(F32)
    trow = lax.broadcasted_iota(jnp.int32, (tb, tb), 0)
    tcol = lax.broadcasted_iota(jnp.int32, (tb, tb), 1)
    tri_blk = ((trow >> 6) == (tcol >> 6)) & (trow >= tcol)
    top = lax.broadcasted_iota(jnp.int32, (CHUNK, LANES), 1) < RWKV_HEAD_DIM
    inv_n = 1.0 / RWKV_HEAD_DIM
    chains = [(j, c) for j in range(n_pairs) for c in range(n_chunks)]

    def piece(ref, j, c):
        return ref[c * CHUNK:(c + 1) * CHUNK, j * LANES:(j + 1) * LANES]

    def fold(x):
        return x[:CHUNK] + x[CHUNK:]

    cum_all = _cumsum_rows(tri_blk.astype(BF16), lw)

    a_s, r_s, b_s, k_s, v_s, bh_s, kh_s, wtot = [], [], [], [], [], [], [], []
    for j, c in chains:
        cum = cum_all[c * CHUNK:(c + 1) * CHUNK, j * LANES:(j + 1) * LANES]
        lwc = piece(lw_ref, j, c)
        kc = piece(kr_ref, j, c)
        knc = piece(kn_ref, j, c)
        b = knc * piece(ag_ref, j, c)
        tot = cum[CHUNK - 1:CHUNK, :]
        e_neg = jnp.exp(-cum)
        e_end = jnp.exp(tot - cum)
        a_s.append(_stack_heads(-knc * jnp.exp(cum - lwc), top).astype(BF16))
        r_s.append(_stack_heads(piece(rr_ref, j, c) * jnp.exp(cum), top))
        b_s.append(_stack_heads(b * e_neg, top).astype(BF16))
        k_s.append(_stack_heads(kc * e_neg, top).astype(BF16))
        v_s.append(_stack_heads(piece(vv_ref, j, c), top).astype(BF16))
        bh_s.append(_stack_heads(b * e_end, top).astype(BF16))
        kh_s.append(_stack_heads(kc * e_end, top).astype(BF16))
        wtot.append(jnp.exp(tot))
    nb = range(len(chains))
    r_b = [x.astype(BF16) for x in r_s]
    n = [jnp.where(strict, _dot_nt(a_s[i], b_s[i]), 0.0) for i in nb]
    ak = [jnp.where(strict, _dot_nt(a_s[i], k_s[i]), 0.0) for i in nb]
    rb = [jnp.where(incl, _dot_nt(r_b[i], b_s[i]), 0.0).astype(BF16) for i in nb]
    rkm = [jnp.where(incl, _dot_nt(r_b[i], k_s[i]), 0.0) for i in nb]
    tinv = [eye + x for x in n]
    npow = n
    for _ in range(5):
        npow = [_dot(x, x) for x in npow]
        tinv = [tinv[i] + _dot(tinv[i], npow[i]) for i in nb]
    akv = [_dot(ak[i], v_s[i]) for i in nb]
    p = [_dot(tinv[i], a_s[i]).astype(BF16) for i in nb]
    q = [_dot(tinv[i], akv[i]).astype(BF16) for i in nb]
    r_eff = [fold(r_s[i] + _dot(rb[i], p[i])).astype(BF16) for i in nb]
    y0 = [fold(_dot(rb[i], q[i]) + _dot(rkm[i], v_s[i])) for i in nb]
    g_eff = [_dot_tn(p[i], bh_s[i]).astype(BF16) for i in nb]
    s1 = [_dot_tn(q[i], bh_s[i]) + _dot_tn(v_s[i], kh_s[i]) for i in nb]

    for j in range(n_pairs):
        lanes = slice(j * LANES, (j + 1) * LANES)
        rk_row = rk_ref[:, lanes]
        lng_row = lng_ref[:, lanes]
        lnb_row = lnb_ref[:, lanes]
        s = s_ref[j]
        for c in range(n_chunks):
            i = j * n_chunks + c
            y = _dot_nt(r_eff[i], s) + y0[i]
            s = s * wtot[i] + _dot(s, g_eff[i]) + s1[i]
            mean = _head_sum(y, top) * inv_n
            yc = y - mean
            var = _head_sum(yc * yc, top) * inv_n
            yn = yc * lax.rsqrt(var + RWKV_GN_EPS) * lng_row + lnb_row
            bonus = _head_sum(piece(rr_ref, j, c) * piece(kr_ref, j, c) * rk_row, top) * piece(vv_ref, j, c)
            o_ref[0, c * CHUNK:(c + 1) * CHUNK, lanes] = ((yn + bonus) * piece(gg_ref, j, c)).astype(o_ref.dtype)
        s_ref[j] = s


def _wkv(rkv, small, mu, w2, a2, g2p, w0, a0, k_k, k_a, r_k, ln_g, ln_b, *, tb=256, pairs=2):
    bsz, t, d3 = rkv.shape
    d = d3 // 3
    wl = pairs * LANES
    nh = d // wl
    mu_rkv = mu[:3 * d].reshape(1, 3 * d)
    mu_w = mu[3 * d:3 * d + W_RANK].reshape(1, W_RANK)
    mu_a = mu[3 * d + W_RANK:3 * d + W_RANK + A_RANK].reshape(1, A_RANK)
    mu_g = jnp.pad(mu[3 * d + W_RANK + A_RANK:], (0, G_RANK_PAD - G_RANK)).reshape(1, G_RANK_PAD)
    seq = lambda w, f: pl.BlockSpec((1, tb, w), lambda b, h, ti: (b, ti, f(h)))
    chan = lambda r, w, f: pl.BlockSpec((r, w), lambda b, h, ti: (0, f(h)))
    vec = lambda q: q.reshape(1, d)
    blk = pltpu.VMEM((tb, wl), F32)
    return pl.pallas_call(
        functools.partial(_wkv_kernel, n_chunks=tb // CHUNK, n_pairs=pairs),
        grid=(bsz, nh, t // tb),
        in_specs=[seq(wl, lambda h: h), seq(wl, lambda h: nh + h), seq(wl, lambda h: 2 * nh + h),
                  seq(G_RANK_PAD, lambda h: 0), seq(W_RANK, lambda h: SMALL_WLO_BLK),
                  seq(A_RANK, lambda h: SMALL_ALO_BLK),
                  chan(1, wl, lambda h: h), chan(1, wl, lambda h: nh + h), chan(1, wl, lambda h: 2 * nh + h),
                  chan(1, G_RANK_PAD, lambda h: 0), chan(1, W_RANK, lambda h: 0), chan(1, A_RANK, lambda h: 0),
                  chan(W_RANK, wl, lambda h: h), chan(A_RANK, wl, lambda h: h), chan(G_RANK_PAD, wl, lambda h: h)]
                 + [chan(1, wl, lambda h: h)] * 7,
        out_specs=seq(wl, lambda h: h),
        out_shape=jax.ShapeDtypeStruct((bsz, t, d), BF16),
        scratch_shapes=[pltpu.VMEM((pairs, LANES, LANES), F32), pltpu.VMEM((SUBLANES, wl), F32),
                        pltpu.VMEM((SUBLANES, G_RANK_PAD + W_RANK + A_RANK), F32)] + [blk] * 7,
        name="wkv7_scan",
        compiler_params=_params("parallel", "parallel", "arbitrary"),
    )(rkv, rkv, rkv, small, small, small, mu_rkv, mu_rkv, mu_rkv, mu_g, mu_w, mu_a,
      w2, a2, g2p, vec(w0), vec(a0), vec(k_k), vec(k_a), vec(r_k), vec(ln_g), vec(ln_b))


def kernel(x, p, norm_mix_g, w_in, ssd_conv_w, ssd_conv_b, ssd_dt_bias, ssd_a_log, ssd_d, ssd_norm_g, rwkv_mu, rwkv_w0, rwkv_w2, rwkv_a0, rwkv_a2, rwkv_g2, rwkv_k_k, rwkv_k_a, rwkv_r_k, rwkv_ln_g, rwkv_ln_b, w_branch_ssd, w_branch_rwkv, w_out, norm_ffn_g, w_ff1, w_ff2, norm_ple_g, w_ple_gate, w_ple_proj, ple_post_g, final_norm_g):
    bsz, t, d = x.shape
    m = bsz * t
    depth = w_in.shape[0]
    xf = x.reshape(m, d)
    for i in range(depth):
        c_dt = SSD_D_INNER + SSD_CONV_DIM
        c_r = c_dt + SSD_HEADS
        c_wlo = c_r + 3 * d
        c_glo = c_wlo + W_RANK + A_RANK
        c_gate = c_glo + G_RANK
        wi = w_in[i]
        w_zx = wi[:, :c_dt].astype(BF16)
        w_rkv = wi[:, c_r:c_wlo].astype(BF16)
        w_gates = wi[:, c_gate:].astype(BF16)
        w_small = jnp.concatenate(
            [wi[:, c_glo:c_gate], jnp.zeros((d, G_RANK_PAD - G_RANK), wi.dtype),
             wi[:, c_dt:c_r], wi[:, c_wlo:c_glo]], axis=1).astype(BF16)
        g2p = jnp.pad(rwkv_g2[i], ((0, G_RANK_PAD - G_RANK), (0, 0))).astype(BF16)

        h = _rmsnorm(xf, norm_mix_g[i], BF16)
        zx = _mm(h, w_zx, name="proj_zx", out_dtype=F32, tm=1024, tn=512).reshape(bsz, t, -1)
        rkv = _mm(h, w_rkv, name="proj_rkv", out_dtype=F32, tm=1024, tn=512).reshape(bsz, t, -1)
        gates = _mm(h, w_gates, name="proj_gates", out_dtype=F32, tm=1024, tn=512)
        small = _mm(h, w_small, name="proj_small", out_dtype=F32, tm=1024, tn=SMALL_COLS).reshape(bsz, t, -1)

        u_ssd = _ssd(zx, small, ssd_conv_w[i], ssd_conv_b[i], ssd_dt_bias[i], ssd_a_log[i], ssd_d[i],
                     ssd_norm_g[i])
        u_rwkv = _wkv(rkv, small, rwkv_mu[i], rwkv_w2[i].astype(BF16), rwkv_a2[i].astype(BF16), g2p,
                      rwkv_w0[i], rwkv_a0[i], rwkv_k_k[i], rwkv_k_a[i], rwkv_r_k[i], rwkv_ln_g[i], rwkv_ln_b[i])

        part = _mm(u_ssd.reshape(m, SSD_D_INNER), w_branch_ssd[i].astype(BF16), name="branch_ssd", out_dtype=F32,
                   tm=1024, tn=512, tk=4096, mul=gates, mul_act="sigmoid", mul_col0=0)
        merged = _mm(u_rwkv.reshape(m, d), w_branch_rwkv[i].astype(BF16), name="branch_rwkv", out_dtype=BF16,
                     tm=1024, tn=512, mul=gates, mul_act="sigmoid", mul_col0=d, res=part)
        xf = _mm(merged, w_out[i].astype(BF16), name="out_proj", out_dtype=F32, tm=1024, tn=512, res=xf)

        h = _rmsnorm(xf, norm_ffn_g[i], BF16)
        ff = _mm(h, w_ff1[i].astype(BF16), name="ffn_up", out_dtype=BF16, tm=1024, tn=512, act="relu2")
        xf = _mm(ff, w_ff2[i].astype(BF16), name="ffn_down", out_dtype=F32, tm=1024, tn=512, tk=4096, res=xf)

        h = _rmsnorm(xf, norm_ple_g[i], BF16)
        e = _ple_embed(p[i].reshape(m, PLE_DIM).astype(BF16), w_ple_proj[i].astype(BF16), ple_post_g[i])
        xf = _mm(h, w_ple_gate[i].astype(BF16), name="ple_gate", out_dtype=F32, tm=1024, tn=512, act="sigmoid",
                 mul=e, res=xf)
    return _rmsnorm(xf, final_norm_g, F32).reshape(bsz, t, d)
```

```python
import functools
import math

import jax
import jax.numpy as jnp
from jax import lax
from jax.experimental import pallas as pl
from jax.experimental.pallas import tpu as pltpu

F32 = jnp.float32
BF16 = jnp.bfloat16

CHUNK = 64
SSD_D_INNER = 8192
SSD_HEAD_DIM = 64
SSD_HEADS = 128
SSD_GROUPS = 8
SSD_HPG = 16
SSD_STATE = 128
SSD_CONV = 4
SSD_GROUP_COLS = SSD_D_INNER // SSD_GROUPS
SSD_CONV_DIM = SSD_D_INNER + 2 * SSD_GROUPS * SSD_STATE
SSD_NORM_EPS = 1e-5
RWKV_HEAD_DIM = 64
W_RANK = 128
A_RANK = 128
G_RANK = 480
G_RANK_PAD = 512
DECAY_SCALE = math.exp(-0.5)
RWKV_GN_EPS = RWKV_HEAD_DIM * 1e-5
L2_EPS = 1e-12
NORM_EPS = 1e-6
PLE_DIM = 256

SMALL_COLS = G_RANK_PAD + SSD_HEADS + W_RANK + A_RANK
SMALL_LO_COLS = W_RANK + A_RANK + G_RANK_PAD
SMALL_DT_BLK = G_RANK_PAD // 128
SMALL_WLO_BLK = SMALL_DT_BLK + 1
SMALL_ALO_BLK = SMALL_DT_BLK + 2

LANES = 128
SUBLANES = 8
VMEM_LIMIT_BYTES = 56 * 1024 * 1024


def _params(*sem):
    return pltpu.CompilerParams(dimension_semantics=sem, vmem_limit_bytes=VMEM_LIMIT_BYTES)


def _split3(x):
    hi = x.astype(BF16)
    r1 = x - hi.astype(F32)
    mid = r1.astype(BF16)
    lo = (r1 - mid.astype(F32)).astype(BF16)
    return hi, mid, lo


def _dot(a, b, dims=(((1,), (0,)), ((), ()))):
    return lax.dot_general(a.astype(BF16), b.astype(BF16), dims, preferred_element_type=F32)


def _dot_nt(a, b):
    return _dot(a, b, (((1,), (1,)), ((), ())))


def _dot_tn(a, b):
    return _dot(a, b, (((0,), (0,)), ((), ())))


def _cumsum_rows(tri_bf16, x):
    x0, x1, x2 = _split3(x)
    d = lambda q: jnp.dot(tri_bf16, q, preferred_element_type=F32)
    return d(x0) + d(x1) + d(x2)


def _sigmoid(x):
    return 0.5 * jnp.tanh(0.5 * x) + 0.5


def _silu(x):
    return x * _sigmoid(x)


def _softplus(x):
    return jnp.maximum(x, 0.0) + jnp.log1p(jnp.exp(-jnp.abs(x)))


def _stack_heads(x, top):
    return jnp.concatenate([jnp.where(top, x, 0.0), jnp.where(top, 0.0, x)], axis=0)


def _head_sum(x, top):
    s0 = jnp.sum(jnp.where(top, x, 0.0), axis=-1, keepdims=True)
    s1 = jnp.sum(jnp.where(top, 0.0, x), axis=-1, keepdims=True)
    return jnp.where(top, s0, s1)


def _rmsnorm_kernel(x_ref, g_ref, o_ref, *, eps):
    x = x_ref[...]
    y = x * lax.rsqrt(jnp.mean(x * x, axis=-1, keepdims=True) + eps)
    o_ref[...] = (y * g_ref[...]).astype(o_ref.dtype)


def _rmsnorm(x, g, out_dtype, tm=256):
    m, d = x.shape
    return pl.pallas_call(
        functools.partial(_rmsnorm_kernel, eps=NORM_EPS),
        grid=(m // tm,),
        in_specs=[pl.BlockSpec((tm, d), lambda i: (i, 0)), pl.BlockSpec((1, d), lambda i: (0, 0))],
        out_specs=pl.BlockSpec((tm, d), lambda i: (i, 0)),
        out_shape=jax.ShapeDtypeStruct((m, d), out_dtype),
        name="rmsnorm",
        compiler_params=_params("parallel"),
    )(x, g.reshape(1, d))


def _act(x, kind):
    if kind is None:
        return x
    if kind == "relu2":
        r = jnp.maximum(x, 0.0)
        return r * r
    if kind == "sigmoid":
        return _sigmoid(x)
    raise ValueError(kind)


def _mm_kernel(*refs, nk, act, has_mul, mul_act, has_res):
    x_ref, w_ref = refs[0], refs[1]
    pos = 2
    mul_ref = res_ref = None
    if has_mul:
        mul_ref = refs[pos]
        pos += 1
    if has_res:
        res_ref = refs[pos]
        pos += 1
    o_ref = refs[pos]
    acc_ref = None if nk == 1 else (o_ref if o_ref.dtype == F32 else refs[pos + 1])

    part = jnp.dot(x_ref[...], w_ref[...], preferred_element_type=F32)

    def finish(acc):
        out = _act(acc, act)
        if has_mul:
            out = out * _act(mul_ref[...].astype(F32), mul_act)
        if has_res:
            out = res_ref[...].astype(F32) + out
        o_ref[...] = out.astype(o_ref.dtype)

    if nk == 1:
        finish(part)
    else:
        k = pl.program_id(2)

        @pl.when(k == 0)
        def _():
            acc_ref[...] = part

        @pl.when((k > 0) & (k < nk - 1))
        def _():
            acc_ref[...] += part

        @pl.when(k == nk - 1)
        def _():
            finish(acc_ref[...] + part)


def _mm(x, w, *, name, out_dtype, tm, tn, tk=None, act=None, mul=None, mul_act=None, mul_col0=0,
        res=None, res_col0=0):
    m, kdim = x.shape
    n = w.shape[1]
    tk = kdim if tk is None else tk
    tm, tn = min(tm, m), min(tn, n)
    nk = kdim // tk
    assert m % tm == 0 and n % tn == 0 and kdim % tk == 0 and mul_col0 % tn == 0 and res_col0 % tn == 0
    in_specs = [pl.BlockSpec((tm, tk), lambda i, j, k: (i, k)),
                pl.BlockSpec((tk, tn), lambda i, j, k: (k, j))]
    args = [x, w]
    if mul is not None:
        off = mul_col0 // tn
        in_specs.append(pl.BlockSpec((tm, tn), lambda i, j, k, off=off: (i, j + off)))
        args.append(mul)
    if res is not None:
        off = res_col0 // tn
        in_specs.append(pl.BlockSpec((tm, tn), lambda i, j, k, off=off: (i, j + off)))
        args.append(res)
    return pl.pallas_call(
        functools.partial(_mm_kernel, nk=nk, act=act, has_mul=mul is not None, mul_act=mul_act,
                          has_res=res is not None),
        grid=(m // tm, n // tn, nk),
        in_specs=in_specs,
        out_specs=pl.BlockSpec((tm, tn), lambda i, j, k: (i, j)),
        out_shape=jax.ShapeDtypeStruct((m, n), out_dtype),
        scratch_shapes=[pltpu.VMEM((tm, tn), F32)] if nk > 1 and out_dtype != F32 else [],
        name=name,
        compiler_params=_params("parallel", "parallel", "arbitrary"),
    )(*args)


def _mm_wres_kernel(*refs, act, has_mul, mul_act, has_res):
    x_ref, w_ref = refs[0], refs[1]
    pos = 2
    mul_ref = res_ref = None
    if has_mul:
        mul_ref = refs[pos]
        pos += 1
    if has_res:
        res_ref = refs[pos]
        pos += 1
    o_ref, wb_ref = refs[pos], refs[pos + 1]

    @pl.when(pl.program_id(1) == 0)
    def _():
        wb_ref[...] = w_ref[...].reshape(wb_ref.shape).astype(BF16)

    out = _act(jnp.dot(x_ref[...], wb_ref[...], preferred_element_type=F32), act)
    if has_mul:
        out = out * _act(mul_ref[...].astype(F32), mul_act)
    if has_res:
        out = res_ref[...].astype(F32) + out
    o_ref[...] = out.astype(o_ref.dtype)


def _mm_wres(x, w, *, name, n, layer=0, w_col0=0, out_dtype, tm, tn, act=None, mul=None, mul_act=None, mul_col0=0,
             res=None):
    m, kdim = x.shape
    assert w.shape[-2] == kdim and m % tm == 0 and n % tn == 0 and w_col0 % LANES == 0 and mul_col0 % tn == 0
    lead = w.ndim - 2
    w_block = (pl.Element(1),) * lead + (pl.Element(kdim), pl.Element(tn))
    w_map = lambda j, i: (layer,) * lead + (0, pl.multiple_of(w_col0 + j * tn, LANES))
    in_specs = [pl.BlockSpec((tm, kdim), lambda j, i: (i, 0)), pl.BlockSpec(w_block, w_map)]
    args = [x, w]
    if mul is not None:
        off = mul_col0 // tn
        in_specs.append(pl.BlockSpec((tm, tn), lambda j, i, off=off: (i, j + off)))
        args.append(mul)
    if res is not None:
        in_specs.append(pl.BlockSpec((tm, tn), lambda j, i: (i, j)))
        args.append(res)
    return pl.pallas_call(
        functools.partial(_mm_wres_kernel, act=act, has_mul=mul is not None, mul_act=mul_act,
                          has_res=res is not None),
        grid=(n // tn, m // tm),
        in_specs=in_specs,
        out_specs=pl.BlockSpec((tm, tn), lambda j, i: (i, j)),
        out_shape=jax.ShapeDtypeStruct((m, n), out_dtype),
        scratch_shapes=[pltpu.VMEM((kdim, tn), BF16)],
        name=name,
        compiler_params=_params("parallel", "arbitrary"),
    )(*args)


def _proj_small_kernel(x_ref, wdt_ref, wlo_ref, o_ref):
    x = x_ref[...]
    kdim = x.shape[1]
    dt = jnp.dot(x, wdt_ref[...].reshape(kdim, SSD_HEADS).astype(BF16), preferred_element_type=F32)
    lo = jnp.dot(x, wlo_ref[...].reshape(kdim, SMALL_LO_COLS).astype(BF16), preferred_element_type=F32)
    o_ref[:, 0:G_RANK_PAD] = lo[:, W_RANK + A_RANK:]
    o_ref[:, G_RANK_PAD:G_RANK_PAD + SSD_HEADS] = dt
    o_ref[:, G_RANK_PAD + SSD_HEADS:] = lo[:, :W_RANK + A_RANK]


def _proj_small(x, w, *, layer, dt_col0, lo_col0, tm=512):
    m, kdim = x.shape
    lead = w.ndim - 2
    win = lambda width, col0: pl.BlockSpec((pl.Element(1),) * lead + (pl.Element(kdim), pl.Element(width)),
                                           lambda i: (layer,) * lead + (0, col0))
    return pl.pallas_call(
        _proj_small_kernel,
        grid=(m // tm,),
        in_specs=[pl.BlockSpec((tm, kdim), lambda i: (i, 0)), win(SSD_HEADS, dt_col0), win(SMALL_LO_COLS, lo_col0)],
        out_specs=pl.BlockSpec((tm, SMALL_COLS), lambda i: (i, 0)),
        out_shape=jax.ShapeDtypeStruct((m, SMALL_COLS), F32),
        name="proj_small",
        compiler_params=_params("parallel"),
    )(x, w, w)


def _ple_embed_kernel(p_ref, w_ref, g_ref, o_ref):
    e = jnp.dot(p_ref[...], w_ref[...], preferred_element_type=F32)
    y = e * lax.rsqrt(jnp.mean(e * e, axis=-1, keepdims=True) + NORM_EPS)
    o_ref[...] = (y * g_ref[...]).astype(o_ref.dtype)


def _ple_embed(p, w, g, tm=256):
    m, kdim = p.shape
    n = w.shape[1]
    return pl.pallas_call(
        _ple_embed_kernel,
        grid=(m // tm,),
        in_specs=[pl.BlockSpec((tm, kdim), lambda i: (i, 0)), pl.BlockSpec((kdim, n), lambda i: (0, 0)),
                  pl.BlockSpec((1, n), lambda i: (0, 0))],
        out_specs=pl.BlockSpec((tm, n), lambda i: (i, 0)),
        out_shape=jax.ShapeDtypeStruct((m, n), F32),
        name="ple_embed",
        compiler_params=_params("parallel"),
    )(p, w, g.reshape(1, n))


def _ssd_kernel(xs_ref, bm_ref, cm_ref, dt_ref, z_ref, wxs_ref, wbm_ref, wcm_ref, bxs_ref, bbm_ref, bcm_ref,
                dtb_ref, alog_ref, d_ref, ng_ref, o_ref,
                st_ref, hxs_ref, hbm_ref, hcm_ref, axs_ref, abm_ref, acm_ref, *, n_chunks):
    tb = n_chunks * CHUNK
    hist = SUBLANES

    @pl.when(pl.program_id(2) == 0)
    def _():
        st_ref[...] = jnp.zeros_like(st_ref)
        for h_ref in (hxs_ref, hbm_ref, hcm_ref):
            h_ref[0:hist, :] = jnp.zeros((hist, h_ref.shape[1]), F32)

    for raw_ref, h_ref, w_ref, b_ref, act_ref in ((xs_ref, hxs_ref, wxs_ref, bxs_ref, axs_ref),
                                                  (bm_ref, hbm_ref, wbm_ref, bbm_ref, abm_ref),
                                                  (cm_ref, hcm_ref, wcm_ref, bcm_ref, acm_ref)):
        h_ref[hist:hist + tb, :] = raw_ref[0]
        hv = h_ref[...]
        acc = b_ref[...] + w_ref[SSD_CONV - 1:SSD_CONV, :] * hv[hist:]
        for j in range(SSD_CONV - 1):
            acc = acc + w_ref[j:j + 1, :] * pltpu.roll(hv, SSD_CONV - 1 - j, 0)[hist:]
        act_ref[...] = _silu(acc)
        h_ref[0:hist, :] = raw_ref[0, tb - hist:tb, :]

    gc = SSD_GROUP_COLS
    g = pl.program_id(1)
    expand = (lax.broadcasted_iota(jnp.int32, (SSD_HEADS, gc), 0)
              == (lax.broadcasted_iota(jnp.int32, (SSD_HEADS, gc), 1) >> 6) + g * SSD_HPG).astype(BF16)
    li = lax.broadcasted_iota(jnp.int32, (CHUNK, gc), 0)
    si = lax.broadcasted_iota(jnp.int32, (CHUNK, gc), 1) & (CHUNK - 1)
    eye_t = li == si
    causal_t = li >= si
    top = lax.broadcasted_iota(jnp.int32, (CHUNK, LANES), 1) < SSD_HEAD_DIM
    d_row = d_ref[0]
    ng_row = ng_ref[...]
    trow = lax.broadcasted_iota(jnp.int32, (tb, tb), 0)
    tcol = lax.broadcasted_iota(jnp.int32, (tb, tb), 1)
    tri_blk = (((trow >> 6) == (tcol >> 6)) & (trow >= tcol)).astype(BF16)
    dt_all = _softplus(dt_ref[0] + dtb_ref[...])
    acs_all = _cumsum_rows(tri_blk, dt_all * -jnp.exp(alog_ref[...]))

    def expand_heads(q):
        q0, q1, q2 = _split3(q)
        de = lambda r: jnp.dot(r, expand, preferred_element_type=F32)
        return de(q0) + de(q1) + de(q2)

    st = st_ref[...]
    for c in range(n_chunks):
        sl = slice(c * CHUNK, (c + 1) * CHUNK)
        xs = axs_ref[sl, :]
        bm = abm_ref[sl, :]
        cm = acm_ref[sl, :]
        dt_exp = expand_heads(dt_all[sl, :])
        acs = expand_heads(acs_all[sl, :])
        rowpart = jnp.sum(jnp.where(eye_t, acs, 0.0), axis=0, keepdims=True)
        decay = jnp.exp(jnp.where(causal_t, acs - rowpart, -jnp.inf))
        xdt = xs * dt_exp
        cb2 = _dot_nt(cm, jnp.concatenate([bm, bm], axis=0))
        last = acs[CHUNK - 1:CHUNK, :]
        y_off = _dot(cm, st) * jnp.exp(acs)
        parts = []
        for j in range(gc // LANES):
            lanes = slice(j * LANES, (j + 1) * LANES)
            parts.append(_dot(cb2 * decay[:, lanes], _stack_heads(xdt[:, lanes], top)))
        y = jnp.concatenate(parts, axis=1) + y_off + d_row * xs
        st = st * jnp.exp(last) + _dot_tn(bm, xdt * jnp.exp(last - acs))
        y = y * _silu(z_ref[0, sl, :])
        y = y * lax.rsqrt(jnp.mean(y * y, axis=-1, keepdims=True) + SSD_NORM_EPS)
        o_ref[0, sl, :] = (y * ng_row).astype(o_ref.dtype)
    st_ref[...] = st


def _ssd(zx, small, conv_w, conv_b, dt_bias, a_log, d_skip, norm_g, *, tb=256):
    bsz, t, _ = zx.shape
    gc = SSD_GROUP_COLS
    xs_blk0 = SSD_D_INNER // gc
    bm_blk0 = 2 * SSD_D_INNER // SSD_STATE
    cm_blk0 = bm_blk0 + SSD_GROUPS
    wbm_blk0 = SSD_D_INNER // SSD_STATE
    wcm_blk0 = wbm_blk0 + SSD_GROUPS
    conv_b = conv_b.reshape(1, SSD_CONV_DIM)
    per_group = lambda v: jnp.repeat(v, SSD_HEAD_DIM).reshape(SSD_GROUPS, 1, gc)
    seq = lambda w, f: pl.BlockSpec((1, tb, w), lambda b, g, ti: (b, ti, f(g)))
    chan = lambda r, w, f: pl.BlockSpec((r, w), lambda b, g, ti: (0, f(g)))
    grp = pl.BlockSpec((1, 1, gc), lambda b, g, ti: (g, 0, 0))
    return pl.pallas_call(
        functools.partial(_ssd_kernel, n_chunks=tb // CHUNK),
        grid=(bsz, SSD_GROUPS, t // tb),
        in_specs=[seq(gc, lambda g: xs_blk0 + g), seq(SSD_STATE, lambda g: bm_blk0 + g),
                  seq(SSD_STATE, lambda g: cm_blk0 + g), seq(SSD_HEADS, lambda g: SMALL_DT_BLK),
                  seq(gc, lambda g: g),
                  chan(SSD_CONV, gc, lambda g: g), chan(SSD_CONV, SSD_STATE, lambda g: wbm_blk0 + g),
                  chan(SSD_CONV, SSD_STATE, lambda g: wcm_blk0 + g),
                  chan(1, gc, lambda g: g), chan(1, SSD_STATE, lambda g: wbm_blk0 + g),
                  chan(1, SSD_STATE, lambda g: wcm_blk0 + g),
                  chan(1, SSD_HEADS, lambda g: 0), chan(1, SSD_HEADS, lambda g: 0), grp,
                  chan(1, gc, lambda g: g)],
        out_specs=seq(gc, lambda g: g),
        out_shape=jax.ShapeDtypeStruct((bsz, t, SSD_D_INNER), BF16),
        scratch_shapes=[pltpu.VMEM((SSD_STATE, gc), F32),
                        pltpu.VMEM((SUBLANES + tb, gc), F32), pltpu.VMEM((SUBLANES + tb, SSD_STATE), F32),
                        pltpu.VMEM((SUBLANES + tb, SSD_STATE), F32),
                        pltpu.VMEM((tb, gc), F32), pltpu.VMEM((tb, SSD_STATE), F32), pltpu.VMEM((tb, SSD_STATE), F32)],
        name="ssd_scan",
        compiler_params=_params("parallel", "parallel", "arbitrary"),
    )(zx, zx, zx, small, zx, conv_w, conv_w, conv_w, conv_b, conv_b, conv_b, dt_bias.reshape(1, SSD_HEADS),
      a_log.reshape(1, SSD_HEADS), per_group(d_skip), norm_g.reshape(1, SSD_D_INNER))


def _wkv_kernel(r_ref, k_ref, v_ref, glo_ref, wlo_ref, alo_ref,
                mur_ref, muk_ref, muv_ref, mug_ref, muw_ref, mua_ref,
                w2_ref, a2_ref, g2_ref, w0_ref, a0_ref, kk_ref, ka_ref, rk_ref, lng_ref, lnb_ref,
                o_ref,
                s_ref, prev_ref, plo_ref, rr_ref, kr_ref, vv_ref, lw_ref, kn_ref, ag_ref, gg_ref,
                *, n_chunks, n_pairs):
    tb = n_chunks * CHUNK
    glo_w = glo_ref.shape[2]

    @pl.when(pl.program_id(2) == 0)
    def _():
        s_ref[...] = jnp.zeros_like(s_ref)
        prev_ref[...] = jnp.zeros_like(prev_ref)
        plo_ref[...] = jnp.zeros_like(plo_ref)

    first_row = lax.broadcasted_iota(jnp.int32, (tb, 1), 0) == 0

    def lerp(x, prev_row, mu):
        shifted = jnp.where(first_row, prev_row, pltpu.roll(x, 1, 0))
        return x + (shifted - x) * mu

    r_raw, k_raw, v_raw = r_ref[0], k_ref[0], v_ref[0]
    glo_raw, wlo_raw, alo_raw = glo_ref[0], wlo_ref[0], alo_ref[0]
    r = lerp(r_raw, prev_ref[0:1, :], mur_ref[...])
    k = lerp(k_raw, prev_ref[1:2, :], muk_ref[...])
    v = lerp(v_raw, prev_ref[2:3, :], muv_ref[...])
    g_lo = lerp(glo_raw, plo_ref[0:1, 0:glo_w], mug_ref[...])
    w_lo = lerp(wlo_raw, plo_ref[0:1, glo_w:glo_w + W_RANK], muw_ref[...])
    a_lo = lerp(alo_raw, plo_ref[0:1, glo_w + W_RANK:], mua_ref[...])
    prev_ref[0:1, :] = r_raw[tb - 1:tb, :]
    prev_ref[1:2, :] = k_raw[tb - 1:tb, :]
    prev_ref[2:3, :] = v_raw[tb - 1:tb, :]
    plo_ref[0:1, 0:glo_w] = glo_raw[tb - 1:tb, :]
    plo_ref[0:1, glo_w:glo_w + W_RANK] = wlo_raw[tb - 1:tb, :]
    plo_ref[0:1, glo_w + W_RANK:] = alo_raw[tb - 1:tb, :]

    lw = -DECAY_SCALE * _sigmoid(w0_ref[...] + jnp.dot(jnp.tanh(w_lo).astype(BF16), w2_ref[...],
                                                       preferred_element_type=F32))
    a_gate = _sigmoid(a0_ref[...] + jnp.dot(a_lo.astype(BF16), a2_ref[...], preferred_element_type=F32))
    gg_ref[...] = jnp.dot(_sigmoid(g_lo).astype(BF16), g2_ref[...], preferred_element_type=F32)
    top_blk = lax.broadcasted_iota(jnp.int32, (tb, LANES), 1) < RWKV_HEAD_DIM
    kk = k * kk_ref[...]
    for j in range(n_pairs):
        lanes = slice(j * LANES, (j + 1) * LANES)
        kj = kk[:, lanes]
        kn_ref[:, lanes] = kj / jnp.maximum(jnp.sqrt(_head_sum(kj * kj, top_blk)), L2_EPS)
    rr_ref[...] = r
    kr_ref[...] = k * (1.0 + (a_gate - 1.0) * ka_ref[...])
    vv_ref[...] = v
    lw_ref[...] = lw
    ag_ref[...] = a_gate

    row = lax.broadcasted_iota(jnp.int32, (LANES, LANES), 0)
    col = lax.broadcasted_iota(jnp.int32, (LANES, LANES), 1)
    same = (row >> 6) == (col >> 6)
    trow = lax.broadcasted_iota(jnp.int32, (tb, tb), 0)
    tcol = lax.broadcasted_iota(jnp.int32, (tb, tb), 1)
    tri_blk = ((trow >> 6) == (tcol >> 6)) & (trow >= tcol)
    top = lax.broadcasted_iota(jnp.int32, (CHUNK, LANES), 1) < RWKV_HEAD_DIM
    inv_n = 1.0 / RWKV_HEAD_DIM
    chains = [(j, c) for j in range(n_pairs) for c in range(n_chunks)]

    def piece(ref, j, c):
        return ref[c * CHUNK:(c + 1) * CHUNK, j * LANES:(j + 1) * LANES]

    cum_all = _cumsum_rows(tri_blk.astype(BF16), lw)

    lane_t = lax.broadcasted_iota(jnp.int32, (CHUNK, LANES), 1) & (CHUNK - 1)
    row_t = lax.broadcasted_iota(jnp.int32, (CHUNK, LANES), 0)
    strict_c = row_t > lane_t
    incl_c = row_t >= lane_t
    eye_c = (row_t == lane_t).astype(F32)
    strict2 = jnp.concatenate([strict_c, incl_c], axis=0)

    def bd(x):
        return _stack_heads(x.astype(BF16), top)

    nb = range(len(chains))
    a_c, r_c, b_bd, k_bd, v_c, bh_c, kh_c, wtot = [], [], [], [], [], [], [], []
    for j, c in chains:
        cum = cum_all[c * CHUNK:(c + 1) * CHUNK, j * LANES:(j + 1) * LANES]
        lwc = piece(lw_ref, j, c)
        kc = piece(kr_ref, j, c)
        knc = piece(kn_ref, j, c)
        b = knc * piece(ag_ref, j, c)
        tot = cum[CHUNK - 1:CHUNK, :]
        e_neg = jnp.exp(-cum)
        e_end = jnp.exp(tot - cum)
        a_c.append((-knc * jnp.exp(cum - lwc)).astype(BF16))
        r_c.append(piece(rr_ref, j, c) * jnp.exp(cum))
        b_bd.append(bd(b * e_neg))
        k_bd.append(bd(kc * e_neg))
        v_c.append(piece(vv_ref, j, c).astype(BF16))
        bh_c.append((b * e_end).astype(BF16))
        kh_c.append((kc * e_end).astype(BF16))
        wtot.append(jnp.exp(tot))
    ar = [jnp.concatenate([a_c[i], r_c[i].astype(BF16)], axis=0) for i in nb]
    sb = [jnp.where(strict2, _dot_nt(ar[i], b_bd[i]), 0.0) for i in nb]
    sk = [jnp.where(strict2, _dot_nt(ar[i], k_bd[i]), 0.0) for i in nb]
    n = [x[:CHUNK] for x in sb]
    rb = [x[CHUNK:].astype(BF16) for x in sb]
    v_bd = [bd(x) for x in v_c]
    tinv = [eye_c + x for x in n]
    p_bd = [bd(x) for x in n]
    npow = [_dot(n[i], p_bd[i]) for i in nb]
    for it in range(1, 6):
        p_bd = [bd(x) for x in npow]
        if it < 5:
            both = [_dot(jnp.concatenate([npow[i], tinv[i]], axis=0), p_bd[i]) for i in nb]
            npow = [x[:CHUNK] for x in both]
            tinv = [tinv[i] + both[i][CHUNK:] for i in nb]
        else:
            tinv = [tinv[i] + _dot(tinv[i], p_bd[i]) for i in nb]
    akv = [_dot(sk[i][:CHUNK], v_bd[i]) for i in nb]
    pq = [_dot(tinv[i], jnp.concatenate([bd(a_c[i]), bd(akv[i])], axis=1)) for i in nb]
    pq_bd = [jnp.concatenate([bd(x[:, :LANES]), bd(x[:, LANES:])], axis=1) for x in pq]
    rpq = [_dot(rb[i], pq_bd[i]) for i in nb]
    r_eff = [(r_c[i] + rpq[i][:, :LANES]).astype(BF16) for i in nb]
    y0 = [rpq[i][:, LANES:] + _dot(sk[i][CHUNK:], v_bd[i]) for i in nb]
    g_eff = [jnp.where(same, _dot_tn(pq[i][:, :LANES], bh_c[i]), 0.0).astype(BF16) for i in nb]
    s1 = [jnp.where(same, _dot_tn(jnp.concatenate([pq[i][:, LANES:].astype(BF16), v_c[i]], axis=0),
                                  jnp.concatenate([bh_c[i], kh_c[i]], axis=0)), 0.0) for i in nb]

    for j in range(n_pairs):
        lanes = slice(j * LANES, (j + 1) * LANES)
        rk_row = rk_ref[:, lanes]
        lng_row = lng_ref[:, lanes]
        lnb_row = lnb_ref[:, lanes]
        s = s_ref[j]
        for c in range(n_chunks):
            i = j * n_chunks + c
            y = _dot_nt(r_eff[i], s) + y0[i]
            s = s * wtot[i] + _dot(s, g_eff[i]) + s1[i]
            mean = _head_sum(y, top) * inv_n
            yc = y - mean
            var = _head_sum(yc * yc, top) * inv_n
            yn = yc * lax.rsqrt(var + RWKV_GN_EPS) * lng_row + lnb_row
            bonus = _head_sum(piece(rr_ref, j, c) * piece(kr_ref, j, c) * rk_row, top) * piece(vv_ref, j, c)
            o_ref[0, c * CHUNK:(c + 1) * CHUNK, lanes] = ((yn + bonus) * piece(gg_ref, j, c)).astype(o_ref.dtype)
        s_ref[j] = s


def _wkv(rkv, small, mu, w2, a2, g2p, w0, a0, k_k, k_a, r_k, ln_g, ln_b, *, tb=128, pairs=8):
    bsz, t, d3 = rkv.shape
    d = d3 // 3
    wl = pairs * LANES
    nh = d // wl
    mu_rkv = mu[:3 * d].reshape(1, 3 * d)
    mu_w = mu[3 * d:3 * d + W_RANK].reshape(1, W_RANK)
    mu_a = mu[3 * d + W_RANK:3 * d + W_RANK + A_RANK].reshape(1, A_RANK)
    mu_g = jnp.pad(mu[3 * d + W_RANK + A_RANK:], (0, G_RANK_PAD - G_RANK)).reshape(1, G_RANK_PAD)
    seq = lambda w, f: pl.BlockSpec((1, tb, w), lambda b, h, ti: (b, ti, f(h)))
    chan = lambda r, w, f: pl.BlockSpec((r, w), lambda b, h, ti: (0, f(h)))
    vec = lambda q: q.reshape(1, d)
    blk = pltpu.VMEM((tb, wl), F32)
    return pl.pallas_call(
        functools.partial(_wkv_kernel, n_chunks=tb // CHUNK, n_pairs=pairs),
        grid=(bsz, nh, t // tb),
        in_specs=[seq(wl, lambda h: h), seq(wl, lambda h: nh + h), seq(wl, lambda h: 2 * nh + h),
                  seq(G_RANK_PAD, lambda h: 0), seq(W_RANK, lambda h: SMALL_WLO_BLK),
                  seq(A_RANK, lambda h: SMALL_ALO_BLK),
                  chan(1, wl, lambda h: h), chan(1, wl, lambda h: nh + h), chan(1, wl, lambda h: 2 * nh + h),
                  chan(1, G_RANK_PAD, lambda h: 0), chan(1, W_RANK, lambda h: 0), chan(1, A_RANK, lambda h: 0),
                  chan(W_RANK, wl, lambda h: h), chan(A_RANK, wl, lambda h: h), chan(G_RANK_PAD, wl, lambda h: h)]
                 + [chan(1, wl, lambda h: h)] * 7,
        out_specs=seq(wl, lambda h: h),
        out_shape=jax.ShapeDtypeStruct((bsz, t, d), BF16),
        scratch_shapes=[pltpu.VMEM((pairs, LANES, LANES), F32), pltpu.VMEM((SUBLANES, wl), F32),
                        pltpu.VMEM((SUBLANES, G_RANK_PAD + W_RANK + A_RANK), F32)] + [blk] * 7,
        name="wkv7_scan",
        compiler_params=_params("parallel", "parallel", "arbitrary"),
    )(rkv, rkv, rkv, small, small, small, mu_rkv, mu_rkv, mu_rkv, mu_g, mu_w, mu_a,
      w2, a2, g2p, vec(w0), vec(a0), vec(k_k), vec(k_a), vec(r_k), vec(ln_g), vec(ln_b))


def kernel(x, p, norm_mix_g, w_in, ssd_conv_w, ssd_conv_b, ssd_dt_bias, ssd_a_log, ssd_d, ssd_norm_g, rwkv_mu, rwkv_w0, rwkv_w2, rwkv_a0, rwkv_a2, rwkv_g2, rwkv_k_k, rwkv_k_a, rwkv_r_k, rwkv_ln_g, rwkv_ln_b, w_branch_ssd, w_branch_rwkv, w_out, norm_ffn_g, w_ff1, w_ff2, norm_ple_g, w_ple_gate, w_ple_proj, ple_post_g, final_norm_g):
    bsz, t, d = x.shape
    m = bsz * t
    depth = w_in.shape[0]
    xf = x.reshape(m, d)
    for i in range(depth):
        c_dt = SSD_D_INNER + SSD_CONV_DIM
        c_r = c_dt + SSD_HEADS
        c_wlo = c_r + 3 * d
        c_glo = c_wlo + W_RANK + A_RANK
        c_gate = c_glo + G_RANK
        w_gates = w_in[i][:, c_gate:].astype(BF16)
        g2p = jnp.pad(rwkv_g2[i], ((0, G_RANK_PAD - G_RANK), (0, 0))).astype(BF16)

        h = _rmsnorm(xf, norm_mix_g[i], BF16)
        zx = _mm_wres(h, w_in, name="proj_zx", layer=i, n=c_dt, out_dtype=F32, tm=1024, tn=512).reshape(bsz, t, -1)
        rkv = _mm_wres(h, w_in, name="proj_rkv", layer=i, n=3 * d, w_col0=c_r, out_dtype=F32, tm=1024,
                       tn=512).reshape(bsz, t, -1)
        gates = _mm(h, w_gates, name="proj_gates", out_dtype=F32, tm=1024, tn=512)
        small = _proj_small(h, w_in, layer=i, dt_col0=c_dt, lo_col0=c_wlo).reshape(bsz, t, -1)

        u_ssd = _ssd(zx, small, ssd_conv_w[i], ssd_conv_b[i], ssd_dt_bias[i], ssd_a_log[i], ssd_d[i],
                     ssd_norm_g[i])
        u_rwkv = _wkv(rkv, small, rwkv_mu[i], rwkv_w2[i].astype(BF16), rwkv_a2[i].astype(BF16), g2p,
                      rwkv_w0[i], rwkv_a0[i], rwkv_k_k[i], rwkv_k_a[i], rwkv_r_k[i], rwkv_ln_g[i], rwkv_ln_b[i])

        part = _mm(u_ssd.reshape(m, SSD_D_INNER), w_branch_ssd[i].astype(BF16), name="branch_ssd", out_dtype=F32,
                   tm=1024, tn=1024, tk=2048, mul=gates, mul_act="sigmoid", mul_col0=0)
        merged = _mm_wres(u_rwkv.reshape(m, d), w_branch_rwkv, name="branch_rwkv", layer=i, n=d, out_dtype=BF16,
                          tm=1024, tn=512, mul=gates, mul_act="sigmoid", mul_col0=d, res=part)
        xf = _mm_wres(merged, w_out, name="out_proj", layer=i, n=d, out_dtype=F32, tm=1024, tn=512, res=xf)

        h = _rmsnorm(xf, norm_ffn_g[i], BF16)
        ff = _mm_wres(h, w_ff1, name="ffn_up", layer=i, n=w_ff1.shape[2], out_dtype=BF16, tm=1024, tn=512,
                      act="relu2")
        xf = _mm(ff, w_ff2[i].astype(BF16), name="ffn_down", out_dtype=F32, tm=1024, tn=1024, tk=2048, res=xf)

        h = _rmsnorm(xf, norm_ple_g[i], BF16)
        e = _ple_embed(p[i].reshape(m, PLE_DIM).astype(BF16), w_ple_proj[i].astype(BF16), ple_post_g[i])
        xf = _mm_wres(h, w_ple_gate, name="ple_gate", layer=i, n=d, out_dtype=F32, tm=1024, tn=512, act="sigmoid",
                      mul=e, res=xf)
    return _rmsnorm(xf, final_norm_g, F32).reshape(bsz, t, d)
```

```python
import functools
import math

import jax
import jax.numpy as jnp
from jax import lax
from jax.experimental import pallas as pl
from jax.experimental.pallas import tpu as pltpu

F32 = jnp.float32
BF16 = jnp.bfloat16

CHUNK = 64
SSD_D_INNER = 8192
SSD_HEAD_DIM = 64
SSD_HEADS = 128
SSD_GROUPS = 8
SSD_HPG = 16
SSD_STATE = 128
SSD_CONV = 4
SSD_GROUP_COLS = SSD_D_INNER // SSD_GROUPS
SSD_CONV_DIM = SSD_D_INNER + 2 * SSD_GROUPS * SSD_STATE
SSD_NORM_EPS = 1e-5
RWKV_HEAD_DIM = 64
W_RANK = 128
A_RANK = 128
G_RANK = 480
G_RANK_PAD = 512
DECAY_SCALE = math.exp(-0.5)
RWKV_GN_EPS = RWKV_HEAD_DIM * 1e-5
L2_EPS = 1e-12
NORM_EPS = 1e-6
PLE_DIM = 256

SMALL_COLS = G_RANK_PAD + SSD_HEADS + W_RANK + A_RANK
SMALL_LO_COLS = W_RANK + A_RANK + G_RANK_PAD
SMALL_DT_BLK = G_RANK_PAD // 128
SMALL_WLO_BLK = SMALL_DT_BLK + 1
SMALL_ALO_BLK = SMALL_DT_BLK + 2

LANES = 128
SUBLANES = 8
BF16_ROWS = 16
VMEM_LIMIT_BYTES = 56 * 1024 * 1024


def _params(*sem):
    return pltpu.CompilerParams(dimension_semantics=sem, vmem_limit_bytes=VMEM_LIMIT_BYTES)


def _split3(x):
    hi = x.astype(BF16)
    r1 = x - hi.astype(F32)
    mid = r1.astype(BF16)
    lo = (r1 - mid.astype(F32)).astype(BF16)
    return hi, mid, lo


def _dot(a, b, dims=(((1,), (0,)), ((), ()))):
    return lax.dot_general(a.astype(BF16), b.astype(BF16), dims, preferred_element_type=F32)


def _dot_nt(a, b):
    return _dot(a, b, (((1,), (1,)), ((), ())))


def _dot_tn(a, b):
    return _dot(a, b, (((0,), (0,)), ((), ())))


def _cumsum_rows(tri_bf16, x):
    x0, x1, x2 = _split3(x)
    d = lambda q: jnp.dot(tri_bf16, q, preferred_element_type=F32)
    return d(x0) + d(x1) + d(x2)


def _sigmoid(x):
    return 0.5 * jnp.tanh(0.5 * x) + 0.5


def _silu(x):
    h = 0.5 * x
    return h + h * jnp.tanh(h)


def _softplus(x):
    return jnp.maximum(x, 0.0) + jnp.log1p(jnp.exp(-jnp.abs(x)))


def _stack_heads(x, top):
    return jnp.concatenate([jnp.where(top, x, 0.0), jnp.where(top, 0.0, x)], axis=0)


def _head_sum(x, top):
    s0 = jnp.sum(jnp.where(top, x, 0.0), axis=-1, keepdims=True)
    s1 = jnp.sum(jnp.where(top, 0.0, x), axis=-1, keepdims=True)
    return jnp.where(top, s0, s1)


def _side_specs(side, step_of, n_steps):
    src, row0, nrows, rs = side
    cols = src.shape[1]
    nslabs = nrows // rs
    assert src.ndim == 2 and nrows % rs == 0 and nslabs <= n_steps and rs % BF16_ROWS == 0 and row0 % SUBLANES == 0
    slab = lambda *g: jnp.minimum(step_of(*g), nslabs - 1)
    in_spec = pl.BlockSpec((pl.Element(rs), pl.Element(cols)),
                           lambda *g: (pl.multiple_of(row0 + slab(*g) * rs, SUBLANES), 0))
    out_spec = pl.BlockSpec((rs, cols), lambda *g: (slab(*g), 0))
    return in_spec, out_spec, jax.ShapeDtypeStruct((nrows, cols), BF16)


def _rmsnorm_kernel(x_ref, g_ref, o_ref, *, eps):
    x = x_ref[...]
    y = x * lax.rsqrt(jnp.mean(x * x, axis=-1, keepdims=True) + eps)
    o_ref[...] = (y * g_ref[...]).astype(o_ref.dtype)


def _rmsnorm(x, g, out_dtype, tm=256):
    m, d = x.shape
    return pl.pallas_call(
        functools.partial(_rmsnorm_kernel, eps=NORM_EPS),
        grid=(m // tm,),
        in_specs=[pl.BlockSpec((tm, d), lambda i: (i, 0)), pl.BlockSpec((1, d), lambda i: (0, 0))],
        out_specs=pl.BlockSpec((tm, d), lambda i: (i, 0)),
        out_shape=jax.ShapeDtypeStruct((m, d), out_dtype),
        name="rmsnorm",
        compiler_params=_params("parallel"),
    )(x, g.reshape(1, d))


def _act(x, kind):
    if kind is None:
        return x
    if kind == "relu2":
        r = jnp.maximum(x, 0.0)
        return r * r
    if kind == "sigmoid":
        return _sigmoid(x)
    raise ValueError(kind)


def _mm_kernel(*refs, nk, w_nt, act, has_mul, mul_act, has_res, has_side):
    x_ref, w_ref = refs[0], refs[1]
    pos = 2
    mul_ref = res_ref = None
    if has_mul:
        mul_ref = refs[pos]
        pos += 1
    if has_res:
        res_ref = refs[pos]
        pos += 1
    if has_side:
        side_in_ref = refs[pos]
        pos += 1
    o_ref = refs[pos]
    pos += 1
    if has_side:
        refs[pos][...] = side_in_ref[...].astype(BF16)
        pos += 1
    acc_ref = None if nk == 1 else (o_ref if o_ref.dtype == F32 else refs[pos])

    if w_nt:
        part = lax.dot_general(x_ref[...], w_ref[...], (((1,), (1,)), ((), ())), preferred_element_type=F32)
    else:
        part = jnp.dot(x_ref[...], w_ref[...], preferred_element_type=F32)

    def finish(acc):
        out = _act(acc, act)
        if has_mul:
            out = out * _act(mul_ref[...].astype(F32), mul_act)
        if has_res:
            out = res_ref[...].astype(F32) + out
        o_ref[...] = out.astype(o_ref.dtype)

    if nk == 1:
        finish(part)
    else:
        k = pl.program_id(2)

        @pl.when(k == 0)
        def _():
            acc_ref[...] = part

        @pl.when((k > 0) & (k < nk - 1))
        def _():
            acc_ref[...] += part

        @pl.when(k == nk - 1)
        def _():
            finish(acc_ref[...] + part)


def _mm(x, w, *, name, out_dtype, tm, tn, tk=None, w_nt=False, w_row0=0, n=None, act=None, mul=None, mul_act=None,
        mul_col0=0, res=None, res_col0=0, side=None):
    m, kdim = x.shape
    n = w.shape[1] if n is None else n
    tk = kdim if tk is None else tk
    tm, tn = min(tm, m), min(tn, n)
    nk = kdim // tk
    assert m % tm == 0 and n % tn == 0 and kdim % tk == 0 and mul_col0 % tn == 0 and res_col0 % tn == 0
    if w_nt:
        assert w.shape[1] == kdim and w_row0 % BF16_ROWS == 0 and tn % BF16_ROWS == 0 and w.dtype == BF16
        w_spec = pl.BlockSpec((pl.Element(tn), pl.Element(tk)),
                              lambda i, j, k: (pl.multiple_of(w_row0 + j * tn, BF16_ROWS), k * tk))
    else:
        assert w.shape[0] == kdim and w_row0 == 0
        w_spec = pl.BlockSpec((tk, tn), lambda i, j, k: (k, j))
    in_specs = [pl.BlockSpec((tm, tk), lambda i, j, k: (i, k)), w_spec]
    args = [x, w]
    if mul is not None:
        off = mul_col0 // tn
        in_specs.append(pl.BlockSpec((tm, tn), lambda i, j, k, off=off: (i, j + off)))
        args.append(mul)
    if res is not None:
        off = res_col0 // tn
        in_specs.append(pl.BlockSpec((tm, tn), lambda i, j, k, off=off: (i, j + off)))
        args.append(res)
    grid = (m // tm, n // tn, nk)
    out_specs = [pl.BlockSpec((tm, tn), lambda i, j, k: (i, j))]
    out_shape = [jax.ShapeDtypeStruct((m, n), out_dtype)]
    if side is not None:
        s_in, s_out, s_shape = _side_specs(side, lambda i, j, k: (i * grid[1] + j) * nk + k, grid[0] * grid[1] * nk)
        in_specs.append(s_in)
        args.append(side[0])
        out_specs.append(s_out)
        out_shape.append(s_shape)
    outs = pl.pallas_call(
        functools.partial(_mm_kernel, nk=nk, w_nt=w_nt, act=act, has_mul=mul is not None, mul_act=mul_act,
                          has_res=res is not None, has_side=side is not None),
        grid=grid,
        in_specs=in_specs,
        out_specs=out_specs,
        out_shape=out_shape,
        scratch_shapes=[pltpu.VMEM((tm, tn), F32)] if nk > 1 and out_dtype != F32 else [],
        name=name,
        compiler_params=_params(*(("parallel", "parallel") if side is None else ("arbitrary", "arbitrary")), "arbitrary"),
    )(*args)
    return outs[0] if side is None else tuple(outs)


def _proj_small_kernel(x_ref, wdt_ref, wlo_ref, o_ref):
    x = x_ref[...]
    nt = (((1,), (1,)), ((), ()))
    dt = lax.dot_general(x, wdt_ref[...], nt, preferred_element_type=F32)
    lo = lax.dot_general(x, wlo_ref[...], nt, preferred_element_type=F32)
    o_ref[:, 0:G_RANK_PAD] = lo[:, W_RANK + A_RANK:]
    o_ref[:, G_RANK_PAD:G_RANK_PAD + SSD_HEADS] = dt
    o_ref[:, G_RANK_PAD + SSD_HEADS:] = lo[:, :W_RANK + A_RANK]


def _proj_small(x, wt_dt, wt_lo, *, dt_row0, lo_row0, tm=512):
    m, kdim = x.shape
    win = lambda rows, row0: pl.BlockSpec((pl.Element(rows), pl.Element(kdim)), lambda i: (row0, 0))
    return pl.pallas_call(
        _proj_small_kernel,
        grid=(m // tm,),
        in_specs=[pl.BlockSpec((tm, kdim), lambda i: (i, 0)), win(SSD_HEADS, dt_row0), win(SMALL_LO_COLS, lo_row0)],
        out_specs=pl.BlockSpec((tm, SMALL_COLS), lambda i: (i, 0)),
        out_shape=jax.ShapeDtypeStruct((m, SMALL_COLS), F32),
        name="proj_small",
        compiler_params=_params("parallel"),
    )(x, wt_dt, wt_lo)


def _ple_embed_kernel(p_ref, w_ref, g_ref, o_ref):
    e = jnp.dot(p_ref[...], w_ref[...], preferred_element_type=F32)
    y = e * lax.rsqrt(jnp.mean(e * e, axis=-1, keepdims=True) + NORM_EPS)
    o_ref[...] = (y * g_ref[...]).astype(o_ref.dtype)


def _ple_embed(p, w, g, tm=256):
    m, kdim = p.shape
    n = w.shape[1]
    return pl.pallas_call(
        _ple_embed_kernel,
        grid=(m // tm,),
        in_specs=[pl.BlockSpec((tm, kdim), lambda i: (i, 0)), pl.BlockSpec((kdim, n), lambda i: (0, 0)),
                  pl.BlockSpec((1, n), lambda i: (0, 0))],
        out_specs=pl.BlockSpec((tm, n), lambda i: (i, 0)),
        out_shape=jax.ShapeDtypeStruct((m, n), F32),
        name="ple_embed",
        compiler_params=_params("parallel"),
    )(p, w, g.reshape(1, n))


def _ssd_kernel(xs_ref, bm_ref, cm_ref, dt_ref, z_ref, wxs_ref, wbm_ref, wcm_ref, bxs_ref, bbm_ref, bcm_ref,
                dtb_ref, alog_ref, d_ref, ng_ref, side_in_ref, o_ref, side_out_ref,
                st_ref, hxs_ref, hbm_ref, hcm_ref, axs_ref, abm_ref, acm_ref, *, n_chunks):
    tb = n_chunks * CHUNK
    hist = SUBLANES
    side_out_ref[...] = side_in_ref[...].astype(BF16)

    @pl.when(pl.program_id(2) == 0)
    def _():
        st_ref[...] = jnp.zeros_like(st_ref)
        for h_ref in (hxs_ref, hbm_ref, hcm_ref):
            h_ref[0:hist, :] = jnp.zeros((hist, h_ref.shape[1]), F32)

    for raw_ref, h_ref, w_ref, b_ref, act_ref in ((xs_ref, hxs_ref, wxs_ref, bxs_ref, axs_ref),
                                                  (bm_ref, hbm_ref, wbm_ref, bbm_ref, abm_ref),
                                                  (cm_ref, hcm_ref, wcm_ref, bcm_ref, acm_ref)):
        h_ref[hist:hist + tb, :] = raw_ref[0]
        hv = h_ref[...]
        acc = b_ref[...] + w_ref[SSD_CONV - 1:SSD_CONV, :] * hv[hist:]
        for j in range(SSD_CONV - 1):
            acc = acc + w_ref[j:j + 1, :] * pltpu.roll(hv, SSD_CONV - 1 - j, 0)[hist:]
        act_ref[...] = _silu(acc)
        h_ref[0:hist, :] = raw_ref[0, tb - hist:tb, :]

    gc = SSD_GROUP_COLS
    g = pl.program_id(1)
    expand = (lax.broadcasted_iota(jnp.int32, (SSD_HEADS, gc), 0)
              == (lax.broadcasted_iota(jnp.int32, (SSD_HEADS, gc), 1) >> 6) + g * SSD_HPG).astype(BF16)
    li = lax.broadcasted_iota(jnp.int32, (CHUNK, gc), 0)
    si = lax.broadcasted_iota(jnp.int32, (CHUNK, gc), 1) & (CHUNK - 1)
    eye_t = li == si
    causal_t = li >= si
    top = lax.broadcasted_iota(jnp.int32, (CHUNK, LANES), 1) < SSD_HEAD_DIM
    d_row = d_ref[0]
    ng_row = ng_ref[...]
    trow = lax.broadcasted_iota(jnp.int32, (tb, tb), 0)
    tcol = lax.broadcasted_iota(jnp.int32, (tb, tb), 1)
    tri_blk = (((trow >> 6) == (tcol >> 6)) & (trow >= tcol)).astype(BF16)
    dt_all = _softplus(dt_ref[0] + dtb_ref[...])
    acs_all = _cumsum_rows(tri_blk, dt_all * -jnp.exp(alog_ref[...]))

    def expand_heads(q):
        q0, q1, q2 = _split3(q)
        de = lambda r: jnp.dot(r, expand, preferred_element_type=F32)
        return de(q0) + de(q1) + de(q2)

    st = st_ref[...]
    for c in range(n_chunks):
        sl = slice(c * CHUNK, (c + 1) * CHUNK)
        xs = axs_ref[sl, :]
        bm = abm_ref[sl, :]
        cm = acm_ref[sl, :]
        dt_exp = expand_heads(dt_all[sl, :])
        acs = expand_heads(acs_all[sl, :])
        rowpart = jnp.sum(jnp.where(eye_t, acs, 0.0), axis=0, keepdims=True)
        decay = jnp.exp(jnp.where(causal_t, acs - rowpart, -jnp.inf))
        xdt = xs * dt_exp
        cb2 = _dot_nt(cm, jnp.concatenate([bm, bm], axis=0))
        last = acs[CHUNK - 1:CHUNK, :]
        y_off = _dot(cm, st) * jnp.exp(acs)
        parts = []
        for j in range(gc // LANES):
            lanes = slice(j * LANES, (j + 1) * LANES)
            parts.append(_dot(cb2 * decay[:, lanes], _stack_heads(xdt[:, lanes], top)))
        y = jnp.concatenate(parts, axis=1) + y_off + d_row * xs
        st = st * jnp.exp(last) + _dot_tn(bm, xdt * jnp.exp(last - acs))
        y = y * _silu(z_ref[0, sl, :])
        y = y * lax.rsqrt(jnp.mean(y * y, axis=-1, keepdims=True) + SSD_NORM_EPS)
        o_ref[0, sl, :] = (y * ng_row).astype(o_ref.dtype)
    st_ref[...] = st


def _ssd(zx, small, conv_w, conv_b, dt_bias, a_log, d_skip, norm_g, side, *, tb=256):
    bsz, t, _ = zx.shape
    gc = SSD_GROUP_COLS
    xs_blk0 = SSD_D_INNER // gc
    bm_blk0 = 2 * SSD_D_INNER // SSD_STATE
    cm_blk0 = bm_blk0 + SSD_GROUPS
    wbm_blk0 = SSD_D_INNER // SSD_STATE
    wcm_blk0 = wbm_blk0 + SSD_GROUPS
    conv_b = conv_b.reshape(1, SSD_CONV_DIM)
    per_group = lambda v: jnp.repeat(v, SSD_HEAD_DIM).reshape(SSD_GROUPS, 1, gc)
    seq = lambda w, f: pl.BlockSpec((1, tb, w), lambda b, g, ti: (b, ti, f(g)))
    chan = lambda r, w, f: pl.BlockSpec((r, w), lambda b, g, ti: (0, f(g)))
    grp = pl.BlockSpec((1, 1, gc), lambda b, g, ti: (g, 0, 0))
    nt_steps = t // tb
    s_in, s_out, s_shape = _side_specs(side, lambda b, g, ti: (b * SSD_GROUPS + g) * nt_steps + ti,
                                       bsz * SSD_GROUPS * nt_steps)
    return pl.pallas_call(
        functools.partial(_ssd_kernel, n_chunks=tb // CHUNK),
        grid=(bsz, SSD_GROUPS, t // tb),
        in_specs=[seq(gc, lambda g: xs_blk0 + g), seq(SSD_STATE, lambda g: bm_blk0 + g),
                  seq(SSD_STATE, lambda g: cm_blk0 + g), seq(SSD_HEADS, lambda g: SMALL_DT_BLK),
                  seq(gc, lambda g: g),
                  chan(SSD_CONV, gc, lambda g: g), chan(SSD_CONV, SSD_STATE, lambda g: wbm_blk0 + g),
                  chan(SSD_CONV, SSD_STATE, lambda g: wcm_blk0 + g),
                  chan(1, gc, lambda g: g), chan(1, SSD_STATE, lambda g: wbm_blk0 + g),
                  chan(1, SSD_STATE, lambda g: wcm_blk0 + g),
                  chan(1, SSD_HEADS, lambda g: 0), chan(1, SSD_HEADS, lambda g: 0), grp,
                  chan(1, gc, lambda g: g), s_in],
        out_specs=[seq(gc, lambda g: g), s_out],
        out_shape=[jax.ShapeDtypeStruct((bsz, t, SSD_D_INNER), BF16), s_shape],
        scratch_shapes=[pltpu.VMEM((SSD_STATE, gc), F32),
                        pltpu.VMEM((SUBLANES + tb, gc), F32), pltpu.VMEM((SUBLANES + tb, SSD_STATE), F32),
                        pltpu.VMEM((SUBLANES + tb, SSD_STATE), F32),
                        pltpu.VMEM((tb, gc), F32), pltpu.VMEM((tb, SSD_STATE), F32), pltpu.VMEM((tb, SSD_STATE), F32)],
        name="ssd_scan",
        compiler_params=_params("arbitrary", "arbitrary", "arbitrary"),
    )(zx, zx, zx, small, zx, conv_w, conv_w, conv_w, conv_b, conv_b, conv_b, dt_bias.reshape(1, SSD_HEADS),
      a_log.reshape(1, SSD_HEADS), per_group(d_skip), norm_g.reshape(1, SSD_D_INNER), side[0])


def _wkv_kernel(r_ref, k_ref, v_ref, glo_ref, wlo_ref, alo_ref,
                mur_ref, muk_ref, muv_ref, mug_ref, muw_ref, mua_ref,
                w2_ref, a2_ref, g2_ref, w0_ref, a0_ref, kk_ref, ka_ref, rk_ref, lng_ref, lnb_ref, side_in_ref,
                o_ref, side_out_ref,
                s_ref, prev_ref, plo_ref, rr_ref, kr_ref, vv_ref, lw_ref, kn_ref, ag_ref, gg_ref,
                *, n_chunks, n_pairs):
    tb = n_chunks * CHUNK
    glo_w = glo_ref.shape[2]
    side_out_ref[...] = side_in_ref[...].astype(BF16)

    @pl.when(pl.program_id(2) == 0)
    def _():
        s_ref[...] = jnp.zeros_like(s_ref)
        prev_ref[...] = jnp.zeros_like(prev_ref)
        plo_ref[...] = jnp.zeros_like(plo_ref)

    first_row = lax.broadcasted_iota(jnp.int32, (tb, 1), 0) == 0

    def lerp(x, prev_row, mu):
        shifted = jnp.where(first_row, prev_row, pltpu.roll(x, 1, 0))
        return x + (shifted - x) * mu

    r_raw, k_raw, v_raw = r_ref[0], k_ref[0], v_ref[0]
    glo_raw, wlo_raw, alo_raw = glo_ref[0], wlo_ref[0], alo_ref[0]
    r = lerp(r_raw, prev_ref[0:1, :], mur_ref[...])
    k = lerp(k_raw, prev_ref[1:2, :], muk_ref[...])
    v = lerp(v_raw, prev_ref[2:3, :], muv_ref[...])
    g_lo = lerp(glo_raw, plo_ref[0:1, 0:glo_w], mug_ref[...])
    w_lo = lerp(wlo_raw, plo_ref[0:1, glo_w:glo_w + W_RANK], muw_ref[...])
    a_lo = lerp(alo_raw, plo_ref[0:1, glo_w + W_RANK:], mua_ref[...])
    prev_ref[0:1, :] = r_raw[tb - 1:tb, :]
    prev_ref[1:2, :] = k_raw[tb - 1:tb, :]
    prev_ref[2:3, :] = v_raw[tb - 1:tb, :]
    plo_ref[0:1, 0:glo_w] = glo_raw[tb - 1:tb, :]
    plo_ref[0:1, glo_w:glo_w + W_RANK] = wlo_raw[tb - 1:tb, :]
    plo_ref[0:1, glo_w + W_RANK:] = alo_raw[tb - 1:tb, :]

    lw = -DECAY_SCALE * _sigmoid(w0_ref[...] + jnp.dot(jnp.tanh(w_lo).astype(BF16), w2_ref[...],
                                                       preferred_element_type=F32))
    a_gate = _sigmoid(a0_ref[...] + jnp.dot(a_lo.astype(BF16), a2_ref[...], preferred_element_type=F32))
    gg_ref[...] = jnp.dot(_sigmoid(g_lo).astype(BF16), g2_ref[...], preferred_element_type=F32)
    top_blk = lax.broadcasted_iota(jnp.int32, (tb, LANES), 1) < RWKV_HEAD_DIM
    kk = k * kk_ref[...]
    for j in range(n_pairs):
        lanes = slice(j * LANES, (j + 1) * LANES)
        kj = kk[:, lanes]
        kn_ref[:, lanes] = kj / jnp.maximum(jnp.sqrt(_head_sum(kj * kj, top_blk)), L2_EPS)
    rr_ref[...] = r
    kr_ref[...] = k * (1.0 + (a_gate - 1.0) * ka_ref[...])
    vv_ref[...] = v
    lw_ref[...] = lw
    ag_ref[...] = a_gate

    row = lax.broadcasted_iota(jnp.int32, (LANES, LANES), 0)
    col = lax.broadcasted_iota(jnp.int32, (LANES, LANES), 1)
    same = (row >> 6) == (col >> 6)
    trow = lax.broadcasted_iota(jnp.int32, (tb, tb), 0)
    tcol = lax.broadcasted_iota(jnp.int32, (tb, tb), 1)
    tri_blk = ((trow >> 6) == (tcol >> 6)) & (trow >= tcol)
    top = lax.broadcasted_iota(jnp.int32, (CHUNK, LANES), 1) < RWKV_HEAD_DIM
    inv_n = 1.0 / RWKV_HEAD_DIM
    chains = [(j, c) for j in range(n_pairs) for c in range(n_chunks)]

    def piece(ref, j, c):
        return ref[c * CHUNK:(c + 1) * CHUNK, j * LANES:(j + 1) * LANES]

    cum_all = _cumsum_rows(tri_blk.astype(BF16), lw)

    lane_t = lax.broadcasted_iota(jnp.int32, (CHUNK, LANES), 1) & (CHUNK - 1)
    row_t = lax.broadcasted_iota(jnp.int32, (CHUNK, LANES), 0)
    strict_c = row_t > lane_t
    incl_c = row_t >= lane_t
    eye_c = (row_t == lane_t).astype(F32)
    strict2 = jnp.concatenate([strict_c, incl_c], axis=0)

    def bd(x):
        return _stack_heads(x.astype(BF16), top)

    nb = range(len(chains))
    a_c, r_c, b_bd, k_bd, v_c, bh_c, kh_c, wtot = [], [], [], [], [], [], [], []
    for j, c in chains:
        cum = cum_all[c * CHUNK:(c + 1) * CHUNK, j * LANES:(j + 1) * LANES]
        lwc = piece(lw_ref, j, c)
        kc = piece(kr_ref, j, c)
        knc = piece(kn_ref, j, c)
        b = knc * piece(ag_ref, j, c)
        tot = cum[CHUNK - 1:CHUNK, :]
        e_neg = jnp.exp(-cum)
        e_end = jnp.exp(tot - cum)
        a_c.append((-knc * jnp.exp(cum - lwc)).astype(BF16))
        r_c.append(piece(rr_ref, j, c) * jnp.exp(cum))
        b_bd.append(bd(b * e_neg))
        k_bd.append(bd(kc * e_neg))
        v_c.append(piece(vv_ref, j, c).astype(BF16))
        bh_c.append((b * e_end).astype(BF16))
        kh_c.append((kc * e_end).astype(BF16))
        wtot.append(jnp.exp(tot))
    ar = [jnp.concatenate([a_c[i], r_c[i].astype(BF16)], axis=0) for i in nb]
    sb = [jnp.where(strict2, _dot_nt(ar[i], b_bd[i]), 0.0) for i in nb]
    sk = [jnp.where(strict2, _dot_nt(ar[i], k_bd[i]), 0.0) for i in nb]
    n = [x[:CHUNK] for x in sb]
    rb = [x[CHUNK:].astype(BF16) for x in sb]
    v_bd = [bd(x) for x in v_c]
    tinv = [eye_c + x for x in n]
    p_bd = [bd(x) for x in n]
    npow = [_dot(n[i], p_bd[i]) for i in nb]
    for it in range(1, 6):
        p_bd = [bd(x) for x in npow]
        if it < 5:
            both = [_dot(jnp.concatenate([npow[i], tinv[i]], axis=0), p_bd[i]) for i in nb]
            npow = [x[:CHUNK] for x in both]
            tinv = [tinv[i] + both[i][CHUNK:] for i in nb]
        else:
            tinv = [tinv[i] + _dot(tinv[i], p_bd[i]) for i in nb]
    akv = [_dot(sk[i][:CHUNK], v_bd[i]) for i in nb]
    pq = [_dot(tinv[i], jnp.concatenate([bd(a_c[i]), bd(akv[i])], axis=1)) for i in nb]
    pq_bd = [jnp.concatenate([bd(x[:, :LANES]), bd(x[:, LANES:])], axis=1) for x in pq]
    rpq = [_dot(rb[i], pq_bd[i]) for i in nb]
    r_eff = [(r_c[i] + rpq[i][:, :LANES]).astype(BF16) for i in nb]
    y0 = [rpq[i][:, LANES:] + _dot(sk[i][CHUNK:], v_bd[i]) for i in nb]
    g_eff = [jnp.where(same, _dot_tn(pq[i][:, :LANES], bh_c[i]), 0.0).astype(BF16) for i in nb]
    s1 = [jnp.where(same, _dot_tn(jnp.concatenate([pq[i][:, LANES:].astype(BF16), v_c[i]], axis=0),
                                  jnp.concatenate([bh_c[i], kh_c[i]], axis=0)), 0.0) for i in nb]

    for j in range(n_pairs):
        lanes = slice(j * LANES, (j + 1) * LANES)
        rk_row = rk_ref[:, lanes]
        lng_row = lng_ref[:, lanes]
        lnb_row = lnb_ref[:, lanes]
        s = s_ref[j]
        for c in range(n_chunks):
            i = j * n_chunks + c
            y = _dot_nt(r_eff[i], s) + y0[i]
            s = s * wtot[i] + _dot(s, g_eff[i]) + s1[i]
            mean = _head_sum(y, top) * inv_n
            yc = y - mean
            var = _head_sum(yc * yc, top) * inv_n
            yn = yc * lax.rsqrt(var + RWKV_GN_EPS) * lng_row + lnb_row
            bonus = _head_sum(piece(rr_ref, j, c) * piece(kr_ref, j, c) * rk_row, top) * piece(vv_ref, j, c)
            o_ref[0, c * CHUNK:(c + 1) * CHUNK, lanes] = ((yn + bonus) * piece(gg_ref, j, c)).astype(o_ref.dtype)
        s_ref[j] = s


def _wkv(rkv, small, mu, w2, a2, g2p, w0, a0, k_k, k_a, r_k, ln_g, ln_b, side, *, tb=128, pairs=8):
    bsz, t, d3 = rkv.shape
    d = d3 // 3
    wl = pairs * LANES
    nh = d // wl
    mu_rkv = mu[:3 * d].reshape(1, 3 * d)
    mu_w = mu[3 * d:3 * d + W_RANK].reshape(1, W_RANK)
    mu_a = mu[3 * d + W_RANK:3 * d + W_RANK + A_RANK].reshape(1, A_RANK)
    mu_g = jnp.pad(mu[3 * d + W_RANK + A_RANK:], (0, G_RANK_PAD - G_RANK)).reshape(1, G_RANK_PAD)
    seq = lambda w, f: pl.BlockSpec((1, tb, w), lambda b, h, ti: (b, ti, f(h)))
    chan = lambda r, w, f: pl.BlockSpec((r, w), lambda b, h, ti: (0, f(h)))
    vec = lambda q: q.reshape(1, d)
    blk = pltpu.VMEM((tb, wl), F32)
    nt_steps = t // tb
    s_in, s_out, s_shape = _side_specs(side, lambda b, h, ti: (b * nh + h) * nt_steps + ti, bsz * nh * nt_steps)
    return pl.pallas_call(
        functools.partial(_wkv_kernel, n_chunks=tb // CHUNK, n_pairs=pairs),
        grid=(bsz, nh, t // tb),
        in_specs=[seq(wl, lambda h: h), seq(wl, lambda h: nh + h), seq(wl, lambda h: 2 * nh + h),
                  seq(G_RANK_PAD, lambda h: 0), seq(W_RANK, lambda h: SMALL_WLO_BLK),
                  seq(A_RANK, lambda h: SMALL_ALO_BLK),
                  chan(1, wl, lambda h: h), chan(1, wl, lambda h: nh + h), chan(1, wl, lambda h: 2 * nh + h),
                  chan(1, G_RANK_PAD, lambda h: 0), chan(1, W_RANK, lambda h: 0), chan(1, A_RANK, lambda h: 0),
                  chan(W_RANK, wl, lambda h: h), chan(A_RANK, wl, lambda h: h), chan(G_RANK_PAD, wl, lambda h: h)]
                 + [chan(1, wl, lambda h: h)] * 7 + [s_in],
        out_specs=[seq(wl, lambda h: h), s_out],
        out_shape=[jax.ShapeDtypeStruct((bsz, t, d), BF16), s_shape],
        scratch_shapes=[pltpu.VMEM((pairs, LANES, LANES), F32), pltpu.VMEM((SUBLANES, wl), F32),
                        pltpu.VMEM((SUBLANES, G_RANK_PAD + W_RANK + A_RANK), F32)] + [blk] * 7,
        name="wkv7_scan",
        compiler_params=_params("arbitrary", "arbitrary", "arbitrary"),
    )(rkv, rkv, rkv, small, small, small, mu_rkv, mu_rkv, mu_rkv, mu_g, mu_w, mu_a,
      w2, a2, g2p, vec(w0), vec(a0), vec(k_k), vec(k_a), vec(r_k), vec(ln_g), vec(ln_b), side[0])


def kernel(x, p, norm_mix_g, w_in, ssd_conv_w, ssd_conv_b, ssd_dt_bias, ssd_a_log, ssd_d, ssd_norm_g, rwkv_mu, rwkv_w0, rwkv_w2, rwkv_a0, rwkv_a2, rwkv_g2, rwkv_k_k, rwkv_k_a, rwkv_r_k, rwkv_ln_g, rwkv_ln_b, w_branch_ssd, w_branch_rwkv, w_out, norm_ffn_g, w_ff1, w_ff2, norm_ple_g, w_ple_gate, w_ple_proj, ple_post_g, final_norm_g):
    bsz, t, d = x.shape
    m = bsz * t
    depth = w_in.shape[0]
    xf = x.reshape(m, d)
    for i in range(depth):
        c_dt = SSD_D_INNER + SSD_CONV_DIM
        c_r = c_dt + SSD_HEADS
        c_wlo = c_r + 3 * d
        c_glo = c_wlo + W_RANK + A_RANK
        c_gate = c_glo + G_RANK
        w_in_t = jnp.swapaxes(w_in[i], 0, 1)
        n_head, n_rest = c_r, w_in.shape[2] - c_r
        w_head = w_in_t[:n_head].astype(BF16)
        g2p = jnp.pad(rwkv_g2[i], ((0, G_RANK_PAD - G_RANK), (0, 0))).astype(BF16)
        whole = lambda w, slab_rows: (w, 0, w.shape[0], slab_rows)

        h = _rmsnorm(xf, norm_mix_g[i], BF16)
        proj = functools.partial(_mm, h, w_nt=True, out_dtype=F32, tm=1024, tn=512)
        zx, w_rest = proj(w_head, name="proj_zx", n=c_dt, side=(w_in_t, n_head, n_rest, 96))
        rkv, w_ff1_b = proj(w_rest, name="proj_rkv", n=3 * d, side=whole(w_ff1[i], 32))
        gates, w_bssd_b = proj(w_rest, name="proj_gates", w_row0=c_gate - n_head, n=2 * d,
                               side=whole(w_branch_ssd[i], 64))
        small = _proj_small(h, w_head, w_rest, dt_row0=c_dt, lo_row0=c_wlo - n_head).reshape(bsz, t, -1)
        zx = zx.reshape(bsz, t, -1)
        rkv = rkv.reshape(bsz, t, -1)

        u_ssd, w_brwkv_b = _ssd(zx, small, ssd_conv_w[i], ssd_conv_b[i], ssd_dt_bias[i], ssd_a_log[i], ssd_d[i],
                                ssd_norm_g[i], whole(w_branch_rwkv[i], 16))
        u_rwkv, w_ff2_b = _wkv(rkv, small, rwkv_mu[i], rwkv_w2[i].astype(BF16), rwkv_a2[i].astype(BF16), g2p,
                               rwkv_w0[i], rwkv_a0[i], rwkv_k_k[i], rwkv_k_a[i], rwkv_r_k[i], rwkv_ln_g[i],
                               rwkv_ln_b[i], whole(w_ff2[i], 64))

        part, w_out_b = _mm(u_ssd.reshape(m, SSD_D_INNER), w_bssd_b, name="branch_ssd", out_dtype=F32, tm=1024,
                            tn=1024, tk=2048, mul=gates, mul_act="sigmoid", mul_col0=0, side=whole(w_out[i], 32))
        merged = _mm(u_rwkv.reshape(m, d), w_brwkv_b, name="branch_rwkv", out_dtype=BF16, tm=1024, tn=512,
                     mul=gates, mul_act="sigmoid", mul_col0=d, res=part)
        xf = _mm(merged, w_out_b, name="out_proj", out_dtype=F32, tm=1024, tn=512, res=xf)

        h = _rmsnorm(xf, norm_ffn_g[i], BF16)
        ff, w_ple_b = _mm(h, w_ff1_b, name="ffn_up", out_dtype=BF16, tm=1024, tn=512, act="relu2",
                          side=whole(w_ple_gate[i], 16))
        xf = _mm(ff, w_ff2_b, name="ffn_down", out_dtype=F32, tm=1024, tn=1024, tk=2048, res=xf)

        h = _rmsnorm(xf, norm_ple_g[i], BF16)
        e = _ple_embed(p[i].reshape(m, PLE_DIM).astype(BF16), w_ple_proj[i].astype(BF16), ple_post_g[i])
        xf = _mm(h, w_ple_b, name="ple_gate", out_dtype=F32, tm=1024, tn=512, act="sigmoid", mul=e, res=xf)
    return _rmsnorm(xf, final_norm_g, F32).reshape(bsz, t, d)
```

```python
import functools
import math

import jax
import jax.numpy as jnp
from jax import lax
from jax.experimental import pallas as pl
from jax.experimental.pallas import tpu as pltpu

F32 = jnp.float32
BF16 = jnp.bfloat16

CHUNK = 64
SSD_D_INNER = 8192
SSD_HEAD_DIM = 64
SSD_HEADS = 128
SSD_GROUPS = 8
SSD_HPG = 16
SSD_STATE = 128
SSD_CONV = 4
SSD_GROUP_COLS = SSD_D_INNER // SSD_GROUPS
SSD_CONV_DIM = SSD_D_INNER + 2 * SSD_GROUPS * SSD_STATE
SSD_NORM_EPS = 1e-5
RWKV_HEAD_DIM = 64
W_RANK = 128
A_RANK = 128
G_RANK = 480
G_RANK_PAD = 512
DECAY_SCALE = math.exp(-0.5)
RWKV_GN_EPS = RWKV_HEAD_DIM * 1e-5
L2_EPS = 1e-12
NORM_EPS = 1e-6
PLE_DIM = 256

SMALL_COLS = G_RANK_PAD + SSD_HEADS + W_RANK + A_RANK
SMALL_LO_COLS = W_RANK + A_RANK + G_RANK_PAD
SMALL_DT_BLK = G_RANK_PAD // 128
SMALL_WLO_BLK = SMALL_DT_BLK + 1
SMALL_ALO_BLK = SMALL_DT_BLK + 2

LANES = 128
SUBLANES = 8
BF16_ROWS = 16
VMEM_LIMIT_BYTES = 56 * 1024 * 1024


def _params(*sem):
    return pltpu.CompilerParams(dimension_semantics=sem, vmem_limit_bytes=VMEM_LIMIT_BYTES)


def _split3(x):
    hi = x.astype(BF16)
    r1 = x - hi.astype(F32)
    mid = r1.astype(BF16)
    lo = (r1 - mid.astype(F32)).astype(BF16)
    return hi, mid, lo


def _dot(a, b, dims=(((1,), (0,)), ((), ()))):
    return lax.dot_general(a.astype(BF16), b.astype(BF16), dims, preferred_element_type=F32)


def _dot_nt(a, b):
    return _dot(a, b, (((1,), (1,)), ((), ())))


def _dot_tn(a, b):
    return _dot(a, b, (((0,), (0,)), ((), ())))


def _cumsum_rows(tri_bf16, x):
    x0, x1, x2 = _split3(x)
    d = lambda q: jnp.dot(tri_bf16, q, preferred_element_type=F32)
    return d(x0) + d(x1) + d(x2)


def _sigmoid(x):
    return 0.5 * jnp.tanh(0.5 * x) + 0.5


def _silu(x):
    h = 0.5 * x
    return h + h * jnp.tanh(h)


def _softplus(x):
    return jnp.maximum(x, 0.0) + jnp.log1p(jnp.exp(-jnp.abs(x)))


def _stack_heads(x, top):
    return jnp.concatenate([jnp.where(top, x, 0.0), jnp.where(top, 0.0, x)], axis=0)


def _head_sum(x, top):
    s0 = jnp.sum(jnp.where(top, x, 0.0), axis=-1, keepdims=True)
    s1 = jnp.sum(jnp.where(top, 0.0, x), axis=-1, keepdims=True)
    return jnp.where(top, s0, s1)


def _side_specs(side, step_of, n_steps):
    src, row0, nrows, rs = side
    cols = src.shape[1]
    nslabs = nrows // rs
    assert src.ndim == 2 and nrows % rs == 0 and nslabs <= n_steps and rs % BF16_ROWS == 0 and row0 % SUBLANES == 0
    slab = lambda *g: jnp.minimum(step_of(*g), nslabs - 1)
    in_spec = pl.BlockSpec((pl.Element(rs), pl.Element(cols)),
                           lambda *g: (pl.multiple_of(row0 + slab(*g) * rs, SUBLANES), 0))
    out_spec = pl.BlockSpec((rs, cols), lambda *g: (slab(*g), 0))
    return in_spec, out_spec, jax.ShapeDtypeStruct((nrows, cols), BF16)


def _rmsnorm_kernel(x_ref, g_ref, o_ref, *, eps):
    x = x_ref[...]
    y = x * lax.rsqrt(jnp.mean(x * x, axis=-1, keepdims=True) + eps)
    o_ref[...] = (y * g_ref[...]).astype(o_ref.dtype)


def _rmsnorm(x, g, out_dtype, tm=256):
    m, d = x.shape
    return pl.pallas_call(
        functools.partial(_rmsnorm_kernel, eps=NORM_EPS),
        grid=(m // tm,),
        in_specs=[pl.BlockSpec((tm, d), lambda i: (i, 0)), pl.BlockSpec((1, d), lambda i: (0, 0))],
        out_specs=pl.BlockSpec((tm, d), lambda i: (i, 0)),
        out_shape=jax.ShapeDtypeStruct((m, d), out_dtype),
        name="rmsnorm",
        compiler_params=_params("parallel"),
    )(x, g.reshape(1, d))


def _act(x, kind):
    if kind is None:
        return x
    if kind == "relu2":
        r = jnp.maximum(x, 0.0)
        return r * r
    if kind == "sigmoid":
        return _sigmoid(x)
    raise ValueError(kind)


def _mm_kernel(*refs, nk, w_nt, act, has_mul, mul_act, has_res, has_scale, has_norm, has_side, norm_dim):
    refs = list(refs)
    x_ref, w_ref = refs[0], refs[1]
    pos = 2

    def take(flag):
        nonlocal pos
        if not flag:
            return None
        pos += 1
        return refs[pos - 1]

    mul_ref, res_ref, ss_in_ref, g_ref, side_in_ref = (take(f) for f in (has_mul, has_res, has_scale, has_norm, has_side))
    o_ref = take(True)
    xg_ref, ss_out_ref, side_out_ref = take(has_norm), take(has_norm), take(has_side)
    if has_side:
        side_out_ref[...] = side_in_ref[...].astype(BF16)
    acc_ref = None if nk == 1 else (o_ref if o_ref.dtype == F32 else refs[pos])

    if w_nt:
        part = lax.dot_general(x_ref[...], w_ref[...], (((1,), (1,)), ((), ())), preferred_element_type=F32)
    else:
        part = jnp.dot(x_ref[...], w_ref[...], preferred_element_type=F32)

    def finish(acc):
        if has_scale:
            acc = acc * lax.rsqrt(ss_in_ref[:, 0:1] * (1.0 / norm_dim) + NORM_EPS)
        out = _act(acc, act)
        if has_mul:
            out = out * _act(mul_ref[...].astype(F32), mul_act)
        if has_res:
            out = res_ref[...].astype(F32) + out
        o_ref[...] = out.astype(o_ref.dtype)
        if has_norm:
            xg_ref[...] = (out * g_ref[...]).astype(BF16)
            row_ss = jnp.broadcast_to(jnp.sum(out * out, axis=1, keepdims=True), ss_out_ref.shape)
            first = pl.program_id(1) == 0

            @pl.when(first)
            def _():
                ss_out_ref[...] = row_ss

            @pl.when(jnp.logical_not(first))
            def _():
                ss_out_ref[...] += row_ss

    if nk == 1:
        finish(part)
    else:
        k = pl.program_id(2)

        @pl.when(k == 0)
        def _():
            acc_ref[...] = part

        @pl.when((k > 0) & (k < nk - 1))
        def _():
            acc_ref[...] += part

        @pl.when(k == nk - 1)
        def _():
            finish(acc_ref[...] + part)


def _mm(x, w, *, name, out_dtype, tm, tn, tk=None, w_nt=False, w_row0=0, n=None, act=None, mul=None, mul_act=None,
        mul_col0=0, res=None, res_col0=0, side=None, row_ss=None, next_norm_g=None):
    m, kdim = x.shape
    n = w.shape[1] if n is None else n
    tk = kdim if tk is None else tk
    tm, tn = min(tm, m), min(tn, n)
    nk = kdim // tk
    assert m % tm == 0 and n % tn == 0 and kdim % tk == 0 and mul_col0 % tn == 0 and res_col0 % tn == 0
    if w_nt:
        assert w.shape[1] == kdim and w_row0 % BF16_ROWS == 0 and tn % BF16_ROWS == 0 and w.dtype == BF16
        w_spec = pl.BlockSpec((pl.Element(tn), pl.Element(tk)),
                              lambda i, j, k: (pl.multiple_of(w_row0 + j * tn, BF16_ROWS), k * tk))
    else:
        assert w.shape[0] == kdim and w_row0 == 0
        w_spec = pl.BlockSpec((tk, tn), lambda i, j, k: (k, j))
    in_specs = [pl.BlockSpec((tm, tk), lambda i, j, k: (i, k)), w_spec]
    args = [x, w]
    if mul is not None:
        off = mul_col0 // tn
        in_specs.append(pl.BlockSpec((tm, tn), lambda i, j, k, off=off: (i, j + off)))
        args.append(mul)
    if res is not None:
        off = res_col0 // tn
        in_specs.append(pl.BlockSpec((tm, tn), lambda i, j, k, off=off: (i, j + off)))
        args.append(res)
    if row_ss is not None:
        in_specs.append(pl.BlockSpec((tm, LANES), lambda i, j, k: (i, 0)))
        args.append(row_ss)
    grid = (m // tm, n // tn, nk)
    out_specs = [pl.BlockSpec((tm, tn), lambda i, j, k: (i, j))]
    out_shape = [jax.ShapeDtypeStruct((m, n), out_dtype)]
    if next_norm_g is not None:
        assert nk == 1 and out_dtype == F32
        in_specs.append(pl.BlockSpec((1, tn), lambda i, j, k: (0, j)))
        args.append(next_norm_g.reshape(1, n))
        out_specs += [pl.BlockSpec((tm, tn), lambda i, j, k: (i, j)), pl.BlockSpec((tm, LANES), lambda i, j, k: (i, 0))]
        out_shape += [jax.ShapeDtypeStruct((m, n), BF16), jax.ShapeDtypeStruct((m, LANES), F32)]
    if side is not None:
        s_in, s_out, s_shape = _side_specs(side, lambda i, j, k: (i * grid[1] + j) * nk + k, grid[0] * grid[1] * nk)
        in_specs.append(s_in)
        args.append(side[0])
        out_specs.append(s_out)
        out_shape.append(s_shape)
    outs = pl.pallas_call(
        functools.partial(_mm_kernel, nk=nk, w_nt=w_nt, act=act, has_mul=mul is not None, mul_act=mul_act,
                          has_res=res is not None, has_scale=row_ss is not None, has_norm=next_norm_g is not None,
                          has_side=side is not None, norm_dim=kdim),
        grid=grid,
        in_specs=in_specs,
        out_specs=out_specs,
        out_shape=out_shape,
        scratch_shapes=[pltpu.VMEM((tm, tn), F32)] if nk > 1 and out_dtype != F32 else [],
        name=name,
        compiler_params=_params(*(("parallel", "parallel") if len(out_shape) == 1 else ("arbitrary", "arbitrary")),
                                "arbitrary"),
    )(*args)
    return outs[0] if len(outs) == 1 else tuple(outs)


def _proj_small_kernel(x_ref, wdt_ref, wlo_ref, o_ref):
    x = x_ref[...]
    nt = (((1,), (1,)), ((), ()))
    dt = lax.dot_general(x, wdt_ref[...], nt, preferred_element_type=F32)
    lo = lax.dot_general(x, wlo_ref[...], nt, preferred_element_type=F32)
    o_ref[:, 0:G_RANK_PAD] = lo[:, W_RANK + A_RANK:]
    o_ref[:, G_RANK_PAD:G_RANK_PAD + SSD_HEADS] = dt
    o_ref[:, G_RANK_PAD + SSD_HEADS:] = lo[:, :W_RANK + A_RANK]


def _proj_small(x, wt_dt, wt_lo, *, dt_row0, lo_row0, tm=512):
    m, kdim = x.shape
    win = lambda rows, row0: pl.BlockSpec((pl.Element(rows), pl.Element(kdim)), lambda i: (row0, 0))
    return pl.pallas_call(
        _proj_small_kernel,
        grid=(m // tm,),
        in_specs=[pl.BlockSpec((tm, kdim), lambda i: (i, 0)), win(SSD_HEADS, dt_row0), win(SMALL_LO_COLS, lo_row0)],
        out_specs=pl.BlockSpec((tm, SMALL_COLS), lambda i: (i, 0)),
        out_shape=jax.ShapeDtypeStruct((m, SMALL_COLS), F32),
        name="proj_small",
        compiler_params=_params("parallel"),
    )(x, wt_dt, wt_lo)


def _ple_embed_kernel(p_ref, w_ref, g_ref, o_ref):
    e = jnp.dot(p_ref[...], w_ref[...], preferred_element_type=F32)
    y = e * lax.rsqrt(jnp.mean(e * e, axis=-1, keepdims=True) + NORM_EPS)
    o_ref[...] = (y * g_ref[...]).astype(o_ref.dtype)


def _ple_embed(p, w, g, tm=256):
    m, kdim = p.shape
    n = w.shape[1]
    return pl.pallas_call(
        _ple_embed_kernel,
        grid=(m // tm,),
        in_specs=[pl.BlockSpec((tm, kdim), lambda i: (i, 0)), pl.BlockSpec((kdim, n), lambda i: (0, 0)),
                  pl.BlockSpec((1, n), lambda i: (0, 0))],
        out_specs=pl.BlockSpec((tm, n), lambda i: (i, 0)),
        out_shape=jax.ShapeDtypeStruct((m, n), F32),
        name="ple_embed",
        compiler_params=_params("parallel"),
    )(p, w, g.reshape(1, n))


def _ssd_kernel(xs_ref, bm_ref, cm_ref, dt_ref, z_ref, wxs_ref, wbm_ref, wcm_ref, bxs_ref, bbm_ref, bcm_ref,
                dtb_ref, alog_ref, d_ref, ng_ref, side_in_ref, o_ref, side_out_ref,
                st_ref, hxs_ref, hbm_ref, hcm_ref, axs_ref, abm_ref, acm_ref, *, n_chunks):
    tb = n_chunks * CHUNK
    hist = SUBLANES
    side_out_ref[...] = side_in_ref[...].astype(BF16)

    @pl.when(pl.program_id(2) == 0)
    def _():
        st_ref[...] = jnp.zeros_like(st_ref)
        for h_ref in (hxs_ref, hbm_ref, hcm_ref):
            h_ref[0:hist, :] = jnp.zeros((hist, h_ref.shape[1]), F32)

    for raw_ref, h_ref, w_ref, b_ref, act_ref in ((xs_ref, hxs_ref, wxs_ref, bxs_ref, axs_ref),
                                                  (bm_ref, hbm_ref, wbm_ref, bbm_ref, abm_ref),
                                                  (cm_ref, hcm_ref, wcm_ref, bcm_ref, acm_ref)):
        h_ref[hist:hist + tb, :] = raw_ref[0]
        hv = h_ref[...]
        acc = b_ref[...] + w_ref[SSD_CONV - 1:SSD_CONV, :] * hv[hist:]
        for j in range(SSD_CONV - 1):
            acc = acc + w_ref[j:j + 1, :] * pltpu.roll(hv, SSD_CONV - 1 - j, 0)[hist:]
        act_ref[...] = _silu(acc)
        h_ref[0:hist, :] = raw_ref[0, tb - hist:tb, :]

    gc = SSD_GROUP_COLS
    g = pl.program_id(1)
    expand = (lax.broadcasted_iota(jnp.int32, (SSD_HEADS, gc), 0)
              == (lax.broadcasted_iota(jnp.int32, (SSD_HEADS, gc), 1) >> 6) + g * SSD_HPG).astype(BF16)
    li = lax.broadcasted_iota(jnp.int32, (CHUNK, gc), 0)
    si = lax.broadcasted_iota(jnp.int32, (CHUNK, gc), 1) & (CHUNK - 1)
    eye_t = li == si
    causal_t = li >= si
    top = lax.broadcasted_iota(jnp.int32, (CHUNK, LANES), 1) < SSD_HEAD_DIM
    d_row = d_ref[0]
    ng_row = ng_ref[...]
    trow = lax.broadcasted_iota(jnp.int32, (tb, tb), 0)
    tcol = lax.broadcasted_iota(jnp.int32, (tb, tb), 1)
    tri_blk = (((trow >> 6) == (tcol >> 6)) & (trow >= tcol)).astype(BF16)
    dt_all = _softplus(dt_ref[0] + dtb_ref[...])
    acs_all = _cumsum_rows(tri_blk, dt_all * -jnp.exp(alog_ref[...]))

    def expand_heads(q):
        q0, q1, q2 = _split3(q)
        de = lambda r: jnp.dot(r, expand, preferred_element_type=F32)
        return de(q0) + de(q1) + de(q2)

    st = st_ref[...]
    for c in range(n_chunks):
        sl = slice(c * CHUNK, (c + 1) * CHUNK)
        xs = axs_ref[sl, :]
        bm = abm_ref[sl, :]
        cm = acm_ref[sl, :]
        dt_exp = expand_heads(dt_all[sl, :])
        acs = expand_heads(acs_all[sl, :])
        rowpart = jnp.sum(jnp.where(eye_t, acs, 0.0), axis=0, keepdims=True)
        decay = jnp.exp(jnp.where(causal_t, acs - rowpart, -jnp.inf))
        xdt = xs * dt_exp
        cb2 = _dot_nt(cm, jnp.concatenate([bm, bm], axis=0))
        last = acs[CHUNK - 1:CHUNK, :]
        y_off = _dot(cm, st) * jnp.exp(acs)
        parts = []
        for j in range(gc // LANES):
            lanes = slice(j * LANES, (j + 1) * LANES)
            parts.append(_dot(cb2 * decay[:, lanes], _stack_heads(xdt[:, lanes], top)))
        y = jnp.concatenate(parts, axis=1) + y_off + d_row * xs
        st = st * jnp.exp(last) + _dot_tn(bm, xdt * jnp.exp(last - acs))
        y = y * _silu(z_ref[0, sl, :])
        y = y * lax.rsqrt(jnp.mean(y * y, axis=-1, keepdims=True) + SSD_NORM_EPS)
        o_ref[0, sl, :] = (y * ng_row).astype(o_ref.dtype)
    st_ref[...] = st


def _ssd(zx, small, conv_w, conv_b, dt_bias, a_log, d_skip, norm_g, side, *, tb=256):
    bsz, t, _ = zx.shape
    gc = SSD_GROUP_COLS
    xs_blk0 = SSD_D_INNER // gc
    bm_blk0 = 2 * SSD_D_INNER // SSD_STATE
    cm_blk0 = bm_blk0 + SSD_GROUPS
    wbm_blk0 = SSD_D_INNER // SSD_STATE
    wcm_blk0 = wbm_blk0 + SSD_GROUPS
    conv_b = conv_b.reshape(1, SSD_CONV_DIM)
    per_group = lambda v: jnp.repeat(v, SSD_HEAD_DIM).reshape(SSD_GROUPS, 1, gc)
    seq = lambda w, f: pl.BlockSpec((1, tb, w), lambda b, g, ti: (b, ti, f(g)))
    chan = lambda r, w, f: pl.BlockSpec((r, w), lambda b, g, ti: (0, f(g)))
    grp = pl.BlockSpec((1, 1, gc), lambda b, g, ti: (g, 0, 0))
    nt_steps = t // tb
    s_in, s_out, s_shape = _side_specs(side, lambda b, g, ti: (b * SSD_GROUPS + g) * nt_steps + ti,
                                       bsz * SSD_GROUPS * nt_steps)
    return pl.pallas_call(
        functools.partial(_ssd_kernel, n_chunks=tb // CHUNK),
        grid=(bsz, SSD_GROUPS, t // tb),
        in_specs=[seq(gc, lambda g: xs_blk0 + g), seq(SSD_STATE, lambda g: bm_blk0 + g),
                  seq(SSD_STATE, lambda g: cm_blk0 + g), seq(SSD_HEADS, lambda g: SMALL_DT_BLK),
                  seq(gc, lambda g: g),
                  chan(SSD_CONV, gc, lambda g: g), chan(SSD_CONV, SSD_STATE, lambda g: wbm_blk0 + g),
                  chan(SSD_CONV, SSD_STATE, lambda g: wcm_blk0 + g),
                  chan(1, gc, lambda g: g), chan(1, SSD_STATE, lambda g: wbm_blk0 + g),
                  chan(1, SSD_STATE, lambda g: wcm_blk0 + g),
                  chan(1, SSD_HEADS, lambda g: 0), chan(1, SSD_HEADS, lambda g: 0), grp,
                  chan(1, gc, lambda g: g), s_in],
        out_specs=[seq(gc, lambda g: g), s_out],
        out_shape=[jax.ShapeDtypeStruct((bsz, t, SSD_D_INNER), BF16), s_shape],
        scratch_shapes=[pltpu.VMEM((SSD_STATE, gc), F32),
                        pltpu.VMEM((SUBLANES + tb, gc), F32), pltpu.VMEM((SUBLANES + tb, SSD_STATE), F32),
                        pltpu.VMEM((SUBLANES + tb, SSD_STATE), F32),
                        pltpu.VMEM((tb, gc), F32), pltpu.VMEM((tb, SSD_STATE), F32), pltpu.VMEM((tb, SSD_STATE), F32)],
        name="ssd_scan",
        compiler_params=_params("arbitrary", "arbitrary", "arbitrary"),
    )(zx, zx, zx, small, zx, conv_w, conv_w, conv_w, conv_b, conv_b, conv_b, dt_bias.reshape(1, SSD_HEADS),
      a_log.reshape(1, SSD_HEADS), per_group(d_skip), norm_g.reshape(1, SSD_D_INNER), side[0])


def _wkv_kernel(r_ref, k_ref, v_ref, glo_ref, wlo_ref, alo_ref,
                mur_ref, muk_ref, muv_ref, mug_ref, muw_ref, mua_ref,
                w2_ref, a2_ref, g2_ref, w0_ref, a0_ref, kk_ref, ka_ref, rk_ref, lng_ref, lnb_ref, side_in_ref,
                o_ref, side_out_ref,
                s_ref, prev_ref, plo_ref, rr_ref, kr_ref, vv_ref, lw_ref, kn_ref, ag_ref, gg_ref,
                *, n_chunks, n_pairs):
    tb = n_chunks * CHUNK
    glo_w = glo_ref.shape[2]
    side_out_ref[...] = side_in_ref[...].astype(BF16)

    @pl.when(pl.program_id(2) == 0)
    def _():
        s_ref[...] = jnp.zeros_like(s_ref)
        prev_ref[...] = jnp.zeros_like(prev_ref)
        plo_ref[...] = jnp.zeros_like(plo_ref)

    first_row = lax.broadcasted_iota(jnp.int32, (tb, 1), 0) == 0

    def lerp(x, prev_row, mu):
        shifted = jnp.where(first_row, prev_row, pltpu.roll(x, 1, 0))
        return x + (shifted - x) * mu

    r_raw, k_raw, v_raw = r_ref[0], k_ref[0], v_ref[0]
    glo_raw, wlo_raw, alo_raw = glo_ref[0], wlo_ref[0], alo_ref[0]
    r = lerp(r_raw, prev_ref[0:1, :], mur_ref[...])
    k = lerp(k_raw, prev_ref[1:2, :], muk_ref[...])
    v = lerp(v_raw, prev_ref[2:3, :], muv_ref[...])
    g_lo = lerp(glo_raw, plo_ref[0:1, 0:glo_w], mug_ref[...])
    w_lo = lerp(wlo_raw, plo_ref[0:1, glo_w:glo_w + W_RANK], muw_ref[...])
    a_lo = lerp(alo_raw, plo_ref[0:1, glo_w + W_RANK:], mua_ref[...])
    prev_ref[0:1, :] = r_raw[tb - 1:tb, :]
    prev_ref[1:2, :] = k_raw[tb - 1:tb, :]
    prev_ref[2:3, :] = v_raw[tb - 1:tb, :]
    plo_ref[0:1, 0:glo_w] = glo_raw[tb - 1:tb, :]
    plo_ref[0:1, glo_w:glo_w + W_RANK] = wlo_raw[tb - 1:tb, :]
    plo_ref[0:1, glo_w + W_RANK:] = alo_raw[tb - 1:tb, :]

    lw = -DECAY_SCALE * _sigmoid(w0_ref[...] + jnp.dot(jnp.tanh(w_lo).astype(BF16), w2_ref[...],
                                                       preferred_element_type=F32))
    a_gate = _sigmoid(a0_ref[...] + jnp.dot(a_lo.astype(BF16), a2_ref[...], preferred_element_type=F32))
    gg_ref[...] = jnp.dot(_sigmoid(g_lo).astype(BF16), g2_ref[...], preferred_element_type=F32)
    top_blk = lax.broadcasted_iota(jnp.int32, (tb, LANES), 1) < RWKV_HEAD_DIM
    kk = k * kk_ref[...]
    for j in range(n_pairs):
        lanes = slice(j * LANES, (j + 1) * LANES)
        kj = kk[:, lanes]
        kn_ref[:, lanes] = kj / jnp.maximum(jnp.sqrt(_head_sum(kj * kj, top_blk)), L2_EPS)
    rr_ref[...] = r
    kr_ref[...] = k * (1.0 + (a_gate - 1.0) * ka_ref[...])
    vv_ref[...] = v
    lw_ref[...] = lw
    ag_ref[...] = a_gate

    row = lax.broadcasted_iota(jnp.int32, (LANES, LANES), 0)
    col = lax.broadcasted_iota(jnp.int32, (LANES, LANES), 1)
    same = (row >> 6) == (col >> 6)
    trow = lax.broadcasted_iota(jnp.int32, (tb, tb), 0)
    tcol = lax.broadcasted_iota(jnp.int32, (tb, tb), 1)
    tri_blk = ((trow >> 6) == (tcol >> 6)) & (trow >= tcol)
    top = lax.broadcasted_iota(jnp.int32, (CHUNK, LANES), 1) < RWKV_HEAD_DIM
    inv_n = 1.0 / RWKV_HEAD_DIM
    chains = [(j, c) for j in range(n_pairs) for c in range(n_chunks)]

    def piece(ref, j, c):
        return ref[c * CHUNK:(c + 1) * CHUNK, j * LANES:(j + 1) * LANES]

    cum_all = _cumsum_rows(tri_blk.astype(BF16), lw)

    lane_t = lax.broadcasted_iota(jnp.int32, (CHUNK, LANES), 1) & (CHUNK - 1)
    row_t = lax.broadcasted_iota(jnp.int32, (CHUNK, LANES), 0)
    strict_c = row_t > lane_t
    incl_c = row_t >= lane_t
    eye_c = (row_t == lane_t).astype(F32)
    strict2 = jnp.concatenate([strict_c, incl_c], axis=0)

    def bd(x):
        return _stack_heads(x.astype(BF16), top)

    nb = range(len(chains))
    a_c, r_c, b_bd, k_bd, v_c, bh_c, kh_c, wtot = [], [], [], [], [], [], [], []
    for j, c in chains:
        cum = cum_all[c * CHUNK:(c + 1) * CHUNK, j * LANES:(j + 1) * LANES]
        lwc = piece(lw_ref, j, c)
        kc = piece(kr_ref, j, c)
        knc = piece(kn_ref, j, c)
        b = knc * piece(ag_ref, j, c)
        tot = cum[CHUNK - 1:CHUNK, :]
        e_neg = jnp.exp(-cum)
        e_end = jnp.exp(tot - cum)
        a_c.append((-knc * jnp.exp(cum - lwc)).astype(BF16))
        r_c.append(piece(rr_ref, j, c) * jnp.exp(cum))
        b_bd.append(bd(b * e_neg))
        k_bd.append(bd(kc * e_neg))
        v_c.append(piece(vv_ref, j, c).astype(BF16))
        bh_c.append((b * e_end).astype(BF16))
        kh_c.append((kc * e_end).astype(BF16))
        wtot.append(jnp.exp(tot))
    ar = [jnp.concatenate([a_c[i], r_c[i].astype(BF16)], axis=0) for i in nb]
    sb = [jnp.where(strict2, _dot_nt(ar[i], b_bd[i]), 0.0) for i in nb]
    sk = [jnp.where(strict2, _dot_nt(ar[i], k_bd[i]), 0.0) for i in nb]
    n = [x[:CHUNK] for x in sb]
    rb = [x[CHUNK:].astype(BF16) for x in sb]
    v_bd = [bd(x) for x in v_c]
    tinv = [eye_c + x for x in n]
    p_bd = [bd(x) for x in n]
    npow = [_dot(n[i], p_bd[i]) for i in nb]
    for it in range(1, 6):
        p_bd = [bd(x) for x in npow]
        if it < 5:
            both = [_dot(jnp.concatenate([npow[i], tinv[i]], axis=0), p_bd[i]) for i in nb]
            npow = [x[:CHUNK] for x in both]
            tinv = [tinv[i] + both[i][CHUNK:] for i in nb]
        else:
            tinv = [tinv[i] + _dot(tinv[i], p_bd[i]) for i in nb]
    akv = [_dot(sk[i][:CHUNK], v_bd[i]) for i in nb]
    pq = [_dot(tinv[i], jnp.concatenate([bd(a_c[i]), bd(akv[i])], axis=1)) for i in nb]
    pq_bd = [jnp.concatenate([bd(x[:, :LANES]), bd(x[:, LANES:])], axis=1) for x in pq]
    rpq = [_dot(rb[i], pq_bd[i]) for i in nb]
    r_eff = [(r_c[i] + rpq[i][:, :LANES]).astype(BF16) for i in nb]
    y0 = [rpq[i][:, LANES:] + _dot(sk[i][CHUNK:], v_bd[i]) for i in nb]
    g_eff = [jnp.where(same, _dot_tn(pq[i][:, :LANES], bh_c[i]), 0.0).astype(BF16) for i in nb]
    s1 = [jnp.where(same, _dot_tn(jnp.concatenate([pq[i][:, LANES:].astype(BF16), v_c[i]], axis=0),
                                  jnp.concatenate([bh_c[i], kh_c[i]], axis=0)), 0.0) for i in nb]

    for j in range(n_pairs):
        lanes = slice(j * LANES, (j + 1) * LANES)
        rk_row = rk_ref[:, lanes]
        lng_row = lng_ref[:, lanes]
        lnb_row = lnb_ref[:, lanes]
        s = s_ref[j]
        for c in range(n_chunks):
            i = j * n_chunks + c
            y = _dot_nt(r_eff[i], s) + y0[i]
            s = s * wtot[i] + _dot(s, g_eff[i]) + s1[i]
            mean = _head_sum(y, top) * inv_n
            yc = y - mean
            var = _head_sum(yc * yc, top) * inv_n
            yn = yc * lax.rsqrt(var + RWKV_GN_EPS) * lng_row + lnb_row
            bonus = _head_sum(piece(rr_ref, j, c) * piece(kr_ref, j, c) * rk_row, top) * piece(vv_ref, j, c)
            o_ref[0, c * CHUNK:(c + 1) * CHUNK, lanes] = ((yn + bonus) * piece(gg_ref, j, c)).astype(o_ref.dtype)
        s_ref[j] = s


def _wkv(rkv, small, mu, w2, a2, g2p, w0, a0, k_k, k_a, r_k, ln_g, ln_b, side, *, tb=128, pairs=8):
    bsz, t, d3 = rkv.shape
    d = d3 // 3
    wl = pairs * LANES
    nh = d // wl
    mu_rkv = mu[:3 * d].reshape(1, 3 * d)
    mu_w = mu[3 * d:3 * d + W_RANK].reshape(1, W_RANK)
    mu_a = mu[3 * d + W_RANK:3 * d + W_RANK + A_RANK].reshape(1, A_RANK)
    mu_g = jnp.pad(mu[3 * d + W_RANK + A_RANK:], (0, G_RANK_PAD - G_RANK)).reshape(1, G_RANK_PAD)
    seq = lambda w, f: pl.BlockSpec((1, tb, w), lambda b, h, ti: (b, ti, f(h)))
    chan = lambda r, w, f: pl.BlockSpec((r, w), lambda b, h, ti: (0, f(h)))
    vec = lambda q: q.reshape(1, d)
    blk = pltpu.VMEM((tb, wl), F32)
    nt_steps = t // tb
    s_in, s_out, s_shape = _side_specs(side, lambda b, h, ti: (b * nh + h) * nt_steps + ti, bsz * nh * nt_steps)
    return pl.pallas_call(
        functools.partial(_wkv_kernel, n_chunks=tb // CHUNK, n_pairs=pairs),
        grid=(bsz, nh, t // tb),
        in_specs=[seq(wl, lambda h: h), seq(wl, lambda h: nh + h), seq(wl, lambda h: 2 * nh + h),
                  seq(G_RANK_PAD, lambda h: 0), seq(W_RANK, lambda h: SMALL_WLO_BLK),
                  seq(A_RANK, lambda h: SMALL_ALO_BLK),
                  chan(1, wl, lambda h: h), chan(1, wl, lambda h: nh + h), chan(1, wl, lambda h: 2 * nh + h),
                  chan(1, G_RANK_PAD, lambda h: 0), chan(1, W_RANK, lambda h: 0), chan(1, A_RANK, lambda h: 0),
                  chan(W_RANK, wl, lambda h: h), chan(A_RANK, wl, lambda h: h), chan(G_RANK_PAD, wl, lambda h: h)]
                 + [chan(1, wl, lambda h: h)] * 7 + [s_in],
        out_specs=[seq(wl, lambda h: h), s_out],
        out_shape=[jax.ShapeDtypeStruct((bsz, t, d), BF16), s_shape],
        scratch_shapes=[pltpu.VMEM((pairs, LANES, LANES), F32), pltpu.VMEM((SUBLANES, wl), F32),
                        pltpu.VMEM((SUBLANES, G_RANK_PAD + W_RANK + A_RANK), F32)] + [blk] * 7,
        name="wkv7_scan",
        compiler_params=_params("arbitrary", "arbitrary", "arbitrary"),
    )(rkv, rkv, rkv, small, small, small, mu_rkv, mu_rkv, mu_rkv, mu_g, mu_w, mu_a,
      w2, a2, g2p, vec(w0), vec(a0), vec(k_k), vec(k_a), vec(r_k), vec(ln_g), vec(ln_b), side[0])


def kernel(x, p, norm_mix_g, w_in, ssd_conv_w, ssd_conv_b, ssd_dt_bias, ssd_a_log, ssd_d, ssd_norm_g, rwkv_mu, rwkv_w0, rwkv_w2, rwkv_a0, rwkv_a2, rwkv_g2, rwkv_k_k, rwkv_k_a, rwkv_r_k, rwkv_ln_g, rwkv_ln_b, w_branch_ssd, w_branch_rwkv, w_out, norm_ffn_g, w_ff1, w_ff2, norm_ple_g, w_ple_gate, w_ple_proj, ple_post_g, final_norm_g):
    bsz, t, d = x.shape
    m = bsz * t
    depth = w_in.shape[0]
    xf = x.reshape(m, d)
    for i in range(depth):
        c_dt = SSD_D_INNER + SSD_CONV_DIM
        c_r = c_dt + SSD_HEADS
        c_wlo = c_r + 3 * d
        c_glo = c_wlo + W_RANK + A_RANK
        c_gate = c_glo + G_RANK
        w_in_t = jnp.swapaxes(w_in[i], 0, 1)
        n_head, n_rest = c_r, w_in.shape[2] - c_r
        w_head = w_in_t[:n_head].astype(BF16)
        g2p = jnp.pad(rwkv_g2[i], ((0, G_RANK_PAD - G_RANK), (0, 0))).astype(BF16)
        whole = lambda w, slab_rows: (w, 0, w.shape[0], slab_rows)

        h = _rmsnorm(xf, norm_mix_g[i], BF16)
        proj = functools.partial(_mm, h, w_nt=True, out_dtype=F32, tm=1024, tn=1024)
        zx, w_rest = proj(w_head, name="proj_zx", n=c_dt, tn=512, side=(w_in_t, n_head, n_rest, 96))
        rkv, w_bssd_b = proj(w_rest, name="proj_rkv", n=3 * d, side=whole(w_branch_ssd[i], 128))
        gates, w_brwkv_b = proj(w_rest, name="proj_gates", w_row0=c_gate - n_head, n=2 * d,
                                side=whole(w_branch_rwkv[i], 64))
        small = _proj_small(h, w_head, w_rest, dt_row0=c_dt, lo_row0=c_wlo - n_head).reshape(bsz, t, -1)
        zx = zx.reshape(bsz, t, -1)
        rkv = rkv.reshape(bsz, t, -1)

        u_ssd, w_ff1_b = _ssd(zx, small, ssd_conv_w[i], ssd_conv_b[i], ssd_dt_bias[i], ssd_a_log[i], ssd_d[i],
                              ssd_norm_g[i], whole(w_ff1[i], 16))
        u_rwkv, w_ff2_b = _wkv(rkv, small, rwkv_mu[i], rwkv_w2[i].astype(BF16), rwkv_a2[i].astype(BF16), g2p,
                               rwkv_w0[i], rwkv_a0[i], rwkv_k_k[i], rwkv_k_a[i], rwkv_r_k[i], rwkv_ln_g[i],
                               rwkv_ln_b[i], whole(w_ff2[i], 64))

        part, w_out_b = _mm(u_ssd.reshape(m, SSD_D_INNER), w_bssd_b, name="branch_ssd", out_dtype=F32, tm=1024,
                            tn=256, mul=gates, mul_act="sigmoid", mul_col0=0, side=whole(w_out[i], 32))
        merged = _mm(u_rwkv.reshape(m, d), w_brwkv_b, name="branch_rwkv", out_dtype=BF16, tm=1024, tn=512,
                     mul=gates, mul_act="sigmoid", mul_col0=d, res=part)
        xf, xg, ss = _mm(merged, w_out_b, name="out_proj", out_dtype=F32, tm=1024, tn=512, res=xf,
                         next_norm_g=norm_ffn_g[i])

        ff, w_ple_b = _mm(xg, w_ff1_b, name="ffn_up", out_dtype=BF16, tm=1024, tn=1024, act="relu2", row_ss=ss,
                          side=whole(w_ple_gate[i], 32))
        xf, xg, ss = _mm(ff, w_ff2_b, name="ffn_down", out_dtype=F32, tm=512, tn=256, res=xf,
                         next_norm_g=norm_ple_g[i])

        e = _ple_embed(p[i].reshape(m, PLE_DIM).astype(BF16), w_ple_proj[i].astype(BF16), ple_post_g[i])
        xf = _mm(xg, w_ple_b, name="ple_gate", out_dtype=F32, tm=1024, tn=512, act="sigmoid", mul=e, res=xf,
                 row_ss=ss)
    return _rmsnorm(xf, final_norm_g, F32).reshape(bsz, t, d)
```

```python
import functools
import math

import jax
import jax.numpy as jnp
from jax import lax
from jax.experimental import pallas as pl
from jax.experimental.pallas import tpu as pltpu

F32 = jnp.float32
BF16 = jnp.bfloat16

CHUNK = 64
SSD_D_INNER = 8192
SSD_HEAD_DIM = 64
SSD_HEADS = 128
SSD_GROUPS = 8
SSD_HPG = 16
SSD_STATE = 128
SSD_CONV = 4
SSD_GROUP_COLS = SSD_D_INNER // SSD_GROUPS
SSD_CONV_DIM = SSD_D_INNER + 2 * SSD_GROUPS * SSD_STATE
SSD_NORM_EPS = 1e-5
RWKV_HEAD_DIM = 64
W_RANK = 128
A_RANK = 128
G_RANK = 480
G_RANK_PAD = 512
DECAY_SCALE = math.exp(-0.5)
RWKV_GN_EPS = RWKV_HEAD_DIM * 1e-5
L2_EPS = 1e-12
LOG2E = math.log2(math.e)
NORM_EPS = 1e-6
PLE_DIM = 256

SMALL_COLS = G_RANK_PAD + SSD_HEADS + W_RANK + A_RANK
SMALL_LO_COLS = W_RANK + A_RANK + G_RANK_PAD
SMALL_DT_BLK = G_RANK_PAD // 128
SMALL_WLO_BLK = SMALL_DT_BLK + 1
SMALL_ALO_BLK = SMALL_DT_BLK + 2

LANES = 128
SUBLANES = 8
BF16_ROWS = 16
VMEM_LIMIT_BYTES = 56 * 1024 * 1024


def _params(*sem):
    return pltpu.CompilerParams(dimension_semantics=sem, vmem_limit_bytes=VMEM_LIMIT_BYTES)


def _split3(x):
    hi = x.astype(BF16)
    r1 = x - hi.astype(F32)
    mid = r1.astype(BF16)
    lo = (r1 - mid.astype(F32)).astype(BF16)
    return hi, mid, lo


def _dot(a, b, dims=(((1,), (0,)), ((), ()))):
    return lax.dot_general(a.astype(BF16), b.astype(BF16), dims, preferred_element_type=F32)


def _dot_nt(a, b):
    return _dot(a, b, (((1,), (1,)), ((), ())))


def _dot_tn(a, b):
    return _dot(a, b, (((0,), (0,)), ((), ())))


def _cumsum_rows(tri_bf16, x):
    x0, x1, x2 = _split3(x)
    d = lambda q: jnp.dot(tri_bf16, q, preferred_element_type=F32)
    return d(x0) + d(x1) + d(x2)


def _sigmoid(x):
    return 0.5 * jnp.tanh(0.5 * x) + 0.5


def _silu(x):
    h = 0.5 * x
    return h + h * jnp.tanh(h)


def _softplus(x):
    return jnp.maximum(x, 0.0) + jnp.log1p(jnp.exp(-jnp.abs(x)))


def _stack_heads(x, top):
    return jnp.concatenate([jnp.where(top, x, 0.0), jnp.where(top, 0.0, x)], axis=0)


def _head_sum(x, top):
    s0 = jnp.sum(jnp.where(top, x, 0.0), axis=-1, keepdims=True)
    s1 = jnp.sum(jnp.where(top, 0.0, x), axis=-1, keepdims=True)
    return jnp.where(top, s0, s1)


def _side_specs(side, step_of, n_steps):
    src, row0, nrows, rs = side
    cols = src.shape[1]
    nslabs = nrows // rs
    assert src.ndim == 2 and nrows % rs == 0 and nslabs <= n_steps and rs % BF16_ROWS == 0 and row0 % SUBLANES == 0
    slab = lambda *g: jnp.minimum(step_of(*g), nslabs - 1)
    in_spec = pl.BlockSpec((pl.Element(rs), pl.Element(cols)),
                           lambda *g: (pl.multiple_of(row0 + slab(*g) * rs, SUBLANES), 0))
    out_spec = pl.BlockSpec((rs, cols), lambda *g: (slab(*g), 0))
    return in_spec, out_spec, jax.ShapeDtypeStruct((nrows, cols), BF16)


def _rmsnorm_kernel(x_ref, g_ref, o_ref, *, eps):
    x = x_ref[...]
    y = x * lax.rsqrt(jnp.mean(x * x, axis=-1, keepdims=True) + eps)
    o_ref[...] = (y * g_ref[...]).astype(o_ref.dtype)


def _rmsnorm(x, g, out_dtype, tm=256):
    m, d = x.shape
    return pl.pallas_call(
        functools.partial(_rmsnorm_kernel, eps=NORM_EPS),
        grid=(m // tm,),
        in_specs=[pl.BlockSpec((tm, d), lambda i: (i, 0)), pl.BlockSpec((1, d), lambda i: (0, 0))],
        out_specs=pl.BlockSpec((tm, d), lambda i: (i, 0)),
        out_shape=jax.ShapeDtypeStruct((m, d), out_dtype),
        name="rmsnorm",
        compiler_params=_params("parallel"),
    )(x, g.reshape(1, d))


def _act(x, kind):
    if kind is None:
        return x
    if kind == "relu2":
        r = jnp.maximum(x, 0.0)
        return r * r
    if kind == "sigmoid":
        return _sigmoid(x)
    raise ValueError(kind)


def _mm_kernel(*refs, nk, w_nt, act, has_mul, mul_act, has_res, has_scale, has_norm, has_side, norm_dim):
    refs = list(refs)
    x_ref, w_ref = refs[0], refs[1]
    pos = 2

    def take(flag):
        nonlocal pos
        if not flag:
            return None
        pos += 1
        return refs[pos - 1]

    mul_ref, res_ref, ss_in_ref, g_ref, side_in_ref = (take(f) for f in (has_mul, has_res, has_scale, has_norm, has_side))
    o_ref = take(True)
    xg_ref, ss_out_ref, side_out_ref = take(has_norm), take(has_norm), take(has_side)
    if has_side:
        side_out_ref[...] = side_in_ref[...].astype(BF16)
    acc_ref = None if nk == 1 else (o_ref if o_ref.dtype == F32 else refs[pos])

    if w_nt:
        part = lax.dot_general(x_ref[...], w_ref[...], (((1,), (1,)), ((), ())), preferred_element_type=F32)
    else:
        part = jnp.dot(x_ref[...], w_ref[...], preferred_element_type=F32)

    def finish(acc):
        if has_scale:
            acc = acc * lax.rsqrt(ss_in_ref[:, 0:1] * (1.0 / norm_dim) + NORM_EPS)
        out = _act(acc, act)
        if has_mul:
            out = out * _act(mul_ref[...].astype(F32), mul_act)
        if has_res:
            out = res_ref[...].astype(F32) + out
        o_ref[...] = out.astype(o_ref.dtype)
        if has_norm:
            xg_ref[...] = (out * g_ref[...]).astype(BF16)
            row_ss = jnp.broadcast_to(jnp.sum(out * out, axis=1, keepdims=True), ss_out_ref.shape)
            first = pl.program_id(1) == 0

            @pl.when(first)
            def _():
                ss_out_ref[...] = row_ss

            @pl.when(jnp.logical_not(first))
            def _():
                ss_out_ref[...] += row_ss

    if nk == 1:
        finish(part)
    else:
        k = pl.program_id(2)

        @pl.when(k == 0)
        def _():
            acc_ref[...] = part

        @pl.when((k > 0) & (k < nk - 1))
        def _():
            acc_ref[...] += part

        @pl.when(k == nk - 1)
        def _():
            finish(acc_ref[...] + part)


def _mm(x, w, *, name, out_dtype, tm, tn, tk=None, w_nt=False, w_row0=0, n=None, act=None, mul=None, mul_act=None,
        mul_col0=0, res=None, res_col0=0, side=None, row_ss=None, next_norm_g=None):
    m, kdim = x.shape
    n = w.shape[1] if n is None else n
    tk = kdim if tk is None else tk
    tm, tn = min(tm, m), min(tn, n)
    nk = kdim // tk
    assert m % tm == 0 and n % tn == 0 and kdim % tk == 0 and mul_col0 % tn == 0 and res_col0 % tn == 0
    if w_nt:
        assert w.shape[1] == kdim and w_row0 % BF16_ROWS == 0 and tn % BF16_ROWS == 0 and w.dtype == BF16
        w_spec = pl.BlockSpec((pl.Element(tn), pl.Element(tk)),
                              lambda i, j, k: (pl.multiple_of(w_row0 + j * tn, BF16_ROWS), k * tk))
    else:
        assert w.shape[0] == kdim and w_row0 == 0
        w_spec = pl.BlockSpec((tk, tn), lambda i, j, k: (k, j))
    in_specs = [pl.BlockSpec((tm, tk), lambda i, j, k: (i, k)), w_spec]
    args = [x, w]
    if mul is not None:
        off = mul_col0 // tn
        in_specs.append(pl.BlockSpec((tm, tn), lambda i, j, k, off=off: (i, j + off)))
        args.append(mul)
    if res is not None:
        off = res_col0 // tn
        in_specs.append(pl.BlockSpec((tm, tn), lambda i, j, k, off=off: (i, j + off)))
        args.append(res)
    if row_ss is not None:
        in_specs.append(pl.BlockSpec((tm, LANES), lambda i, j, k: (i, 0)))
        args.append(row_ss)
    grid = (m // tm, n // tn, nk)
    out_specs = [pl.BlockSpec((tm, tn), lambda i, j, k: (i, j))]
    out_shape = [jax.ShapeDtypeStruct((m, n), out_dtype)]
    if next_norm_g is not None:
        assert nk == 1 and out_dtype == F32
        in_specs.append(pl.BlockSpec((1, tn), lambda i, j, k: (0, j)))
        args.append(next_norm_g.reshape(1, n))
        out_specs += [pl.BlockSpec((tm, tn), lambda i, j, k: (i, j)), pl.BlockSpec((tm, LANES), lambda i, j, k: (i, 0))]
        out_shape += [jax.ShapeDtypeStruct((m, n), BF16), jax.ShapeDtypeStruct((m, LANES), F32)]
    if side is not None:
        s_in, s_out, s_shape = _side_specs(side, lambda i, j, k: (i * grid[1] + j) * nk + k, grid[0] * grid[1] * nk)
        in_specs.append(s_in)
        args.append(side[0])
        out_specs.append(s_out)
        out_shape.append(s_shape)
    outs = pl.pallas_call(
        functools.partial(_mm_kernel, nk=nk, w_nt=w_nt, act=act, has_mul=mul is not None, mul_act=mul_act,
                          has_res=res is not None, has_scale=row_ss is not None, has_norm=next_norm_g is not None,
                          has_side=side is not None, norm_dim=kdim),
        grid=grid,
        in_specs=in_specs,
        out_specs=out_specs,
        out_shape=out_shape,
        scratch_shapes=[pltpu.VMEM((tm, tn), F32)] if nk > 1 and out_dtype != F32 else [],
        name=name,
        compiler_params=_params(*(("parallel", "parallel") if len(out_shape) == 1 else ("arbitrary", "arbitrary")),
                                "arbitrary"),
    )(*args)
    return outs[0] if len(outs) == 1 else tuple(outs)


def _proj_small_kernel(x_ref, wdt_ref, wlo_ref, o_ref):
    x = x_ref[...]
    nt = (((1,), (1,)), ((), ()))
    dt = lax.dot_general(x, wdt_ref[...], nt, preferred_element_type=F32)
    lo = lax.dot_general(x, wlo_ref[...], nt, preferred_element_type=F32)
    o_ref[:, 0:G_RANK_PAD] = lo[:, W_RANK + A_RANK:]
    o_ref[:, G_RANK_PAD:G_RANK_PAD + SSD_HEADS] = dt
    o_ref[:, G_RANK_PAD + SSD_HEADS:] = lo[:, :W_RANK + A_RANK]


def _proj_small(x, wt_dt, wt_lo, *, dt_row0, lo_row0, tm=512):
    m, kdim = x.shape
    win = lambda rows, row0: pl.BlockSpec((pl.Element(rows), pl.Element(kdim)), lambda i: (row0, 0))
    return pl.pallas_call(
        _proj_small_kernel,
        grid=(m // tm,),
        in_specs=[pl.BlockSpec((tm, kdim), lambda i: (i, 0)), win(SSD_HEADS, dt_row0), win(SMALL_LO_COLS, lo_row0)],
        out_specs=pl.BlockSpec((tm, SMALL_COLS), lambda i: (i, 0)),
        out_shape=jax.ShapeDtypeStruct((m, SMALL_COLS), F32),
        name="proj_small",
        compiler_params=_params("parallel"),
    )(x, wt_dt, wt_lo)


def _ple_embed_kernel(p_ref, w_ref, g_ref, o_ref):
    e = jnp.dot(p_ref[...], w_ref[...], preferred_element_type=F32)
    y = e * lax.rsqrt(jnp.mean(e * e, axis=-1, keepdims=True) + NORM_EPS)
    o_ref[...] = (y * g_ref[...]).astype(o_ref.dtype)


def _ple_embed(p, w, g, tm=256):
    m, kdim = p.shape
    n = w.shape[1]
    return pl.pallas_call(
        _ple_embed_kernel,
        grid=(m // tm,),
        in_specs=[pl.BlockSpec((tm, kdim), lambda i: (i, 0)), pl.BlockSpec((kdim, n), lambda i: (0, 0)),
                  pl.BlockSpec((1, n), lambda i: (0, 0))],
        out_specs=pl.BlockSpec((tm, n), lambda i: (i, 0)),
        out_shape=jax.ShapeDtypeStruct((m, n), F32),
        name="ple_embed",
        compiler_params=_params("parallel"),
    )(p, w, g.reshape(1, n))


def _ssd_kernel(xs_ref, bm_ref, cm_ref, dt_ref, z_ref, wxs_ref, wbm_ref, wcm_ref, bxs_ref, bbm_ref, bcm_ref,
                dtb_ref, alog_ref, d_ref, ng_ref, side_in_ref, o_ref, side_out_ref,
                st_ref, hxs_ref, hbm_ref, hcm_ref, axs_ref, abm_ref, acm_ref, *, n_chunks):
    tb = n_chunks * CHUNK
    hist = SUBLANES
    side_out_ref[...] = side_in_ref[...].astype(BF16)

    @pl.when(pl.program_id(2) == 0)
    def _():
        st_ref[...] = jnp.zeros_like(st_ref)
        for h_ref in (hxs_ref, hbm_ref, hcm_ref):
            h_ref[0:hist, :] = jnp.zeros((hist, h_ref.shape[1]), F32)

    for raw_ref, h_ref, w_ref, b_ref, act_ref in ((xs_ref, hxs_ref, wxs_ref, bxs_ref, axs_ref),
                                                  (bm_ref, hbm_ref, wbm_ref, bbm_ref, abm_ref),
                                                  (cm_ref, hcm_ref, wcm_ref, bcm_ref, acm_ref)):
        h_ref[hist:hist + tb, :] = raw_ref[0]
        hv = h_ref[...]
        acc = b_ref[...] + w_ref[SSD_CONV - 1:SSD_CONV, :] * hv[hist:]
        for j in range(SSD_CONV - 1):
            acc = acc + w_ref[j:j + 1, :] * pltpu.roll(hv, SSD_CONV - 1 - j, 0)[hist:]
        act_ref[...] = _silu(acc)
        h_ref[0:hist, :] = raw_ref[0, tb - hist:tb, :]

    gc = SSD_GROUP_COLS
    g = pl.program_id(1)
    expand = (lax.broadcasted_iota(jnp.int32, (SSD_HEADS, gc), 0)
              == (lax.broadcasted_iota(jnp.int32, (SSD_HEADS, gc), 1) >> 6) + g * SSD_HPG).astype(BF16)
    li = lax.broadcasted_iota(jnp.int32, (CHUNK, gc), 0)
    si = lax.broadcasted_iota(jnp.int32, (CHUNK, gc), 1) & (CHUNK - 1)
    eye_t = li == si
    causal_t = li >= si
    top = lax.broadcasted_iota(jnp.int32, (CHUNK, LANES), 1) < SSD_HEAD_DIM
    d_row = d_ref[0]
    ng_row = ng_ref[...]
    trow = lax.broadcasted_iota(jnp.int32, (tb, tb), 0)
    tcol = lax.broadcasted_iota(jnp.int32, (tb, tb), 1)
    tri_blk = (((trow >> 6) == (tcol >> 6)) & (trow >= tcol)).astype(BF16)
    dt_all = _softplus(dt_ref[0] + dtb_ref[...])
    acs_all = _cumsum_rows(tri_blk, dt_all * (-LOG2E * jnp.exp(alog_ref[...])))

    def expand_heads(q):
        q0, q1, q2 = _split3(q)
        de = lambda r: jnp.dot(r, expand, preferred_element_type=F32)
        return de(q0) + de(q1) + de(q2)

    st = st_ref[...]
    for c in range(n_chunks):
        sl = slice(c * CHUNK, (c + 1) * CHUNK)
        xs = axs_ref[sl, :]
        bm = abm_ref[sl, :]
        cm = acm_ref[sl, :]
        dt_exp = expand_heads(dt_all[sl, :])
        acs = expand_heads(acs_all[sl, :])
        rowpart = jnp.sum(jnp.where(eye_t, acs, 0.0), axis=0, keepdims=True)
        decay = jnp.exp2(jnp.where(causal_t, acs - rowpart, -jnp.inf))
        xdt = xs * dt_exp
        cb2 = _dot_nt(cm, jnp.concatenate([bm, bm], axis=0))
        last = acs[CHUNK - 1:CHUNK, :]
        y_off = _dot(cm, st) * jnp.exp2(acs)
        parts = []
        for j in range(gc // LANES):
            lanes = slice(j * LANES, (j + 1) * LANES)
            parts.append(_dot(cb2 * decay[:, lanes], _stack_heads(xdt[:, lanes], top)))
        y = jnp.concatenate(parts, axis=1) + y_off + d_row * xs
        st = st * jnp.exp2(last) + _dot_tn(bm, xdt * jnp.exp2(last - acs))
        y = y * _silu(z_ref[0, sl, :])
        y = y * lax.rsqrt(jnp.mean(y * y, axis=-1, keepdims=True) + SSD_NORM_EPS)
        o_ref[0, sl, :] = (y * ng_row).astype(o_ref.dtype)
    st_ref[...] = st


def _ssd(zx, small, conv_w, conv_b, dt_bias, a_log, d_skip, norm_g, side, *, tb=256):
    bsz, t, _ = zx.shape
    gc = SSD_GROUP_COLS
    xs_blk0 = SSD_D_INNER // gc
    bm_blk0 = 2 * SSD_D_INNER // SSD_STATE
    cm_blk0 = bm_blk0 + SSD_GROUPS
    wbm_blk0 = SSD_D_INNER // SSD_STATE
    wcm_blk0 = wbm_blk0 + SSD_GROUPS
    conv_b = conv_b.reshape(1, SSD_CONV_DIM)
    per_group = lambda v: jnp.repeat(v, SSD_HEAD_DIM).reshape(SSD_GROUPS, 1, gc)
    seq = lambda w, f: pl.BlockSpec((1, tb, w), lambda b, g, ti: (b, ti, f(g)))
    chan = lambda r, w, f: pl.BlockSpec((r, w), lambda b, g, ti: (0, f(g)))
    grp = pl.BlockSpec((1, 1, gc), lambda b, g, ti: (g, 0, 0))
    nt_steps = t // tb
    s_in, s_out, s_shape = _side_specs(side, lambda b, g, ti: (b * SSD_GROUPS + g) * nt_steps + ti,
                                       bsz * SSD_GROUPS * nt_steps)
    return pl.pallas_call(
        functools.partial(_ssd_kernel, n_chunks=tb // CHUNK),
        grid=(bsz, SSD_GROUPS, t // tb),
        in_specs=[seq(gc, lambda g: xs_blk0 + g), seq(SSD_STATE, lambda g: bm_blk0 + g),
                  seq(SSD_STATE, lambda g: cm_blk0 + g), seq(SSD_HEADS, lambda g: SMALL_DT_BLK),
                  seq(gc, lambda g: g),
                  chan(SSD_CONV, gc, lambda g: g), chan(SSD_CONV, SSD_STATE, lambda g: wbm_blk0 + g),
                  chan(SSD_CONV, SSD_STATE, lambda g: wcm_blk0 + g),
                  chan(1, gc, lambda g: g), chan(1, SSD_STATE, lambda g: wbm_blk0 + g),
                  chan(1, SSD_STATE, lambda g: wcm_blk0 + g),
                  chan(1, SSD_HEADS, lambda g: 0), chan(1, SSD_HEADS, lambda g: 0), grp,
                  chan(1, gc, lambda g: g), s_in],
        out_specs=[seq(gc, lambda g: g), s_out],
        out_shape=[jax.ShapeDtypeStruct((bsz, t, SSD_D_INNER), BF16), s_shape],
        scratch_shapes=[pltpu.VMEM((SSD_STATE, gc), F32),
                        pltpu.VMEM((SUBLANES + tb, gc), F32), pltpu.VMEM((SUBLANES + tb, SSD_STATE), F32),
                        pltpu.VMEM((SUBLANES + tb, SSD_STATE), F32),
                        pltpu.VMEM((tb, gc), F32), pltpu.VMEM((tb, SSD_STATE), F32), pltpu.VMEM((tb, SSD_STATE), F32)],
        name="ssd_scan",
        compiler_params=_params("arbitrary", "arbitrary", "arbitrary"),
    )(zx, zx, zx, small, zx, conv_w, conv_w, conv_w, conv_b, conv_b, conv_b, dt_bias.reshape(1, SSD_HEADS),
      a_log.reshape(1, SSD_HEADS), per_group(d_skip), norm_g.reshape(1, SSD_D_INNER), side[0])


def _wkv_kernel(r_ref, k_ref, v_ref, glo_ref, wlo_ref, alo_ref,
                mur_ref, muk_ref, muv_ref, mug_ref, muw_ref, mua_ref,
                w2_ref, a2_ref, g2_ref, w0_ref, a0_ref, kk_ref, ka_ref, rk_ref, lng_ref, lnb_ref, side_in_ref,
                o_ref, side_out_ref,
                s_ref, prev_ref, plo_ref, rr_ref, kr_ref, vv_ref, lw_ref, kn_ref, ag_ref, gg_ref,
                *, n_chunks, n_pairs):
    tb = n_chunks * CHUNK
    glo_w = glo_ref.shape[2]
    side_out_ref[...] = side_in_ref[...].astype(BF16)

    @pl.when(pl.program_id(2) == 0)
    def _():
        s_ref[...] = jnp.zeros_like(s_ref)
        prev_ref[...] = jnp.zeros_like(prev_ref)
        plo_ref[...] = jnp.zeros_like(plo_ref)

    first_row = lax.broadcasted_iota(jnp.int32, (tb, 1), 0) == 0

    def lerp(x, prev_row, mu):
        shifted = jnp.where(first_row, prev_row, pltpu.roll(x, 1, 0))
        return x + (shifted - x) * mu

    r_raw, k_raw, v_raw = r_ref[0], k_ref[0], v_ref[0]
    glo_raw, wlo_raw, alo_raw = glo_ref[0], wlo_ref[0], alo_ref[0]
    r = lerp(r_raw, prev_ref[0:1, :], mur_ref[...])
    k = lerp(k_raw, prev_ref[1:2, :], muk_ref[...])
    v = lerp(v_raw, prev_ref[2:3, :], muv_ref[...])
    g_lo = lerp(glo_raw, plo_ref[0:1, 0:glo_w], mug_ref[...])
    w_lo = lerp(wlo_raw, plo_ref[0:1, glo_w:glo_w + W_RANK], muw_ref[...])
    a_lo = lerp(alo_raw, plo_ref[0:1, glo_w + W_RANK:], mua_ref[...])
    prev_ref[0:1, :] = r_raw[tb - 1:tb, :]
    prev_ref[1:2, :] = k_raw[tb - 1:tb, :]
    prev_ref[2:3, :] = v_raw[tb - 1:tb, :]
    plo_ref[0:1, 0:glo_w] = glo_raw[tb - 1:tb, :]
    plo_ref[0:1, glo_w:glo_w + W_RANK] = wlo_raw[tb - 1:tb, :]
    plo_ref[0:1, glo_w + W_RANK:] = alo_raw[tb - 1:tb, :]

    lw = (-DECAY_SCALE * LOG2E) * _sigmoid(w0_ref[...] + jnp.dot(jnp.tanh(w_lo).astype(BF16), w2_ref[...],
                                                                 preferred_element_type=F32))
    a_gate = _sigmoid(a0_ref[...] + jnp.dot(a_lo.astype(BF16), a2_ref[...], preferred_element_type=F32))
    gg_ref[...] = jnp.dot(_sigmoid(g_lo).astype(BF16), g2_ref[...], preferred_element_type=F32)
    top_blk = lax.broadcasted_iota(jnp.int32, (tb, LANES), 1) < RWKV_HEAD_DIM
    kk = k * kk_ref[...]
    for j in range(n_pairs):
        lanes = slice(j * LANES, (j + 1) * LANES)
        kj = kk[:, lanes]
        kn_ref[:, lanes] = kj * lax.rsqrt(jnp.maximum(_head_sum(kj * kj, top_blk), L2_EPS * L2_EPS))
    rr_ref[...] = r
    kr_ref[...] = k * (1.0 + (a_gate - 1.0) * ka_ref[...])
    vv_ref[...] = v
    lw_ref[...] = lw
    ag_ref[...] = a_gate

    row = lax.broadcasted_iota(jnp.int32, (LANES, LANES), 0)
    col = lax.broadcasted_iota(jnp.int32, (LANES, LANES), 1)
    same = (row >> 6) == (col >> 6)
    trow = lax.broadcasted_iota(jnp.int32, (tb, tb), 0)
    tcol = lax.broadcasted_iota(jnp.int32, (tb, tb), 1)
    tri_blk = ((trow >> 6) == (tcol >> 6)) & (trow >= tcol)
    top = lax.broadcasted_iota(jnp.int32, (CHUNK, LANES), 1) < RWKV_HEAD_DIM
    inv_n = 1.0 / RWKV_HEAD_DIM
    chains = [(j, c) for j in range(n_pairs) for c in range(n_chunks)]

    def piece(ref, j, c):
        return ref[c * CHUNK:(c + 1) * CHUNK, j * LANES:(j + 1) * LANES]

    cum_all = _cumsum_rows(tri_blk.astype(BF16), lw)

    lane_t = lax.broadcasted_iota(jnp.int32, (CHUNK, LANES), 1) & (CHUNK - 1)
    row_t = lax.broadcasted_iota(jnp.int32, (CHUNK, LANES), 0)
    strict_c = row_t > lane_t
    incl_c = row_t >= lane_t
    eye_c = (row_t == lane_t).astype(F32)
    strict2 = jnp.concatenate([strict_c, incl_c], axis=0)

    def bd(x):
        return _stack_heads(x.astype(BF16), top)

    nb = range(len(chains))
    a_c, r_c, b_bd, k_bd, v_c, bh_c, kh_c, wtot = [], [], [], [], [], [], [], []
    for j, c in chains:
        cum = cum_all[c * CHUNK:(c + 1) * CHUNK, j * LANES:(j + 1) * LANES]
        lwc = piece(lw_ref, j, c)
        kc = piece(kr_ref, j, c)
        knc = piece(kn_ref, j, c)
        b = knc * piece(ag_ref, j, c)
        tot = cum[CHUNK - 1:CHUNK, :]
        e_neg = jnp.exp2(-cum)
        e_end = jnp.exp2(tot - cum)
        a_c.append((-knc * jnp.exp2(cum - lwc)).astype(BF16))
        r_c.append(piece(rr_ref, j, c) * jnp.exp2(cum))
        b_bd.append(bd(b * e_neg))
        k_bd.append(bd(kc * e_neg))
        v_c.append(piece(vv_ref, j, c).astype(BF16))
        bh_c.append((b * e_end).astype(BF16))
        kh_c.append((kc * e_end).astype(BF16))
        wtot.append(jnp.exp2(tot))
    ar = [jnp.concatenate([a_c[i], r_c[i].astype(BF16)], axis=0) for i in nb]
    sb = [jnp.where(strict2, _dot_nt(ar[i], b_bd[i]), 0.0) for i in nb]
    sk = [jnp.where(strict2, _dot_nt(ar[i], k_bd[i]), 0.0) for i in nb]
    n = [x[:CHUNK] for x in sb]
    rb = [x[CHUNK:].astype(BF16) for x in sb]
    v_bd = [bd(x) for x in v_c]
    tinv = [eye_c + x for x in n]
    p_bd = [bd(x) for x in n]
    npow = [_dot(n[i], p_bd[i]) for i in nb]
    for it in range(1, 6):
        p_bd = [bd(x) for x in npow]
        if it < 5:
            both = [_dot(jnp.concatenate([npow[i], tinv[i]], axis=0), p_bd[i]) for i in nb]
            npow = [x[:CHUNK] for x in both]
            tinv = [tinv[i] + both[i][CHUNK:] for i in nb]
        else:
            tinv = [tinv[i] + _dot(tinv[i], p_bd[i]) for i in nb]
    akv = [_dot(sk[i][:CHUNK], v_bd[i]) for i in nb]
    pq = [_dot(tinv[i], jnp.concatenate([bd(a_c[i]), bd(akv[i])], axis=1)) for i in nb]
    pq_bd = [jnp.concatenate([bd(x[:, :LANES]), bd(x[:, LANES:])], axis=1) for x in pq]
    rpq = [_dot(rb[i], pq_bd[i]) for i in nb]
    r_eff = [(r_c[i] + rpq[i][:, :LANES]).astype(BF16) for i in nb]
    y0 = [rpq[i][:, LANES:] + _dot(sk[i][CHUNK:], v_bd[i]) for i in nb]
    g_eff = [jnp.where(same, _dot_tn(pq[i][:, :LANES], bh_c[i]), 0.0).astype(BF16) for i in nb]
    s1 = [jnp.where(same, _dot_tn(jnp.concatenate([pq[i][:, LANES:].astype(BF16), v_c[i]], axis=0),
                                  jnp.concatenate([bh_c[i], kh_c[i]], axis=0)), 0.0) for i in nb]

    for j in range(n_pairs):
        lanes = slice(j * LANES, (j + 1) * LANES)
        rk_row = rk_ref[:, lanes]
        lng_row = lng_ref[:, lanes]
        lnb_row = lnb_ref[:, lanes]
        s = s_ref[j]
        for c in range(n_chunks):
            i = j * n_chunks + c
            y = _dot_nt(r_eff[i], s) + y0[i]
            s = s * wtot[i] + _dot(s, g_eff[i]) + s1[i]
            mean = _head_sum(y, top) * inv_n
            yc = y - mean
            var = _head_sum(yc * yc, top) * inv_n
            yn = yc * lax.rsqrt(var + RWKV_GN_EPS) * lng_row + lnb_row
            bonus = _head_sum(piece(rr_ref, j, c) * piece(kr_ref, j, c) * rk_row, top) * piece(vv_ref, j, c)
            o_ref[0, c * CHUNK:(c + 1) * CHUNK, lanes] = ((yn + bonus) * piece(gg_ref, j, c)).astype(o_ref.dtype)
        s_ref[j] = s


def _wkv(rkv, small, mu, w2, a2, g2p, w0, a0, k_k, k_a, r_k, ln_g, ln_b, side, *, tb=256, pairs=8):
    bsz, t, d3 = rkv.shape
    d = d3 // 3
    wl = pairs * LANES
    nh = d // wl
    mu_rkv = mu[:3 * d].reshape(1, 3 * d)
    mu_w = mu[3 * d:3 * d + W_RANK].reshape(1, W_RANK)
    mu_a = mu[3 * d + W_RANK:3 * d + W_RANK + A_RANK].reshape(1, A_RANK)
    mu_g = jnp.pad(mu[3 * d + W_RANK + A_RANK:], (0, G_RANK_PAD - G_RANK)).reshape(1, G_RANK_PAD)
    seq = lambda w, f: pl.BlockSpec((1, tb, w), lambda b, h, ti: (b, ti, f(h)))
    chan = lambda r, w, f: pl.BlockSpec((r, w), lambda b, h, ti: (0, f(h)))
    vec = lambda q: q.reshape(1, d)
    blk = pltpu.VMEM((tb, wl), F32)
    nt_steps = t // tb
    s_in, s_out, s_shape = _side_specs(side, lambda b, h, ti: (b * nh + h) * nt_steps + ti, bsz * nh * nt_steps)
    return pl.pallas_call(
        functools.partial(_wkv_kernel, n_chunks=tb // CHUNK, n_pairs=pairs),
        grid=(bsz, nh, t // tb),
        in_specs=[seq(wl, lambda h: h), seq(wl, lambda h: nh + h), seq(wl, lambda h: 2 * nh + h),
                  seq(G_RANK_PAD, lambda h: 0), seq(W_RANK, lambda h: SMALL_WLO_BLK),
                  seq(A_RANK, lambda h: SMALL_ALO_BLK),
                  chan(1, wl, lambda h: h), chan(1, wl, lambda h: nh + h), chan(1, wl, lambda h: 2 * nh + h),
                  chan(1, G_RANK_PAD, lambda h: 0), chan(1, W_RANK, lambda h: 0), chan(1, A_RANK, lambda h: 0),
                  chan(W_RANK, wl, lambda h: h), chan(A_RANK, wl, lambda h: h), chan(G_RANK_PAD, wl, lambda h: h)]
                 + [chan(1, wl, lambda h: h)] * 7 + [s_in],
        out_specs=[seq(wl, lambda h: h), s_out],
        out_shape=[jax.ShapeDtypeStruct((bsz, t, d), BF16), s_shape],
        scratch_shapes=[pltpu.VMEM((pairs, LANES, LANES), F32), pltpu.VMEM((SUBLANES, wl), F32),
                        pltpu.VMEM((SUBLANES, G_RANK_PAD + W_RANK + A_RANK), F32)] + [blk] * 7,
        name="wkv7_scan",
        compiler_params=_params("arbitrary", "arbitrary", "arbitrary"),
    )(rkv, rkv, rkv, small, small, small, mu_rkv, mu_rkv, mu_rkv, mu_g, mu_w, mu_a,
      w2, a2, g2p, vec(w0), vec(a0), vec(k_k), vec(k_a), vec(r_k), vec(ln_g), vec(ln_b), side[0])


def kernel(x, p, norm_mix_g, w_in, ssd_conv_w, ssd_conv_b, ssd_dt_bias, ssd_a_log, ssd_d, ssd_norm_g, rwkv_mu, rwkv_w0, rwkv_w2, rwkv_a0, rwkv_a2, rwkv_g2, rwkv_k_k, rwkv_k_a, rwkv_r_k, rwkv_ln_g, rwkv_ln_b, w_branch_ssd, w_branch_rwkv, w_out, norm_ffn_g, w_ff1, w_ff2, norm_ple_g, w_ple_gate, w_ple_proj, ple_post_g, final_norm_g):
    bsz, t, d = x.shape
    m = bsz * t
    depth = w_in.shape[0]
    xf = x.reshape(m, d)
    for i in range(depth):
        c_dt = SSD_D_INNER + SSD_CONV_DIM
        c_r = c_dt + SSD_HEADS
        c_wlo = c_r + 3 * d
        c_glo = c_wlo + W_RANK + A_RANK
        c_gate = c_glo + G_RANK
        w_in_t = jnp.swapaxes(w_in[i], 0, 1)
        n_head, n_rest = c_r, w_in.shape[2] - c_r
        w_head = w_in_t[:n_head].astype(BF16)
        g2p = jnp.pad(rwkv_g2[i], ((0, G_RANK_PAD - G_RANK), (0, 0))).astype(BF16)
        whole = lambda w, slab_rows: (w, 0, w.shape[0], slab_rows)

        h = _rmsnorm(xf, norm_mix_g[i], BF16)
        proj = functools.partial(_mm, h, w_nt=True, out_dtype=F32, tm=1024, tn=1024)
        zx, w_rest = proj(w_head, name="proj_zx", n=c_dt, tn=512, side=(w_in_t, n_head, n_rest, 96))
        rkv, w_bssd_b = proj(w_rest, name="proj_rkv", n=3 * d, side=whole(w_branch_ssd[i], 128))
        gates, w_brwkv_b = proj(w_rest, name="proj_gates", w_row0=c_gate - n_head, n=2 * d,
                                side=whole(w_branch_rwkv[i], 64))
        small = _proj_small(h, w_head, w_rest, dt_row0=c_dt, lo_row0=c_wlo - n_head).reshape(bsz, t, -1)
        zx = zx.reshape(bsz, t, -1)
        rkv = rkv.reshape(bsz, t, -1)

        u_ssd, w_ff1_b = _ssd(zx, small, ssd_conv_w[i], ssd_conv_b[i], ssd_dt_bias[i], ssd_a_log[i], ssd_d[i],
                              ssd_norm_g[i], whole(w_ff1[i], 16))
        u_rwkv, w_ff2_b = _wkv(rkv, small, rwkv_mu[i], rwkv_w2[i].astype(BF16), rwkv_a2[i].astype(BF16), g2p,
                               rwkv_w0[i], rwkv_a0[i], rwkv_k_k[i], rwkv_k_a[i], rwkv_r_k[i], rwkv_ln_g[i],
                               rwkv_ln_b[i], whole(w_ff2[i], 128))

        part, w_out_b = _mm(u_ssd.reshape(m, SSD_D_INNER), w_bssd_b, name="branch_ssd", out_dtype=F32, tm=1024,
                            tn=256, mul=gates, mul_act="sigmoid", mul_col0=0, side=whole(w_out[i], 32))
        merged = _mm(u_rwkv.reshape(m, d), w_brwkv_b, name="branch_rwkv", out_dtype=BF16, tm=1024, tn=512,
                     mul=gates, mul_act="sigmoid", mul_col0=d, res=part)
        xf, xg, ss = _mm(merged, w_out_b, name="out_proj", out_dtype=F32, tm=1024, tn=512, res=xf,
                         next_norm_g=norm_ffn_g[i])

        ff, w_ple_b = _mm(xg, w_ff1_b, name="ffn_up", out_dtype=BF16, tm=1024, tn=1024, act="relu2", row_ss=ss,
                          side=whole(w_ple_gate[i], 32))
        xf, xg, ss = _mm(ff, w_ff2_b, name="ffn_down", out_dtype=F32, tm=512, tn=256, res=xf,
                         next_norm_g=norm_ple_g[i])

        e = _ple_embed(p[i].reshape(m, PLE_DIM).astype(BF16), w_ple_proj[i].astype(BF16), ple_post_g[i])
        xf = _mm(xg, w_ple_b, name="ple_gate", out_dtype=F32, tm=1024, tn=512, act="sigmoid", mul=e, res=xf,
                 row_ss=ss)
    return _rmsnorm(xf, final_norm_g, F32).reshape(bsz, t, d)
```

```python
import functools
import math

import jax
import jax.numpy as jnp
from jax import lax
from jax.experimental import pallas as pl
from jax.experimental.pallas import tpu as pltpu

F32 = jnp.float32
BF16 = jnp.bfloat16

CHUNK = 64
LOG2_CHUNK = 6
SSD_D_INNER = 8192
SSD_HEAD_DIM = 64
SSD_HEADS = 128
SSD_GROUPS = 8
SSD_HPG = 16
SSD_STATE = 128
SSD_CONV = 4
SSD_GROUP_COLS = SSD_D_INNER // SSD_GROUPS
SSD_CONV_DIM = SSD_D_INNER + 2 * SSD_GROUPS * SSD_STATE
SSD_NORM_EPS = 1e-5
RWKV_HEAD_DIM = 64
W_RANK = 128
A_RANK = 128
G_RANK = 480
G_RANK_PAD = 512
DECAY_SCALE = math.exp(-0.5)
RWKV_GN_EPS = RWKV_HEAD_DIM * 1e-5
L2_EPS = 1e-12
LOG2E = math.log2(math.e)
NORM_EPS = 1e-6
PLE_DIM = 256

SMALL_COLS = G_RANK_PAD + SSD_HEADS + W_RANK + A_RANK
SMALL_LO_COLS = W_RANK + A_RANK + G_RANK_PAD
SMALL_DT_BLK = G_RANK_PAD // 128
SMALL_WLO_BLK = SMALL_DT_BLK + 1
SMALL_ALO_BLK = SMALL_DT_BLK + 2

LANES = 128
SUBLANES = 8
BF16_ROWS = 16
VMEM_LIMIT_BYTES = 56 * 1024 * 1024

TILE_WIDE = dict(tm=1024, tn=1024)
TILE_EPI = dict(tm=1024, tn=512)
TILE_K2D = dict(tm=1024, tn=256)
TILE_K4D = dict(tm=512, tn=256)


def _params(*sem):
    return pltpu.CompilerParams(dimension_semantics=sem, vmem_limit_bytes=VMEM_LIMIT_BYTES)


def _split3(x):
    hi = x.astype(BF16)
    r1 = x - hi.astype(F32)
    mid = r1.astype(BF16)
    lo = (r1 - mid.astype(F32)).astype(BF16)
    return hi, mid, lo


def _dot(a, b, dims=(((1,), (0,)), ((), ()))):
    return lax.dot_general(a.astype(BF16), b.astype(BF16), dims, preferred_element_type=F32)


def _dot_nt(a, b):
    return _dot(a, b, (((1,), (1,)), ((), ())))


def _dot_tn(a, b):
    return _dot(a, b, (((0,), (0,)), ((), ())))


def _cumsum_rows(tri_bf16, x):
    x0, x1, x2 = _split3(x)
    d = lambda q: jnp.dot(tri_bf16, q, preferred_element_type=F32)
    return d(x0) + d(x1) + d(x2)


def _sigmoid(x):
    return 0.5 * jnp.tanh(0.5 * x) + 0.5


def _silu(x):
    h = 0.5 * x
    return h + h * jnp.tanh(h)


def _softplus(x):
    return jnp.maximum(x, 0.0) + jnp.log1p(jnp.exp(-jnp.abs(x)))


def _stack_heads(x, top):
    return jnp.concatenate([jnp.where(top, x, 0.0), jnp.where(top, 0.0, x)], axis=0)


def _head_sum(x, top):
    s0 = jnp.sum(jnp.where(top, x, 0.0), axis=-1, keepdims=True)
    s1 = jnp.sum(jnp.where(top, 0.0, x), axis=-1, keepdims=True)
    return jnp.where(top, s0, s1)


def _side_specs(side, step_of, n_steps):
    src, row0, nrows = side[:3]
    panel = side[3] if len(side) > 3 else None
    cols = src.shape[1]
    assert src.ndim == 2 and nrows % BF16_ROWS == 0 and row0 % SUBLANES == 0
    rs = next(r for r in range(BF16_ROWS, nrows + 1, BF16_ROWS) if nrows % r == 0 and nrows // r <= n_steps)
    nslabs = nrows // rs
    slab = lambda *g: jnp.minimum(step_of(*g), nslabs - 1)
    in_spec = pl.BlockSpec((pl.Element(rs), pl.Element(cols)),
                           lambda *g: (pl.multiple_of(row0 + slab(*g) * rs, SUBLANES), 0))
    if panel is None:
        return in_spec, pl.BlockSpec((rs, cols), lambda *g: (slab(*g), 0)), jax.ShapeDtypeStruct((nrows, cols), BF16)
    assert cols % panel == 0 and panel % LANES == 0
    out_spec = pl.BlockSpec((cols // panel, rs, panel), lambda *g: (0, slab(*g), 0))
    return in_spec, out_spec, jax.ShapeDtypeStruct((cols // panel, nrows, panel), BF16)


def _side_convert(src_ref, dst_ref):
    if len(dst_ref.shape) == 2:
        dst_ref[...] = src_ref[...].astype(BF16)
    else:
        panel = dst_ref.shape[2]
        for c in range(dst_ref.shape[0]):
            dst_ref[c] = src_ref[:, c * panel:(c + 1) * panel].astype(BF16)


def _rmsnorm_kernel(x_ref, g_ref, o_ref, *, eps):
    x = x_ref[...]
    y = x * lax.rsqrt(jnp.mean(x * x, axis=-1, keepdims=True) + eps)
    o_ref[...] = (y * g_ref[...]).astype(o_ref.dtype)


def _rmsnorm(x, g, out_dtype, tm=256):
    m, d = x.shape
    return pl.pallas_call(
        functools.partial(_rmsnorm_kernel, eps=NORM_EPS),
        grid=(m // tm,),
        in_specs=[pl.BlockSpec((tm, d), lambda i: (i, 0)), pl.BlockSpec((1, d), lambda i: (0, 0))],
        out_specs=pl.BlockSpec((tm, d), lambda i: (i, 0)),
        out_shape=jax.ShapeDtypeStruct((m, d), out_dtype),
        name="rmsnorm",
        compiler_params=_params("parallel"),
    )(x, g.reshape(1, d))


def _act(x, kind):
    if kind is None:
        return x
    if kind == "relu2":
        r = jnp.maximum(x, 0.0)
        return r * r
    if kind == "sigmoid":
        return _sigmoid(x)
    raise ValueError(kind)


def _mm_kernel(*refs, nk, w_nt, act, has_mul, mul_act, has_res, has_scale, has_norm, has_side, norm_dim):
    refs = list(refs)
    x_ref, w_ref = refs[0], refs[1]
    pos = 2

    def take(flag):
        nonlocal pos
        if not flag:
            return None
        pos += 1
        return refs[pos - 1]

    mul_ref, res_ref, ss_in_ref, g_ref, side_in_ref = (take(f) for f in (has_mul, has_res, has_scale, has_norm, has_side))
    o_ref = take(True)
    xg_ref, ss_out_ref, side_out_ref = take(has_norm), take(has_norm), take(has_side)
    if has_side:
        _side_convert(side_in_ref, side_out_ref)
    acc_ref = None if nk == 1 else (o_ref if o_ref.dtype == F32 else refs[pos])

    if w_nt:
        part = lax.dot_general(x_ref[...], w_ref[...], (((1,), (1,)), ((), ())), preferred_element_type=F32)
    else:
        part = jnp.dot(x_ref[...], w_ref[...], preferred_element_type=F32)

    def finish(acc):
        if has_scale:
            acc = acc * lax.rsqrt(ss_in_ref[:, 0:1] * (1.0 / norm_dim) + NORM_EPS)
        out = _act(acc, act)
        if has_mul:
            out = out * _act(mul_ref[...].astype(F32), mul_act)
        if has_res:
            out = res_ref[...].astype(F32) + out
        o_ref[...] = out.astype(o_ref.dtype)
        if has_norm:
            xg_ref[...] = (out * g_ref[...]).astype(BF16)
            row_ss = jnp.broadcast_to(jnp.sum(out * out, axis=1, keepdims=True), ss_out_ref.shape)
            first = pl.program_id(1) == 0

            @pl.when(first)
            def _():
                ss_out_ref[...] = row_ss

            @pl.when(jnp.logical_not(first))
            def _():
                ss_out_ref[...] += row_ss

    if nk == 1:
        finish(part)
    else:
        k = pl.program_id(2)

        @pl.when(k == 0)
        def _():
            acc_ref[...] = part

        @pl.when((k > 0) & (k < nk - 1))
        def _():
            acc_ref[...] += part

        @pl.when(k == nk - 1)
        def _():
            finish(acc_ref[...] + part)


def _mm(x, w, *, name, out_dtype, tm, tn, tk=None, w_nt=False, w_row0=0, n=None, act=None, mul=None, mul_act=None,
        mul_col0=0, res=None, res_col0=0, side=None, row_ss=None, next_norm_g=None):
    m, kdim = x.shape
    panels = w.ndim == 3
    n = (w.shape[0] * w.shape[2] if panels else w.shape[1]) if n is None else n
    tk = kdim if tk is None else tk
    tm, tn = min(tm, m), min(tn, n)
    nk = kdim // tk
    assert m % tm == 0 and n % tn == 0 and kdim % tk == 0 and mul_col0 % tn == 0 and res_col0 % tn == 0
    if w_nt:
        assert w.shape[1] == kdim and w_row0 % BF16_ROWS == 0 and tn % BF16_ROWS == 0 and w.dtype == BF16
        w_spec = pl.BlockSpec((pl.Element(tn), pl.Element(tk)),
                              lambda i, j, k: (pl.multiple_of(w_row0 + j * tn, BF16_ROWS), k * tk))
    elif panels:
        assert w.shape[1:] == (kdim, tn) and w_row0 == 0
        w_spec = pl.BlockSpec((None, tk, tn), lambda i, j, k: (j, k, 0))
    else:
        assert w.shape[0] == kdim and w_row0 == 0
        w_spec = pl.BlockSpec((tk, tn), lambda i, j, k: (k, j))
    in_specs = [pl.BlockSpec((tm, tk), lambda i, j, k: (i, k)), w_spec]
    args = [x, w]
    if mul is not None:
        off = mul_col0 // tn
        in_specs.append(pl.BlockSpec((tm, tn), lambda i, j, k, off=off: (i, j + off)))
        args.append(mul)
    if res is not None:
        off = res_col0 // tn
        in_specs.append(pl.BlockSpec((tm, tn), lambda i, j, k, off=off: (i, j + off)))
        args.append(res)
    if row_ss is not None:
        in_specs.append(pl.BlockSpec((tm, LANES), lambda i, j, k: (i, 0)))
        args.append(row_ss)
    grid = (m // tm, n // tn, nk)
    out_specs = [pl.BlockSpec((tm, tn), lambda i, j, k: (i, j))]
    out_shape = [jax.ShapeDtypeStruct((m, n), out_dtype)]
    if next_norm_g is not None:
        assert nk == 1 and out_dtype == F32
        in_specs.append(pl.BlockSpec((1, tn), lambda i, j, k: (0, j)))
        args.append(next_norm_g.reshape(1, n))
        out_specs += [pl.BlockSpec((tm, tn), lambda i, j, k: (i, j)), pl.BlockSpec((tm, LANES), lambda i, j, k: (i, 0))]
        out_shape += [jax.ShapeDtypeStruct((m, n), BF16), jax.ShapeDtypeStruct((m, LANES), F32)]
    if side is not None:
        s_in, s_out, s_shape = _side_specs(side, lambda i, j, k: (i * grid[1] + j) * nk + k, grid[0] * grid[1] * nk)
        in_specs.append(s_in)
        args.append(side[0])
        out_specs.append(s_out)
        out_shape.append(s_shape)
    outs = pl.pallas_call(
        functools.partial(_mm_kernel, nk=nk, w_nt=w_nt, act=act, has_mul=mul is not None, mul_act=mul_act,
                          has_res=res is not None, has_scale=row_ss is not None, has_norm=next_norm_g is not None,
                          has_side=side is not None, norm_dim=kdim),
        grid=grid,
        in_specs=in_specs,
        out_specs=out_specs,
        out_shape=out_shape,
        scratch_shapes=[pltpu.VMEM((tm, tn), F32)] if nk > 1 and out_dtype != F32 else [],
        name=name,
        compiler_params=_params(*(("parallel", "parallel") if len(out_shape) == 1 else ("arbitrary", "arbitrary")),
                                "arbitrary"),
    )(*args)
    return outs[0] if len(outs) == 1 else tuple(outs)


def _proj_small_kernel(x_ref, wdt_ref, wlo_ref, o_ref):
    x = x_ref[...]
    nt = (((1,), (1,)), ((), ()))
    dt = lax.dot_general(x, wdt_ref[...], nt, preferred_element_type=F32)
    lo = lax.dot_general(x, wlo_ref[...], nt, preferred_element_type=F32)
    o_ref[:, 0:G_RANK_PAD] = lo[:, W_RANK + A_RANK:]
    o_ref[:, G_RANK_PAD:G_RANK_PAD + SSD_HEADS] = dt
    o_ref[:, G_RANK_PAD + SSD_HEADS:] = lo[:, :W_RANK + A_RANK]


def _proj_small(x, wt_dt, wt_lo, *, dt_row0, lo_row0, tm=512):
    m, kdim = x.shape
    win = lambda rows, row0: pl.BlockSpec((pl.Element(rows), pl.Element(kdim)), lambda i: (row0, 0))
    return pl.pallas_call(
        _proj_small_kernel,
        grid=(m // tm,),
        in_specs=[pl.BlockSpec((tm, kdim), lambda i: (i, 0)), win(SSD_HEADS, dt_row0), win(SMALL_LO_COLS, lo_row0)],
        out_specs=pl.BlockSpec((tm, SMALL_COLS), lambda i: (i, 0)),
        out_shape=jax.ShapeDtypeStruct((m, SMALL_COLS), F32),
        name="proj_small",
        compiler_params=_params("parallel"),
    )(x, wt_dt, wt_lo)


def _ple_embed_kernel(p_ref, w_ref, g_ref, o_ref):
    e = jnp.dot(p_ref[...], w_ref[...], preferred_element_type=F32)
    y = e * lax.rsqrt(jnp.mean(e * e, axis=-1, keepdims=True) + NORM_EPS)
    o_ref[...] = (y * g_ref[...]).astype(o_ref.dtype)


def _ple_embed(p, w, g, tm=256):
    m, kdim = p.shape
    n = w.shape[1]
    return pl.pallas_call(
        _ple_embed_kernel,
        grid=(m // tm,),
        in_specs=[pl.BlockSpec((tm, kdim), lambda i: (i, 0)), pl.BlockSpec((kdim, n), lambda i: (0, 0)),
                  pl.BlockSpec((1, n), lambda i: (0, 0))],
        out_specs=pl.BlockSpec((tm, n), lambda i: (i, 0)),
        out_shape=jax.ShapeDtypeStruct((m, n), F32),
        name="ple_embed",
        compiler_params=_params("parallel"),
    )(p, w, g.reshape(1, n))


def _ssd_kernel(xs_ref, bm_ref, cm_ref, dt_ref, z_ref, wxs_ref, wbm_ref, wcm_ref, bxs_ref, bbm_ref, bcm_ref,
                dtb_ref, alog_ref, d_ref, ng_ref, side_in_ref, o_ref, side_out_ref,
                st_ref, hxs_ref, hbm_ref, hcm_ref, axs_ref, abm_ref, acm_ref, *, n_chunks):
    tb = n_chunks * CHUNK
    hist = SUBLANES
    _side_convert(side_in_ref, side_out_ref)

    @pl.when(pl.program_id(2) == 0)
    def _():
        st_ref[...] = jnp.zeros_like(st_ref)
        for h_ref in (hxs_ref, hbm_ref, hcm_ref):
            h_ref[0:hist, :] = jnp.zeros((hist, h_ref.shape[1]), F32)

    for raw_ref, h_ref, w_ref, b_ref, act_ref in ((xs_ref, hxs_ref, wxs_ref, bxs_ref, axs_ref),
                                                  (bm_ref, hbm_ref, wbm_ref, bbm_ref, abm_ref),
                                                  (cm_ref, hcm_ref, wcm_ref, bcm_ref, acm_ref)):
        h_ref[hist:hist + tb, :] = raw_ref[0]
        hv = h_ref[...]
        acc = b_ref[...] + w_ref[SSD_CONV - 1:SSD_CONV, :] * hv[hist:]
        for j in range(SSD_CONV - 1):
            acc = acc + w_ref[j:j + 1, :] * pltpu.roll(hv, SSD_CONV - 1 - j, 0)[hist:]
        act_ref[...] = _silu(acc)
        h_ref[0:hist, :] = raw_ref[0, tb - hist:tb, :]

    gc = SSD_GROUP_COLS
    g = pl.program_id(1)
    expand = (lax.broadcasted_iota(jnp.int32, (SSD_HEADS, gc), 0)
              == (lax.broadcasted_iota(jnp.int32, (SSD_HEADS, gc), 1) >> LOG2_CHUNK) + g * SSD_HPG).astype(BF16)
    li = lax.broadcasted_iota(jnp.int32, (CHUNK, gc), 0)
    si = lax.broadcasted_iota(jnp.int32, (CHUNK, gc), 1) & (CHUNK - 1)
    eye_t = li == si
    causal_t = li >= si
    top = lax.broadcasted_iota(jnp.int32, (CHUNK, LANES), 1) < SSD_HEAD_DIM
    d_row = d_ref[0]
    ng_row = ng_ref[...]
    trow = lax.broadcasted_iota(jnp.int32, (tb, tb), 0)
    tcol = lax.broadcasted_iota(jnp.int32, (tb, tb), 1)
    tri_blk = (((trow >> LOG2_CHUNK) == (tcol >> LOG2_CHUNK)) & (trow >= tcol)).astype(BF16)
    dt_all = _softplus(dt_ref[0] + dtb_ref[...])
    acs_all = _cumsum_rows(tri_blk, dt_all * (-LOG2E * jnp.exp(alog_ref[...])))

    def expand_heads(q):
        q0, q1, q2 = _split3(q)
        de = lambda r: jnp.dot(r, expand, preferred_element_type=F32)
        return de(q0) + de(q1) + de(q2)

    st = st_ref[...]
    for c in range(n_chunks):
        sl = slice(c * CHUNK, (c + 1) * CHUNK)
        xs = axs_ref[sl, :]
        bm = abm_ref[sl, :]
        cm = acm_ref[sl, :]
        dt_exp = expand_heads(dt_all[sl, :])
        acs = expand_heads(acs_all[sl, :])
        rowpart = jnp.sum(jnp.where(eye_t, acs, 0.0), axis=0, keepdims=True)
        decay = jnp.exp2(jnp.where(causal_t, acs - rowpart, -jnp.inf))
        xdt = xs * dt_exp
        cb2 = _dot_nt(cm, jnp.concatenate([bm, bm], axis=0))
        last = acs[CHUNK - 1:CHUNK, :]
        y_off = _dot(cm, st) * jnp.exp2(acs)
        parts = []
        for j in range(gc // LANES):
            lanes = slice(j * LANES, (j + 1) * LANES)
            parts.append(_dot(cb2 * decay[:, lanes], _stack_heads(xdt[:, lanes], top)))
        y = jnp.concatenate(parts, axis=1) + y_off + d_row * xs
        st = st * jnp.exp2(last) + _dot_tn(bm, xdt * jnp.exp2(last - acs))
        y = y * _silu(z_ref[0, sl, :])
        y = y * lax.rsqrt(jnp.mean(y * y, axis=-1, keepdims=True) + SSD_NORM_EPS)
        o_ref[0, sl, :] = (y * ng_row).astype(o_ref.dtype)
    st_ref[...] = st


def _ssd(zx, small, conv_w, conv_b, dt_bias, a_log, d_skip, norm_g, side, *, tb=256):
    bsz, t, _ = zx.shape
    gc = SSD_GROUP_COLS
    xs_blk0 = SSD_D_INNER // gc
    bm_blk0 = 2 * SSD_D_INNER // SSD_STATE
    cm_blk0 = bm_blk0 + SSD_GROUPS
    wbm_blk0 = SSD_D_INNER // SSD_STATE
    wcm_blk0 = wbm_blk0 + SSD_GROUPS
    conv_b = conv_b.reshape(1, SSD_CONV_DIM)
    per_group = lambda v: jnp.repeat(v, SSD_HEAD_DIM).reshape(SSD_GROUPS, 1, gc)
    seq = lambda w, f: pl.BlockSpec((1, tb, w), lambda b, g, ti: (b, ti, f(g)))
    chan = lambda r, w, f: pl.BlockSpec((r, w), lambda b, g, ti: (0, f(g)))
    grp = pl.BlockSpec((1, 1, gc), lambda b, g, ti: (g, 0, 0))
    nt_steps = t // tb
    s_in, s_out, s_shape = _side_specs(side, lambda b, g, ti: (b * SSD_GROUPS + g) * nt_steps + ti,
                                       bsz * SSD_GROUPS * nt_steps)
    return pl.pallas_call(
        functools.partial(_ssd_kernel, n_chunks=tb // CHUNK),
        grid=(bsz, SSD_GROUPS, t // tb),
        in_specs=[seq(gc, lambda g: xs_blk0 + g), seq(SSD_STATE, lambda g: bm_blk0 + g),
                  seq(SSD_STATE, lambda g: cm_blk0 + g), seq(SSD_HEADS, lambda g: SMALL_DT_BLK),
                  seq(gc, lambda g: g),
                  chan(SSD_CONV, gc, lambda g: g), chan(SSD_CONV, SSD_STATE, lambda g: wbm_blk0 + g),
                  chan(SSD_CONV, SSD_STATE, lambda g: wcm_blk0 + g),
                  chan(1, gc, lambda g: g), chan(1, SSD_STATE, lambda g: wbm_blk0 + g),
                  chan(1, SSD_STATE, lambda g: wcm_blk0 + g),
                  chan(1, SSD_HEADS, lambda g: 0), chan(1, SSD_HEADS, lambda g: 0), grp,
                  chan(1, gc, lambda g: g), s_in],
        out_specs=[seq(gc, lambda g: g), s_out],
        out_shape=[jax.ShapeDtypeStruct((bsz, t, SSD_D_INNER), BF16), s_shape],
        scratch_shapes=[pltpu.VMEM((SSD_STATE, gc), F32),
                        pltpu.VMEM((SUBLANES + tb, gc), F32), pltpu.VMEM((SUBLANES + tb, SSD_STATE), F32),
                        pltpu.VMEM((SUBLANES + tb, SSD_STATE), F32),
                        pltpu.VMEM((tb, gc), F32), pltpu.VMEM((tb, SSD_STATE), F32), pltpu.VMEM((tb, SSD_STATE), F32)],
        name="ssd_scan",
        compiler_params=_params("arbitrary", "arbitrary", "arbitrary"),
    )(zx, zx, zx, small, zx, conv_w, conv_w, conv_w, conv_b, conv_b, conv_b, dt_bias.reshape(1, SSD_HEADS),
      a_log.reshape(1, SSD_HEADS), per_group(d_skip), norm_g.reshape(1, SSD_D_INNER), side[0])


def _wkv_kernel(r_ref, k_ref, v_ref, glo_ref, wlo_ref, alo_ref,
                mur_ref, muk_ref, muv_ref, mug_ref, muw_ref, mua_ref,
                w2_ref, a2_ref, g2_ref, w0_ref, a0_ref, kk_ref, ka_ref, rk_ref, lng_ref, lnb_ref, side_in_ref,
                o_ref, side_out_ref,
                s_ref, prev_ref, plo_ref, rr_ref, kr_ref, vv_ref, lw_ref, kn_ref, ag_ref, gg_ref,
                *, n_chunks, n_pairs):
    tb = n_chunks * CHUNK
    glo_w = glo_ref.shape[2]
    _side_convert(side_in_ref, side_out_ref)

    @pl.when(pl.program_id(2) == 0)
    def _():
        s_ref[...] = jnp.zeros_like(s_ref)
        prev_ref[...] = jnp.zeros_like(prev_ref)
        plo_ref[...] = jnp.zeros_like(plo_ref)

    first_row = lax.broadcasted_iota(jnp.int32, (tb, 1), 0) == 0

    def lerp(x, prev_row, mu):
        shifted = jnp.where(first_row, prev_row, pltpu.roll(x, 1, 0))
        return x + (shifted - x) * mu

    r_raw, k_raw, v_raw = r_ref[0], k_ref[0], v_ref[0]
    glo_raw, wlo_raw, alo_raw = glo_ref[0], wlo_ref[0], alo_ref[0]
    r = lerp(r_raw, prev_ref[0:1, :], mur_ref[...])
    k = lerp(k_raw, prev_ref[1:2, :], muk_ref[...])
    v = lerp(v_raw, prev_ref[2:3, :], muv_ref[...])
    g_lo = lerp(glo_raw, plo_ref[0:1, 0:glo_w], mug_ref[...])
    w_lo = lerp(wlo_raw, plo_ref[0:1, glo_w:glo_w + W_RANK], muw_ref[...])
    a_lo = lerp(alo_raw, plo_ref[0:1, glo_w + W_RANK:], mua_ref[...])
    prev_ref[0:1, :] = r_raw[tb - 1:tb, :]
    prev_ref[1:2, :] = k_raw[tb - 1:tb, :]
    prev_ref[2:3, :] = v_raw[tb - 1:tb, :]
    plo_ref[0:1, 0:glo_w] = glo_raw[tb - 1:tb, :]
    plo_ref[0:1, glo_w:glo_w + W_RANK] = wlo_raw[tb - 1:tb, :]
    plo_ref[0:1, glo_w + W_RANK:] = alo_raw[tb - 1:tb, :]

    lw = (-DECAY_SCALE * LOG2E) * _sigmoid(w0_ref[...] + jnp.dot(jnp.tanh(w_lo).astype(BF16), w2_ref[...],
                                                                 preferred_element_type=F32))
    a_gate = _sigmoid(a0_ref[...] + jnp.dot(a_lo.astype(BF16), a2_ref[...], preferred_element_type=F32))
    gg_ref[...] = jnp.dot(_sigmoid(g_lo).astype(BF16), g2_ref[...], preferred_element_type=F32)
    top_blk = lax.broadcasted_iota(jnp.int32, (tb, LANES), 1) < RWKV_HEAD_DIM
    kk = k * kk_ref[...]
    for j in range(n_pairs):
        lanes = slice(j * LANES, (j + 1) * LANES)
        kj = kk[:, lanes]
        kn_ref[:, lanes] = kj * lax.rsqrt(jnp.maximum(_head_sum(kj * kj, top_blk), L2_EPS * L2_EPS))
    rr_ref[...] = r
    kr_ref[...] = k * (1.0 + (a_gate - 1.0) * ka_ref[...])
    vv_ref[...] = v
    lw_ref[...] = lw
    ag_ref[...] = a_gate

    row = lax.broadcasted_iota(jnp.int32, (LANES, LANES), 0)
    col = lax.broadcasted_iota(jnp.int32, (LANES, LANES), 1)
    same = (row >> LOG2_CHUNK) == (col >> LOG2_CHUNK)
    trow = lax.broadcasted_iota(jnp.int32, (tb, tb), 0)
    tcol = lax.broadcasted_iota(jnp.int32, (tb, tb), 1)
    tri_blk = ((trow >> LOG2_CHUNK) == (tcol >> LOG2_CHUNK)) & (trow >= tcol)
    top = lax.broadcasted_iota(jnp.int32, (CHUNK, LANES), 1) < RWKV_HEAD_DIM
    inv_n = 1.0 / RWKV_HEAD_DIM
    chains = [(j, c) for j in range(n_pairs) for c in range(n_chunks)]

    def piece(ref, j, c):
        return ref[c * CHUNK:(c + 1) * CHUNK, j * LANES:(j + 1) * LANES]

    cum_all = _cumsum_rows(tri_blk.astype(BF16), lw)

    lane_t = lax.broadcasted_iota(jnp.int32, (CHUNK, LANES), 1) & (CHUNK - 1)
    row_t = lax.broadcasted_iota(jnp.int32, (CHUNK, LANES), 0)
    strict_c = row_t > lane_t
    incl_c = row_t >= lane_t
    eye_c = (row_t == lane_t).astype(F32)
    strict2 = jnp.concatenate([strict_c, incl_c], axis=0)

    def bd(x):
        return _stack_heads(x.astype(BF16), top)

    nb = range(len(chains))
    a_c, r_c, b_bd, k_bd, v_c, bh_c, kh_c, wtot = [], [], [], [], [], [], [], []
    for j, c in chains:
        cum = cum_all[c * CHUNK:(c + 1) * CHUNK, j * LANES:(j + 1) * LANES]
        lwc = piece(lw_ref, j, c)
        kc = piece(kr_ref, j, c)
        knc = piece(kn_ref, j, c)
        b = knc * piece(ag_ref, j, c)
        tot = cum[CHUNK - 1:CHUNK, :]
        e_neg = jnp.exp2(-cum)
        e_end = jnp.exp2(tot - cum)
        a_c.append((-knc * jnp.exp2(cum - lwc)).astype(BF16))
        r_c.append(piece(rr_ref, j, c) * jnp.exp2(cum))
        b_bd.append(bd(b * e_neg))
        k_bd.append(bd(kc * e_neg))
        v_c.append(piece(vv_ref, j, c).astype(BF16))
        bh_c.append((b * e_end).astype(BF16))
        kh_c.append((kc * e_end).astype(BF16))
        wtot.append(jnp.exp2(tot))
    ar = [jnp.concatenate([a_c[i], r_c[i].astype(BF16)], axis=0) for i in nb]
    sb = [jnp.where(strict2, _dot_nt(ar[i], b_bd[i]), 0.0) for i in nb]
    sk = [jnp.where(strict2, _dot_nt(ar[i], k_bd[i]), 0.0) for i in nb]
    n = [x[:CHUNK] for x in sb]
    rb = [x[CHUNK:].astype(BF16) for x in sb]
    v_bd = [bd(x) for x in v_c]
    tinv = [eye_c + x for x in n]
    p_bd = [bd(x) for x in n]
    npow = [_dot(n[i], p_bd[i]) for i in nb]
    for it in range(1, 6):
        p_bd = [bd(x) for x in npow]
        if it < 5:
            both = [_dot(jnp.concatenate([npow[i], tinv[i]], axis=0), p_bd[i]) for i in nb]
            npow = [x[:CHUNK] for x in both]
            tinv = [tinv[i] + both[i][CHUNK:] for i in nb]
        else:
            tinv = [tinv[i] + _dot(tinv[i], p_bd[i]) for i in nb]
    akv = [_dot(sk[i][:CHUNK], v_bd[i]) for i in nb]
    pq = [_dot(tinv[i], jnp.concatenate([bd(a_c[i]), bd(akv[i])], axis=1)) for i in nb]
    pq_bd = [jnp.concatenate([bd(x[:, :LANES]), bd(x[:, LANES:])], axis=1) for x in pq]
    rpq = [_dot(rb[i], pq_bd[i]) for i in nb]
    r_eff = [(r_c[i] + rpq[i][:, :LANES]).astype(BF16) for i in nb]
    y0 = [rpq[i][:, LANES:] + _dot(sk[i][CHUNK:], v_bd[i]) for i in nb]
    g_eff = [jnp.where(same, _dot_tn(pq[i][:, :LANES], bh_c[i]), 0.0).astype(BF16) for i in nb]
    s1 = [jnp.where(same, _dot_tn(jnp.concatenate([pq[i][:, LANES:].astype(BF16), v_c[i]], axis=0),
                                  jnp.concatenate([bh_c[i], kh_c[i]], axis=0)), 0.0) for i in nb]

    for j in range(n_pairs):
        lanes = slice(j * LANES, (j + 1) * LANES)
        rk_row = rk_ref[:, lanes]
        lng_row = lng_ref[:, lanes]
        lnb_row = lnb_ref[:, lanes]
        s = s_ref[j]
        for c in range(n_chunks):
            i = j * n_chunks + c
            y = _dot_nt(r_eff[i], s) + y0[i]
            s = s * wtot[i] + _dot(s, g_eff[i]) + s1[i]
            mean = _head_sum(y, top) * inv_n
            yc = y - mean
            var = _head_sum(yc * yc, top) * inv_n
            yn = yc * lax.rsqrt(var + RWKV_GN_EPS) * lng_row + lnb_row
            bonus = _head_sum(piece(rr_ref, j, c) * piece(kr_ref, j, c) * rk_row, top) * piece(vv_ref, j, c)
            o_ref[0, c * CHUNK:(c + 1) * CHUNK, lanes] = ((yn + bonus) * piece(gg_ref, j, c)).astype(o_ref.dtype)
        s_ref[j] = s


def _wkv(rkv, small, mu, w2, a2, g2p, w0, a0, k_k, k_a, r_k, ln_g, ln_b, side, *, tb=256, pairs=8):
    bsz, t, d3 = rkv.shape
    d = d3 // 3
    wl = pairs * LANES
    nh = d // wl
    mu_rkv = mu[:3 * d].reshape(1, 3 * d)
    mu_w = mu[3 * d:3 * d + W_RANK].reshape(1, W_RANK)
    mu_a = mu[3 * d + W_RANK:3 * d + W_RANK + A_RANK].reshape(1, A_RANK)
    mu_g = jnp.pad(mu[3 * d + W_RANK + A_RANK:], (0, G_RANK_PAD - G_RANK)).reshape(1, G_RANK_PAD)
    seq = lambda w, f: pl.BlockSpec((1, tb, w), lambda b, h, ti: (b, ti, f(h)))
    chan = lambda r, w, f: pl.BlockSpec((r, w), lambda b, h, ti: (0, f(h)))
    vec = lambda q: q.reshape(1, d)
    blk = pltpu.VMEM((tb, wl), F32)
    nt_steps = t // tb
    s_in, s_out, s_shape = _side_specs(side, lambda b, h, ti: (b * nh + h) * nt_steps + ti, bsz * nh * nt_steps)
    return pl.pallas_call(
        functools.partial(_wkv_kernel, n_chunks=tb // CHUNK, n_pairs=pairs),
        grid=(bsz, nh, t // tb),
        in_specs=[seq(wl, lambda h: h), seq(wl, lambda h: nh + h), seq(wl, lambda h: 2 * nh + h),
                  seq(G_RANK_PAD, lambda h: 0), seq(W_RANK, lambda h: SMALL_WLO_BLK),
                  seq(A_RANK, lambda h: SMALL_ALO_BLK),
                  chan(1, wl, lambda h: h), chan(1, wl, lambda h: nh + h), chan(1, wl, lambda h: 2 * nh + h),
                  chan(1, G_RANK_PAD, lambda h: 0), chan(1, W_RANK, lambda h: 0), chan(1, A_RANK, lambda h: 0),
                  chan(W_RANK, wl, lambda h: h), chan(A_RANK, wl, lambda h: h), chan(G_RANK_PAD, wl, lambda h: h)]
                 + [chan(1, wl, lambda h: h)] * 7 + [s_in],
        out_specs=[seq(wl, lambda h: h), s_out],
        out_shape=[jax.ShapeDtypeStruct((bsz, t, d), BF16), s_shape],
        scratch_shapes=[pltpu.VMEM((pairs, LANES, LANES), F32), pltpu.VMEM((SUBLANES, wl), F32),
                        pltpu.VMEM((SUBLANES, G_RANK_PAD + W_RANK + A_RANK), F32)] + [blk] * 7,
        name="wkv7_scan",
        compiler_params=_params("arbitrary", "arbitrary", "arbitrary"),
    )(rkv, rkv, rkv, small, small, small, mu_rkv, mu_rkv, mu_rkv, mu_g, mu_w, mu_a,
      w2, a2, g2p, vec(w0), vec(a0), vec(k_k), vec(k_a), vec(r_k), vec(ln_g), vec(ln_b), side[0])


def kernel(x, p, norm_mix_g, w_in, ssd_conv_w, ssd_conv_b, ssd_dt_bias, ssd_a_log, ssd_d, ssd_norm_g, rwkv_mu, rwkv_w0, rwkv_w2, rwkv_a0, rwkv_a2, rwkv_g2, rwkv_k_k, rwkv_k_a, rwkv_r_k, rwkv_ln_g, rwkv_ln_b, w_branch_ssd, w_branch_rwkv, w_out, norm_ffn_g, w_ff1, w_ff2, norm_ple_g, w_ple_gate, w_ple_proj, ple_post_g, final_norm_g):
    bsz, t, d = x.shape
    m = bsz * t
    depth = w_in.shape[0]
    xf = x.reshape(m, d)
    for i in range(depth):
        c_dt = SSD_D_INNER + SSD_CONV_DIM
        c_r = c_dt + SSD_HEADS
        c_wlo = c_r + 3 * d
        c_glo = c_wlo + W_RANK + A_RANK
        c_gate = c_glo + G_RANK
        w_in_t = jnp.swapaxes(w_in[i], 0, 1)
        w_head = w_in_t[:c_r].astype(BF16)
        g2p = jnp.pad(rwkv_g2[i], ((0, G_RANK_PAD - G_RANK), (0, 0))).astype(BF16)
        whole = lambda w: (w, 0, w.shape[0])

        h = _rmsnorm(xf, norm_mix_g[i], BF16)
        proj = functools.partial(_mm, h, w_nt=True, out_dtype=F32, **TILE_WIDE)
        zx, w_rkv = proj(w_head, name="proj_zx", n=c_dt, side=(w_in_t, c_r, 3 * d))
        rkv, w_tail = proj(w_rkv, name="proj_rkv", n=3 * d, side=(w_in_t, c_wlo, w_in.shape[2] - c_wlo))
        gates, w_bssd_b = proj(w_tail, name="proj_gates", w_row0=c_gate - c_wlo, n=2 * d,
                               side=whole(w_branch_ssd[i]) + (TILE_K2D["tn"],))
        small = _proj_small(h, w_head, w_tail, dt_row0=c_dt, lo_row0=0).reshape(bsz, t, -1)
        zx = zx.reshape(bsz, t, -1)
        rkv = rkv.reshape(bsz, t, -1)

        u_ssd, w_ff1_b = _ssd(zx, small, ssd_conv_w[i], ssd_conv_b[i], ssd_dt_bias[i], ssd_a_log[i], ssd_d[i],
                              ssd_norm_g[i], whole(w_ff1[i]))
        u_rwkv, w_ff2_b = _wkv(rkv, small, rwkv_mu[i], rwkv_w2[i].astype(BF16), rwkv_a2[i].astype(BF16), g2p,
                               rwkv_w0[i], rwkv_a0[i], rwkv_k_k[i], rwkv_k_a[i], rwkv_r_k[i], rwkv_ln_g[i],
                               rwkv_ln_b[i], whole(w_ff2[i]) + (TILE_K4D["tn"],))

        part, w_brwkv_b = _mm(u_ssd.reshape(m, SSD_D_INNER), w_bssd_b, name="branch_ssd", out_dtype=F32, **TILE_K2D,
                              mul=gates, mul_act="sigmoid", mul_col0=0, side=whole(w_branch_rwkv[i]))
        merged, w_out_b = _mm(u_rwkv.reshape(m, d), w_brwkv_b, name="branch_rwkv", out_dtype=BF16, **TILE_EPI,
                              mul=gates, mul_act="sigmoid", mul_col0=d, res=part, side=whole(w_out[i]))
        xf, xg, ss = _mm(merged, w_out_b, name="out_proj", out_dtype=F32, **TILE_EPI, res=xf,
                         next_norm_g=norm_ffn_g[i])

        ff, w_ple_b = _mm(xg, w_ff1_b, name="ffn_up", out_dtype=BF16, **TILE_WIDE, act="relu2", row_ss=ss,
                          side=whole(w_ple_gate[i]))
        xf, xg, ss = _mm(ff, w_ff2_b, name="ffn_down", out_dtype=F32, **TILE_K4D, res=xf,
                         next_norm_g=norm_ple_g[i])

        e = _ple_embed(p[i].reshape(m, PLE_DIM).astype(BF16), w_ple_proj[i].astype(BF16), ple_post_g[i])
        xf = _mm(xg, w_ple_b, name="ple_gate", out_dtype=F32, **TILE_EPI, act="sigmoid", mul=e, res=xf,
                 row_ss=ss)
    return _rmsnorm(xf, final_norm_g, F32).reshape(bsz, t, d)
```

```python
import functools
import math

import jax
import jax.numpy as jnp
from jax import lax
from jax.experimental import pallas as pl
from jax.experimental.pallas import tpu as pltpu

F32 = jnp.float32
BF16 = jnp.bfloat16

CHUNK = 64
LOG2_CHUNK = 6
SSD_D_INNER = 8192
SSD_HEAD_DIM = 64
SSD_HEADS = 128
SSD_GROUPS = 8
SSD_HPG = 16
SSD_STATE = 128
SSD_CONV = 4
SSD_GROUP_COLS = SSD_D_INNER // SSD_GROUPS
SSD_CONV_DIM = SSD_D_INNER + 2 * SSD_GROUPS * SSD_STATE
SSD_NORM_EPS = 1e-5
RWKV_HEAD_DIM = 64
W_RANK = 128
A_RANK = 128
G_RANK = 480
G_RANK_PAD = 512
DECAY_SCALE = math.exp(-0.5)
RWKV_GN_EPS = RWKV_HEAD_DIM * 1e-5
L2_EPS = 1e-12
LOG2E = math.log2(math.e)
NORM_EPS = 1e-6
PLE_DIM = 256

SMALL_COLS = G_RANK_PAD + SSD_HEADS + W_RANK + A_RANK
SMALL_LO_COLS = W_RANK + A_RANK + G_RANK_PAD

LANES = 128
SUBLANES = 8
BF16_ROWS = 16
VMEM_BYTES = 64 * 1024 * 1024
VMEM_LIMIT_BYTES = VMEM_BYTES - 8 * 1024 * 1024

SMALL_DT_BLK = G_RANK_PAD // LANES
SMALL_WLO_BLK = SMALL_DT_BLK + 1
SMALL_ALO_BLK = SMALL_DT_BLK + 2

TILE_WIDE = dict(tm=1024, tn=1024)
TILE_EPI = dict(tm=1024, tn=512)
TILE_K2D = dict(tm=1024, tn=256)
TILE_K4D = dict(tm=512, tn=256)


def _params(*sem):
    return pltpu.CompilerParams(dimension_semantics=sem, vmem_limit_bytes=VMEM_LIMIT_BYTES)


def _split3(x):
    hi = x.astype(BF16)
    r1 = x - hi.astype(F32)
    mid = r1.astype(BF16)
    lo = (r1 - mid.astype(F32)).astype(BF16)
    return hi, mid, lo


def _dot(a, b, dims=(((1,), (0,)), ((), ()))):
    return lax.dot_general(a.astype(BF16), b.astype(BF16), dims, preferred_element_type=F32)


def _dot_nt(a, b):
    return _dot(a, b, (((1,), (1,)), ((), ())))


def _dot_tn(a, b):
    return _dot(a, b, (((0,), (0,)), ((), ())))


def _cumsum_rows(tri_bf16, x):
    x0, x1, x2 = _split3(x)
    d = lambda q: jnp.dot(tri_bf16, q, preferred_element_type=F32)
    return d(x0) + d(x1) + d(x2)


def _sigmoid(x):
    return 0.5 * jnp.tanh(0.5 * x) + 0.5


def _silu(x):
    h = 0.5 * x
    return h + h * jnp.tanh(h)


def _softplus(x):
    return jnp.maximum(x, 0.0) + jnp.log1p(jnp.exp(-jnp.abs(x)))


def _stack_heads(x, top):
    return jnp.concatenate([jnp.where(top, x, 0.0), jnp.where(top, 0.0, x)], axis=0)


def _head_sum(x, top):
    s0 = jnp.sum(jnp.where(top, x, 0.0), axis=-1, keepdims=True)
    s1 = jnp.sum(jnp.where(top, 0.0, x), axis=-1, keepdims=True)
    return jnp.where(top, s0, s1)


def _side_specs(side, step_of, n_steps):
    src, row0, nrows = side
    cols = src.shape[1]
    assert src.ndim == 2 and nrows % BF16_ROWS == 0 and row0 % SUBLANES == 0
    rs = next(r for r in range(BF16_ROWS, nrows + 1, BF16_ROWS) if nrows % r == 0 and nrows // r <= n_steps)
    nslabs = nrows // rs
    slab = lambda *g: jnp.minimum(step_of(*g), nslabs - 1)
    in_spec = pl.BlockSpec((pl.Element(rs), pl.Element(cols)),
                           lambda *g: (pl.multiple_of(row0 + slab(*g) * rs, SUBLANES), 0))
    out_spec = pl.BlockSpec((rs, cols), lambda *g: (slab(*g), 0))
    return in_spec, out_spec, jax.ShapeDtypeStruct((nrows, cols), BF16)


def _rmsnorm_kernel(x_ref, g_ref, o_ref, *, eps):
    x = x_ref[...]
    y = x * lax.rsqrt(jnp.mean(x * x, axis=-1, keepdims=True) + eps)
    o_ref[...] = (y * g_ref[...]).astype(o_ref.dtype)


def _rmsnorm(x, g, out_dtype, tm=256):
    m, d = x.shape
    return pl.pallas_call(
        functools.partial(_rmsnorm_kernel, eps=NORM_EPS),
        grid=(m // tm,),
        in_specs=[pl.BlockSpec((tm, d), lambda i: (i, 0)), pl.BlockSpec((1, d), lambda i: (0, 0))],
        out_specs=pl.BlockSpec((tm, d), lambda i: (i, 0)),
        out_shape=jax.ShapeDtypeStruct((m, d), out_dtype),
        name="rmsnorm",
        compiler_params=_params("parallel"),
    )(x, g.reshape(1, d))


def _act(x, kind):
    if kind is None:
        return x
    if kind == "relu2":
        r = jnp.maximum(x, 0.0)
        return r * r
    if kind == "sigmoid":
        return _sigmoid(x)
    raise ValueError(kind)


def _mm_kernel(*refs, nk, w_nt, act, has_mul, mul_act, has_res, has_scale, has_norm, has_side, norm_dim):
    refs = list(refs)
    x_ref, w_ref = refs[0], refs[1]
    pos = 2

    def take(flag):
        nonlocal pos
        if not flag:
            return None
        pos += 1
        return refs[pos - 1]

    mul_ref, res_ref, ss_in_ref, g_ref, side_in_ref = (take(f) for f in (has_mul, has_res, has_scale, has_norm, has_side))
    o_ref = take(True)
    xg_ref, ss_out_ref, side_out_ref = take(has_norm), take(has_norm), take(has_side)
    if has_side:
        side_out_ref[...] = side_in_ref[...].astype(BF16)
    acc_ref = None if nk == 1 else (o_ref if o_ref.dtype == F32 else refs[pos])

    if w_nt:
        part = lax.dot_general(x_ref[...], w_ref[...], (((1,), (1,)), ((), ())), preferred_element_type=F32)
    else:
        part = jnp.dot(x_ref[...], w_ref[...], preferred_element_type=F32)

    def finish(acc):
        if has_scale:
            acc = acc * lax.rsqrt(ss_in_ref[:, 0:1] * (1.0 / norm_dim) + NORM_EPS)
        out = _act(acc, act)
        if has_mul:
            out = out * _act(mul_ref[...].astype(F32), mul_act)
        if has_res:
            out = res_ref[...].astype(F32) + out
        o_ref[...] = out.astype(o_ref.dtype)
        if has_norm:
            xg_ref[...] = (out * g_ref[...]).astype(BF16)
            row_ss = jnp.broadcast_to(jnp.sum(out * out, axis=1, keepdims=True), ss_out_ref.shape)
            first = pl.program_id(1) == 0

            @pl.when(first)
            def _():
                ss_out_ref[...] = row_ss

            @pl.when(jnp.logical_not(first))
            def _():
                ss_out_ref[...] += row_ss

    if nk == 1:
        finish(part)
    else:
        k = pl.program_id(2)

        @pl.when(k == 0)
        def _():
            acc_ref[...] = part

        @pl.when((k > 0) & (k < nk - 1))
        def _():
            acc_ref[...] += part

        @pl.when(k == nk - 1)
        def _():
            finish(acc_ref[...] + part)


def _mm(x, w, *, name, out_dtype, tm, tn, tk=None, w_nt=False, w_row0=0, n=None, act=None, mul=None, mul_act=None,
        mul_col0=0, res=None, res_col0=0, side=None, row_ss=None, next_norm_g=None):
    m, kdim = x.shape
    n = w.shape[1] if n is None else n
    tk = kdim if tk is None else tk
    tm, tn = min(tm, m), min(tn, n)
    nk = kdim // tk
    assert m % tm == 0 and n % tn == 0 and kdim % tk == 0 and mul_col0 % tn == 0 and res_col0 % tn == 0
    if w_nt:
        assert w.shape[1] == kdim and w_row0 % BF16_ROWS == 0 and tn % BF16_ROWS == 0 and w.dtype == BF16
        w_spec = pl.BlockSpec((pl.Element(tn), pl.Element(tk)),
                              lambda i, j, k: (pl.multiple_of(w_row0 + j * tn, BF16_ROWS), k * tk))
    else:
        assert w.shape[0] == kdim and w_row0 == 0
        w_spec = pl.BlockSpec((tk, tn), lambda i, j, k: (k, j))
    in_specs = [pl.BlockSpec((tm, tk), lambda i, j, k: (i, k)), w_spec]
    args = [x, w]
    if mul is not None:
        off = mul_col0 // tn
        in_specs.append(pl.BlockSpec((tm, tn), lambda i, j, k, off=off: (i, j + off)))
        args.append(mul)
    if res is not None:
        off = res_col0 // tn
        in_specs.append(pl.BlockSpec((tm, tn), lambda i, j, k, off=off: (i, j + off)))
        args.append(res)
    if row_ss is not None:
        in_specs.append(pl.BlockSpec((tm, LANES), lambda i, j, k: (i, 0)))
        args.append(row_ss)
    grid = (m // tm, n // tn, nk)
    out_specs = [pl.BlockSpec((tm, tn), lambda i, j, k: (i, j))]
    out_shape = [jax.ShapeDtypeStruct((m, n), out_dtype)]
    if next_norm_g is not None:
        assert nk == 1 and out_dtype == F32
        in_specs.append(pl.BlockSpec((1, tn), lambda i, j, k: (0, j)))
        args.append(next_norm_g.reshape(1, n))
        out_specs += [pl.BlockSpec((tm, tn), lambda i, j, k: (i, j)), pl.BlockSpec((tm, LANES), lambda i, j, k: (i, 0))]
        out_shape += [jax.ShapeDtypeStruct((m, n), BF16), jax.ShapeDtypeStruct((m, LANES), F32)]
    if side is not None:
        s_in, s_out, s_shape = _side_specs(side, lambda i, j, k: (i * grid[1] + j) * nk + k, grid[0] * grid[1] * nk)
        in_specs.append(s_in)
        args.append(side[0])
        out_specs.append(s_out)
        out_shape.append(s_shape)
    outs = pl.pallas_call(
        functools.partial(_mm_kernel, nk=nk, w_nt=w_nt, act=act, has_mul=mul is not None, mul_act=mul_act,
                          has_res=res is not None, has_scale=row_ss is not None, has_norm=next_norm_g is not None,
                          has_side=side is not None, norm_dim=kdim),
        grid=grid,
        in_specs=in_specs,
        out_specs=out_specs,
        out_shape=out_shape,
        scratch_shapes=[pltpu.VMEM((tm, tn), F32)] if nk > 1 and out_dtype != F32 else [],
        name=name,
        compiler_params=_params(*(("parallel", "parallel") if len(out_shape) == 1 else ("arbitrary", "arbitrary")),
                                "arbitrary"),
    )(*args)
    return outs[0] if len(outs) == 1 else tuple(outs)


def _proj_small_kernel(x_ref, wdt_ref, wlo_ref, o_ref):
    x = x_ref[...]
    nt = (((1,), (1,)), ((), ()))
    dt = lax.dot_general(x, wdt_ref[...], nt, preferred_element_type=F32)
    lo = lax.dot_general(x, wlo_ref[...], nt, preferred_element_type=F32)
    o_ref[:, 0:G_RANK_PAD] = lo[:, W_RANK + A_RANK:]
    o_ref[:, G_RANK_PAD:G_RANK_PAD + SSD_HEADS] = dt
    o_ref[:, G_RANK_PAD + SSD_HEADS:] = lo[:, :W_RANK + A_RANK]


def _proj_small(x, wt_dt, wt_lo, *, dt_row0, lo_row0, tm=512):
    m, kdim = x.shape
    win = lambda rows, row0: pl.BlockSpec((pl.Element(rows), pl.Element(kdim)), lambda i: (row0, 0))
    return pl.pallas_call(
        _proj_small_kernel,
        grid=(m // tm,),
        in_specs=[pl.BlockSpec((tm, kdim), lambda i: (i, 0)), win(SSD_HEADS, dt_row0), win(SMALL_LO_COLS, lo_row0)],
        out_specs=pl.BlockSpec((tm, SMALL_COLS), lambda i: (i, 0)),
        out_shape=jax.ShapeDtypeStruct((m, SMALL_COLS), F32),
        name="proj_small",
        compiler_params=_params("parallel"),
    )(x, wt_dt, wt_lo)


def _ple_embed_kernel(p_ref, w_ref, g_ref, o_ref):
    e = jnp.dot(p_ref[...], w_ref[...], preferred_element_type=F32)
    y = e * lax.rsqrt(jnp.mean(e * e, axis=-1, keepdims=True) + NORM_EPS)
    o_ref[...] = (y * g_ref[...]).astype(o_ref.dtype)


def _ple_embed(p, w, g, tm=256):
    m, kdim = p.shape
    n = w.shape[1]
    return pl.pallas_call(
        _ple_embed_kernel,
        grid=(m // tm,),
        in_specs=[pl.BlockSpec((tm, kdim), lambda i: (i, 0)), pl.BlockSpec((kdim, n), lambda i: (0, 0)),
                  pl.BlockSpec((1, n), lambda i: (0, 0))],
        out_specs=pl.BlockSpec((tm, n), lambda i: (i, 0)),
        out_shape=jax.ShapeDtypeStruct((m, n), F32),
        name="ple_embed",
        compiler_params=_params("parallel"),
    )(p, w, g.reshape(1, n))


def _ssd_kernel(xs_ref, bm_ref, cm_ref, dt_ref, z_ref, wxs_ref, wbm_ref, wcm_ref, bxs_ref, bbm_ref, bcm_ref,
                dtb_ref, alog_ref, d_ref, ng_ref, side_in_ref, o_ref, side_out_ref,
                st_ref, hxs_ref, hbm_ref, hcm_ref, axs_ref, abm_ref, acm_ref, *, n_chunks):
    tb = n_chunks * CHUNK
    hist = SUBLANES
    side_out_ref[...] = side_in_ref[...].astype(BF16)

    @pl.when(pl.program_id(2) == 0)
    def _():
        st_ref[...] = jnp.zeros_like(st_ref)
        for h_ref in (hxs_ref, hbm_ref, hcm_ref):
            h_ref[0:hist, :] = jnp.zeros((hist, h_ref.shape[1]), F32)

    for raw_ref, h_ref, w_ref, b_ref, act_ref in ((xs_ref, hxs_ref, wxs_ref, bxs_ref, axs_ref),
                                                  (bm_ref, hbm_ref, wbm_ref, bbm_ref, abm_ref),
                                                  (cm_ref, hcm_ref, wcm_ref, bcm_ref, acm_ref)):
        h_ref[hist:hist + tb, :] = raw_ref[0]
        hv = h_ref[...]
        acc = b_ref[...] + w_ref[SSD_CONV - 1:SSD_CONV, :] * hv[hist:]
        for j in range(SSD_CONV - 1):
            acc = acc + w_ref[j:j + 1, :] * pltpu.roll(hv, SSD_CONV - 1 - j, 0)[hist:]
        act_ref[...] = _silu(acc)
        h_ref[0:hist, :] = raw_ref[0, tb - hist:tb, :]

    gc = SSD_GROUP_COLS
    g = pl.program_id(1)
    expand = (lax.broadcasted_iota(jnp.int32, (SSD_HEADS, gc), 0)
              == (lax.broadcasted_iota(jnp.int32, (SSD_HEADS, gc), 1) >> LOG2_CHUNK) + g * SSD_HPG).astype(BF16)
    li = lax.broadcasted_iota(jnp.int32, (CHUNK, gc), 0)
    si = lax.broadcasted_iota(jnp.int32, (CHUNK, gc), 1) & (CHUNK - 1)
    eye_t = li == si
    causal_t = li >= si
    top = lax.broadcasted_iota(jnp.int32, (CHUNK, LANES), 1) < SSD_HEAD_DIM
    d_row = d_ref[0]
    ng_row = ng_ref[...]
    trow = lax.broadcasted_iota(jnp.int32, (tb, tb), 0)
    tcol = lax.broadcasted_iota(jnp.int32, (tb, tb), 1)
    tri_blk = (((trow >> LOG2_CHUNK) == (tcol >> LOG2_CHUNK)) & (trow >= tcol)).astype(BF16)
    dt_all = _softplus(dt_ref[0] + dtb_ref[...])
    acs_all = _cumsum_rows(tri_blk, dt_all * (-LOG2E * jnp.exp(alog_ref[...])))

    def expand_heads(q):
        q0, q1, q2 = _split3(q)
        de = lambda r: jnp.dot(r, expand, preferred_element_type=F32)
        return de(q0) + de(q1) + de(q2)

    st = st_ref[...]
    for c in range(n_chunks):
        sl = slice(c * CHUNK, (c + 1) * CHUNK)
        xs = axs_ref[sl, :]
        bm = abm_ref[sl, :]
        cm = acm_ref[sl, :]
        dt_exp = expand_heads(dt_all[sl, :])
        acs = expand_heads(acs_all[sl, :])
        rowpart = jnp.sum(jnp.where(eye_t, acs, 0.0), axis=0, keepdims=True)
        decay = jnp.exp2(jnp.where(causal_t, acs - rowpart, -jnp.inf))
        xdt = xs * dt_exp
        cb2 = _dot_nt(cm, jnp.concatenate([bm, bm], axis=0))
        last = acs[CHUNK - 1:CHUNK, :]
        y_off = _dot(cm, st) * jnp.exp2(acs)
        parts = []
        for j in range(gc // LANES):
            lanes = slice(j * LANES, (j + 1) * LANES)
            parts.append(_dot(cb2 * decay[:, lanes], _stack_heads(xdt[:, lanes], top)))
        y = jnp.concatenate(parts, axis=1) + y_off + d_row * xs
        st = st * jnp.exp2(last) + _dot_tn(bm, xdt * jnp.exp2(last - acs))
        y = y * _silu(z_ref[0, sl, :])
        y = y * lax.rsqrt(jnp.mean(y * y, axis=-1, keepdims=True) + SSD_NORM_EPS)
        o_ref[0, sl, :] = (y * ng_row).astype(o_ref.dtype)
    st_ref[...] = st


def _ssd(zx, small, conv_w, conv_b, dt_bias, a_log, d_skip, norm_g, side, *, tb=512):
    bsz, t, _ = zx.shape
    gc = SSD_GROUP_COLS
    xs_blk0 = SSD_D_INNER // gc
    bm_blk0 = 2 * SSD_D_INNER // SSD_STATE
    cm_blk0 = bm_blk0 + SSD_GROUPS
    wbm_blk0 = SSD_D_INNER // SSD_STATE
    wcm_blk0 = wbm_blk0 + SSD_GROUPS
    conv_b = conv_b.reshape(1, SSD_CONV_DIM)
    per_group = lambda v: jnp.repeat(v, SSD_HEAD_DIM).reshape(SSD_GROUPS, 1, gc)
    seq = lambda w, f: pl.BlockSpec((1, tb, w), lambda b, g, ti: (b, ti, f(g)))
    chan = lambda r, w, f: pl.BlockSpec((r, w), lambda b, g, ti: (0, f(g)))
    grp = pl.BlockSpec((1, 1, gc), lambda b, g, ti: (g, 0, 0))
    nt_steps = t // tb
    s_in, s_out, s_shape = _side_specs(side, lambda b, g, ti: (b * SSD_GROUPS + g) * nt_steps + ti,
                                       bsz * SSD_GROUPS * nt_steps)
    return pl.pallas_call(
        functools.partial(_ssd_kernel, n_chunks=tb // CHUNK),
        grid=(bsz, SSD_GROUPS, t // tb),
        in_specs=[seq(gc, lambda g: xs_blk0 + g), seq(SSD_STATE, lambda g: bm_blk0 + g),
                  seq(SSD_STATE, lambda g: cm_blk0 + g), seq(SSD_HEADS, lambda g: SMALL_DT_BLK),
                  seq(gc, lambda g: g),
                  chan(SSD_CONV, gc, lambda g: g), chan(SSD_CONV, SSD_STATE, lambda g: wbm_blk0 + g),
                  chan(SSD_CONV, SSD_STATE, lambda g: wcm_blk0 + g),
                  chan(1, gc, lambda g: g), chan(1, SSD_STATE, lambda g: wbm_blk0 + g),
                  chan(1, SSD_STATE, lambda g: wcm_blk0 + g),
                  chan(1, SSD_HEADS, lambda g: 0), chan(1, SSD_HEADS, lambda g: 0), grp,
                  chan(1, gc, lambda g: g), s_in],
        out_specs=[seq(gc, lambda g: g), s_out],
        out_shape=[jax.ShapeDtypeStruct((bsz, t, SSD_D_INNER), BF16), s_shape],
        scratch_shapes=[pltpu.VMEM((SSD_STATE, gc), F32),
                        pltpu.VMEM((SUBLANES + tb, gc), F32), pltpu.VMEM((SUBLANES + tb, SSD_STATE), F32),
                        pltpu.VMEM((SUBLANES + tb, SSD_STATE), F32),
                        pltpu.VMEM((tb, gc), F32), pltpu.VMEM((tb, SSD_STATE), F32), pltpu.VMEM((tb, SSD_STATE), F32)],
        name="ssd_scan",
        compiler_params=_params("arbitrary", "arbitrary", "arbitrary"),
    )(zx, zx, zx, small, zx, conv_w, conv_w, conv_w, conv_b, conv_b, conv_b, dt_bias.reshape(1, SSD_HEADS),
      a_log.reshape(1, SSD_HEADS), per_group(d_skip), norm_g.reshape(1, SSD_D_INNER), side[0])


def _wkv_kernel(r_ref, k_ref, v_ref, glo_ref, wlo_ref, alo_ref,
                mur_ref, muk_ref, muv_ref, mug_ref, muw_ref, mua_ref,
                w2_ref, a2_ref, g2_ref, w0_ref, a0_ref, kk_ref, ka_ref, rk_ref, lng_ref, lnb_ref, side_in_ref,
                o_ref, side_out_ref,
                s_ref, prev_ref, plo_ref, rr_ref, kr_ref, vv_ref, lw_ref, kn_ref, ag_ref, gg_ref,
                *, n_chunks, n_pairs):
    tb = n_chunks * CHUNK
    glo_w = glo_ref.shape[2]
    side_out_ref[...] = side_in_ref[...].astype(BF16)

    @pl.when(pl.program_id(2) == 0)
    def _():
        s_ref[...] = jnp.zeros_like(s_ref)
        prev_ref[...] = jnp.zeros_like(prev_ref)
        plo_ref[...] = jnp.zeros_like(plo_ref)

    first_row = lax.broadcasted_iota(jnp.int32, (tb, 1), 0) == 0

    def lerp(x, prev_row, mu):
        shifted = jnp.where(first_row, prev_row, pltpu.roll(x, 1, 0))
        return x + (shifted - x) * mu

    r_raw, k_raw, v_raw = r_ref[0], k_ref[0], v_ref[0]
    glo_raw, wlo_raw, alo_raw = glo_ref[0], wlo_ref[0], alo_ref[0]
    r = lerp(r_raw, prev_ref[0:1, :], mur_ref[...])
    k = lerp(k_raw, prev_ref[1:2, :], muk_ref[...])
    v = lerp(v_raw, prev_ref[2:3, :], muv_ref[...])
    g_lo = lerp(glo_raw, plo_ref[0:1, 0:glo_w], mug_ref[...])
    w_lo = lerp(wlo_raw, plo_ref[0:1, glo_w:glo_w + W_RANK], muw_ref[...])
    a_lo = lerp(alo_raw, plo_ref[0:1, glo_w + W_RANK:], mua_ref[...])
    prev_ref[0:1, :] = r_raw[tb - 1:tb, :]
    prev_ref[1:2, :] = k_raw[tb - 1:tb, :]
    prev_ref[2:3, :] = v_raw[tb - 1:tb, :]
    plo_ref[0:1, 0:glo_w] = glo_raw[tb - 1:tb, :]
    plo_ref[0:1, glo_w:glo_w + W_RANK] = wlo_raw[tb - 1:tb, :]
    plo_ref[0:1, glo_w + W_RANK:] = alo_raw[tb - 1:tb, :]

    lw = (-DECAY_SCALE * LOG2E) * _sigmoid(w0_ref[...] + jnp.dot(jnp.tanh(w_lo).astype(BF16), w2_ref[...],
                                                                 preferred_element_type=F32))
    a_gate = _sigmoid(a0_ref[...] + jnp.dot(a_lo.astype(BF16), a2_ref[...], preferred_element_type=F32))
    gg_ref[...] = jnp.dot(_sigmoid(g_lo).astype(BF16), g2_ref[...], preferred_element_type=F32)
    top_blk = lax.broadcasted_iota(jnp.int32, (tb, LANES), 1) < RWKV_HEAD_DIM
    kk = k * kk_ref[...]
    for j in range(n_pairs):
        lanes = slice(j * LANES, (j + 1) * LANES)
        kj = kk[:, lanes]
        kn_ref[:, lanes] = kj * lax.rsqrt(jnp.maximum(_head_sum(kj * kj, top_blk), L2_EPS * L2_EPS))
    rr_ref[...] = r
    kr_ref[...] = k * (1.0 + (a_gate - 1.0) * ka_ref[...])
    vv_ref[...] = v
    lw_ref[...] = lw
    ag_ref[...] = a_gate

    row = lax.broadcasted_iota(jnp.int32, (LANES, LANES), 0)
    col = lax.broadcasted_iota(jnp.int32, (LANES, LANES), 1)
    same = (row >> LOG2_CHUNK) == (col >> LOG2_CHUNK)
    trow = lax.broadcasted_iota(jnp.int32, (tb, tb), 0)
    tcol = lax.broadcasted_iota(jnp.int32, (tb, tb), 1)
    tri_blk = ((trow >> LOG2_CHUNK) == (tcol >> LOG2_CHUNK)) & (trow >= tcol)
    top = lax.broadcasted_iota(jnp.int32, (CHUNK, LANES), 1) < RWKV_HEAD_DIM
    inv_n = 1.0 / RWKV_HEAD_DIM
    chains = [(j, c) for j in range(n_pairs) for c in range(n_chunks)]

    def piece(ref, j, c):
        return ref[c * CHUNK:(c + 1) * CHUNK, j * LANES:(j + 1) * LANES]

    cum_all = _cumsum_rows(tri_blk.astype(BF16), lw)

    lane_t = lax.broadcasted_iota(jnp.int32, (CHUNK, LANES), 1) & (CHUNK - 1)
    row_t = lax.broadcasted_iota(jnp.int32, (CHUNK, LANES), 0)
    strict_c = row_t > lane_t
    incl_c = row_t >= lane_t
    eye_c = (row_t == lane_t).astype(F32)
    strict2 = jnp.concatenate([strict_c, incl_c], axis=0)

    def bd(x):
        return _stack_heads(x.astype(BF16), top)

    nb = range(len(chains))
    a_c, r_c, b_bd, k_bd, v_c, bh_c, kh_c, wtot = [], [], [], [], [], [], [], []
    for j, c in chains:
        cum = cum_all[c * CHUNK:(c + 1) * CHUNK, j * LANES:(j + 1) * LANES]
        lwc = piece(lw_ref, j, c)
        kc = piece(kr_ref, j, c)
        knc = piece(kn_ref, j, c)
        b = knc * piece(ag_ref, j, c)
        tot = cum[CHUNK - 1:CHUNK, :]
        e_neg = jnp.exp2(-cum)
        e_end = jnp.exp2(tot - cum)
        a_c.append((-knc * jnp.exp2(cum - lwc)).astype(BF16))
        r_c.append(piece(rr_ref, j, c) * jnp.exp2(cum))
        b_bd.append(bd(b * e_neg))
        k_bd.append(bd(kc * e_neg))
        v_c.append(piece(vv_ref, j, c).astype(BF16))
        bh_c.append((b * e_end).astype(BF16))
        kh_c.append((kc * e_end).astype(BF16))
        wtot.append(jnp.exp2(tot))
    ar = [jnp.concatenate([a_c[i], r_c[i].astype(BF16)], axis=0) for i in nb]
    sb = [jnp.where(strict2, _dot_nt(ar[i], b_bd[i]), 0.0) for i in nb]
    sk = [jnp.where(strict2, _dot_nt(ar[i], k_bd[i]), 0.0) for i in nb]
    n = [x[:CHUNK] for x in sb]
    rb = [x[CHUNK:].astype(BF16) for x in sb]
    v_bd = [bd(x) for x in v_c]
    tinv = [eye_c + x for x in n]
    p_bd = [bd(x) for x in n]
    npow = [_dot(n[i], p_bd[i]) for i in nb]
    for it in range(1, 6):
        p_bd = [bd(x) for x in npow]
        if it < 5:
            both = [_dot(jnp.concatenate([npow[i], tinv[i]], axis=0), p_bd[i]) for i in nb]
            npow = [x[:CHUNK] for x in both]
            tinv = [tinv[i] + both[i][CHUNK:] for i in nb]
        else:
            tinv = [tinv[i] + _dot(tinv[i], p_bd[i]) for i in nb]
    akv = [_dot(sk[i][:CHUNK], v_bd[i]) for i in nb]
    pq = [_dot(tinv[i], jnp.concatenate([bd(a_c[i]), bd(akv[i])], axis=1)) for i in nb]
    pq_bd = [jnp.concatenate([bd(x[:, :LANES]), bd(x[:, LANES:])], axis=1) for x in pq]
    rpq = [_dot(rb[i], pq_bd[i]) for i in nb]
    r_eff = [(r_c[i] + rpq[i][:, :LANES]).astype(BF16) for i in nb]
    y0 = [rpq[i][:, LANES:] + _dot(sk[i][CHUNK:], v_bd[i]) for i in nb]
    g_eff = [jnp.where(same, _dot_tn(pq[i][:, :LANES], bh_c[i]), 0.0).astype(BF16) for i in nb]
    s1 = [jnp.where(same, _dot_tn(jnp.concatenate([pq[i][:, LANES:].astype(BF16), v_c[i]], axis=0),
                                  jnp.concatenate([bh_c[i], kh_c[i]], axis=0)), 0.0) for i in nb]

    for j in range(n_pairs):
        lanes = slice(j * LANES, (j + 1) * LANES)
        rk_row = rk_ref[:, lanes]
        lng_row = lng_ref[:, lanes]
        lnb_row = lnb_ref[:, lanes]
        s = s_ref[j]
        for c in range(n_chunks):
            i = j * n_chunks + c
            y = _dot_nt(r_eff[i], s) + y0[i]
            s = s * wtot[i] + _dot(s, g_eff[i]) + s1[i]
            mean = _head_sum(y, top) * inv_n
            yc = y - mean
            var = _head_sum(yc * yc, top) * inv_n
            yn = yc * lax.rsqrt(var + RWKV_GN_EPS) * lng_row + lnb_row
            bonus = _head_sum(piece(rr_ref, j, c) * piece(kr_ref, j, c) * rk_row, top) * piece(vv_ref, j, c)
            o_ref[0, c * CHUNK:(c + 1) * CHUNK, lanes] = ((yn + bonus) * piece(gg_ref, j, c)).astype(o_ref.dtype)
        s_ref[j] = s


def _wkv(rkv, small, mu, w2, a2, g2p, w0, a0, k_k, k_a, r_k, ln_g, ln_b, side, *, tb=256, pairs=8):
    bsz, t, d3 = rkv.shape
    d = d3 // 3
    wl = pairs * LANES
    nh = d // wl
    mu_rkv = mu[:3 * d].reshape(1, 3 * d)
    mu_w = mu[3 * d:3 * d + W_RANK].reshape(1, W_RANK)
    mu_a = mu[3 * d + W_RANK:3 * d + W_RANK + A_RANK].reshape(1, A_RANK)
    mu_g = jnp.pad(mu[3 * d + W_RANK + A_RANK:], (0, G_RANK_PAD - G_RANK)).reshape(1, G_RANK_PAD)
    seq = lambda w, f: pl.BlockSpec((1, tb, w), lambda b, h, ti: (b, ti, f(h)))
    chan = lambda r, w, f: pl.BlockSpec((r, w), lambda b, h, ti: (0, f(h)))
    vec = lambda q: q.reshape(1, d)
    blk = pltpu.VMEM((tb, wl), F32)
    nt_steps = t // tb
    s_in, s_out, s_shape = _side_specs(side, lambda b, h, ti: (b * nh + h) * nt_steps + ti, bsz * nh * nt_steps)
    return pl.pallas_call(
        functools.partial(_wkv_kernel, n_chunks=tb // CHUNK, n_pairs=pairs),
        grid=(bsz, nh, t // tb),
        in_specs=[seq(wl, lambda h: h), seq(wl, lambda h: nh + h), seq(wl, lambda h: 2 * nh + h),
                  seq(G_RANK_PAD, lambda h: 0), seq(W_RANK, lambda h: SMALL_WLO_BLK),
                  seq(A_RANK, lambda h: SMALL_ALO_BLK),
                  chan(1, wl, lambda h: h), chan(1, wl, lambda h: nh + h), chan(1, wl, lambda h: 2 * nh + h),
                  chan(1, G_RANK_PAD, lambda h: 0), chan(1, W_RANK, lambda h: 0), chan(1, A_RANK, lambda h: 0),
                  chan(W_RANK, wl, lambda h: h), chan(A_RANK, wl, lambda h: h), chan(G_RANK_PAD, wl, lambda h: h)]
                 + [chan(1, wl, lambda h: h)] * 7 + [s_in],
        out_specs=[seq(wl, lambda h: h), s_out],
        out_shape=[jax.ShapeDtypeStruct((bsz, t, d), BF16), s_shape],
        scratch_shapes=[pltpu.VMEM((pairs, LANES, LANES), F32), pltpu.VMEM((SUBLANES, wl), F32),
                        pltpu.VMEM((SUBLANES, G_RANK_PAD + W_RANK + A_RANK), F32)] + [blk] * 7,
        name="wkv7_scan",
        compiler_params=_params("arbitrary", "arbitrary", "arbitrary"),
    )(rkv, rkv, rkv, small, small, small, mu_rkv, mu_rkv, mu_rkv, mu_g, mu_w, mu_a,
      w2, a2, g2p, vec(w0), vec(a0), vec(k_k), vec(k_a), vec(r_k), vec(ln_g), vec(ln_b), side[0])


def kernel(x, p, norm_mix_g, w_in, ssd_conv_w, ssd_conv_b, ssd_dt_bias, ssd_a_log, ssd_d, ssd_norm_g, rwkv_mu, rwkv_w0, rwkv_w2, rwkv_a0, rwkv_a2, rwkv_g2, rwkv_k_k, rwkv_k_a, rwkv_r_k, rwkv_ln_g, rwkv_ln_b, w_branch_ssd, w_branch_rwkv, w_out, norm_ffn_g, w_ff1, w_ff2, norm_ple_g, w_ple_gate, w_ple_proj, ple_post_g, final_norm_g):
    bsz, t, d = x.shape
    m = bsz * t
    depth = w_in.shape[0]
    xf = x.reshape(m, d)
    for i in range(depth):
        c_dt = SSD_D_INNER + SSD_CONV_DIM
        c_r = c_dt + SSD_HEADS
        c_wlo = c_r + 3 * d
        c_glo = c_wlo + W_RANK + A_RANK
        c_gate = c_glo + G_RANK
        w_in_t = jnp.swapaxes(w_in[i], 0, 1)
        w_head = w_in_t[:c_r].astype(BF16)
        g2p = jnp.pad(rwkv_g2[i], ((0, G_RANK_PAD - G_RANK), (0, 0))).astype(BF16)
        whole = lambda w: (w, 0, w.shape[0])

        h = _rmsnorm(xf, norm_mix_g[i], BF16)
        proj = functools.partial(_mm, h, w_nt=True, out_dtype=F32, **TILE_WIDE)
        zx, w_rkv = proj(w_head, name="proj_zx", n=c_dt, side=(w_in_t, c_r, 3 * d))
        rkv, w_tail = proj(w_rkv, name="proj_rkv", n=3 * d, side=(w_in_t, c_wlo, w_in.shape[2] - c_wlo))
        gates, w_bssd_b = proj(w_tail, name="proj_gates", w_row0=c_gate - c_wlo, n=2 * d, side=whole(w_branch_ssd[i]))
        small = _proj_small(h, w_head, w_tail, dt_row0=c_dt, lo_row0=0).reshape(bsz, t, -1)
        zx = zx.reshape(bsz, t, -1)
        rkv = rkv.reshape(bsz, t, -1)

        u_ssd, w_ff1_b = _ssd(zx, small, ssd_conv_w[i], ssd_conv_b[i], ssd_dt_bias[i], ssd_a_log[i], ssd_d[i],
                              ssd_norm_g[i], whole(w_ff1[i]))
        u_rwkv, w_ff2_b = _wkv(rkv, small, rwkv_mu[i], rwkv_w2[i].astype(BF16), rwkv_a2[i].astype(BF16), g2p,
                               rwkv_w0[i], rwkv_a0[i], rwkv_k_k[i], rwkv_k_a[i], rwkv_r_k[i], rwkv_ln_g[i],
                               rwkv_ln_b[i], whole(w_ff2[i]))

        part, w_brwkv_b = _mm(u_ssd.reshape(m, SSD_D_INNER), w_bssd_b, name="branch_ssd", out_dtype=F32, **TILE_K2D,
                              mul=gates, mul_act="sigmoid", mul_col0=0, side=whole(w_branch_rwkv[i]))
        merged, w_out_b = _mm(u_rwkv.reshape(m, d), w_brwkv_b, name="branch_rwkv", out_dtype=BF16, **TILE_EPI,
                              mul=gates, mul_act="sigmoid", mul_col0=d, res=part, side=whole(w_out[i]))
        xf, xg, ss = _mm(merged, w_out_b, name="out_proj", out_dtype=F32, **TILE_EPI, res=xf,
                         next_norm_g=norm_ffn_g[i])

        ff, w_ple_b = _mm(xg, w_ff1_b, name="ffn_up", out_dtype=BF16, **TILE_WIDE, act="relu2", row_ss=ss,
                          side=whole(w_ple_gate[i]))
        xf, xg, ss = _mm(ff, w_ff2_b, name="ffn_down", out_dtype=F32, **TILE_K4D, res=xf,
                         next_norm_g=norm_ple_g[i])

        e = _ple_embed(p[i].reshape(m, PLE_DIM).astype(BF16), w_ple_proj[i].astype(BF16), ple_post_g[i])
        xf = _mm(xg, w_ple_b, name="ple_gate", out_dtype=F32, **TILE_EPI, act="sigmoid", mul=e, res=xf,
                 row_ss=ss)
    return _rmsnorm(xf, final_norm_g, F32).reshape(bsz, t, d)
```

```python
import functools
import math

import jax
import jax.numpy as jnp
from jax import lax
from jax.experimental import pallas as pl
from jax.experimental.pallas import tpu as pltpu

F32 = jnp.float32
BF16 = jnp.bfloat16

CHUNK = 64
LOG2_CHUNK = 6
SSD_D_INNER = 8192
SSD_HEAD_DIM = 64
SSD_HEADS = 128
SSD_GROUPS = 8
SSD_HPG = 16
SSD_STATE = 128
SSD_CONV = 4
SSD_GROUP_COLS = SSD_D_INNER // SSD_GROUPS
SSD_CONV_DIM = SSD_D_INNER + 2 * SSD_GROUPS * SSD_STATE
SSD_NORM_EPS = 1e-5
RWKV_HEAD_DIM = 64
W_RANK = 128
A_RANK = 128
G_RANK = 480
G_RANK_PAD = 512
DECAY_SCALE = math.exp(-0.5)
RWKV_GN_EPS = RWKV_HEAD_DIM * 1e-5
L2_EPS = 1e-12
LOG2E = math.log2(math.e)
NORM_EPS = 1e-6
PLE_DIM = 256

SMALL_COLS = G_RANK_PAD + SSD_HEADS + W_RANK + A_RANK
SMALL_LO_COLS = W_RANK + A_RANK + G_RANK_PAD

LANES = 128
SUBLANES = 8
BF16_ROWS = 16
VMEM_BYTES = 64 * 1024 * 1024
VMEM_LIMIT_BYTES = VMEM_BYTES - 8 * 1024 * 1024

SMALL_DT_BLK = G_RANK_PAD // LANES
SMALL_WLO_BLK = SMALL_DT_BLK + 1
SMALL_ALO_BLK = SMALL_DT_BLK + 2

TILE_WIDE = dict(tm=1024, tn=1024)
TILE_EPI = dict(tm=1024, tn=512)
TILE_K2D = dict(tm=1024, tn=256)
TILE_K4D = dict(tm=512, tn=256)


def _params(*sem):
    return pltpu.CompilerParams(dimension_semantics=sem, vmem_limit_bytes=VMEM_LIMIT_BYTES)


def _split3(x):
    hi = x.astype(BF16)
    r1 = x - hi.astype(F32)
    mid = r1.astype(BF16)
    lo = (r1 - mid.astype(F32)).astype(BF16)
    return hi, mid, lo


def _dot(a, b, dims=(((1,), (0,)), ((), ()))):
    return lax.dot_general(a.astype(BF16), b.astype(BF16), dims, preferred_element_type=F32)


def _dot_nt(a, b):
    return _dot(a, b, (((1,), (1,)), ((), ())))


def _dot_tn(a, b):
    return _dot(a, b, (((0,), (0,)), ((), ())))


def _cumsum_rows(tri_bf16, x):
    x0, x1, x2 = _split3(x)
    d = lambda q: jnp.dot(tri_bf16, q, preferred_element_type=F32)
    return d(x0) + d(x1) + d(x2)


def _sigmoid(x):
    return 0.5 * jnp.tanh(0.5 * x) + 0.5


def _silu(x):
    h = 0.5 * x
    return h + h * jnp.tanh(h)


def _softplus(x):
    return jnp.maximum(x, 0.0) + jnp.log1p(jnp.exp(-jnp.abs(x)))


def _stack_heads(x, top):
    return jnp.concatenate([jnp.where(top, x, 0.0), jnp.where(top, 0.0, x)], axis=0)


def _head_sum(x, top):
    s0 = jnp.sum(jnp.where(top, x, 0.0), axis=-1, keepdims=True)
    s1 = jnp.sum(jnp.where(top, 0.0, x), axis=-1, keepdims=True)
    return jnp.where(top, s0, s1)


def _side_specs(side, step_of, n_steps):
    src, row0, nrows = side
    cols = src.shape[1]
    assert src.ndim == 2 and nrows % BF16_ROWS == 0 and row0 % SUBLANES == 0
    rs = next(r for r in range(BF16_ROWS, nrows + 1, BF16_ROWS) if nrows % r == 0 and nrows // r <= n_steps)
    nslabs = nrows // rs
    slab = lambda *g: jnp.minimum(step_of(*g), nslabs - 1)
    in_spec = pl.BlockSpec((pl.Element(rs), pl.Element(cols)),
                           lambda *g: (pl.multiple_of(row0 + slab(*g) * rs, SUBLANES), 0))
    out_spec = pl.BlockSpec((rs, cols), lambda *g: (slab(*g), 0))
    return in_spec, out_spec, jax.ShapeDtypeStruct((nrows, cols), BF16)


def _rmsnorm_kernel(x_ref, g_ref, o_ref, *, eps):
    x = x_ref[...]
    y = x * lax.rsqrt(jnp.mean(x * x, axis=-1, keepdims=True) + eps)
    o_ref[...] = (y * g_ref[...]).astype(o_ref.dtype)


def _rmsnorm(x, g, out_dtype, tm=256):
    m, d = x.shape
    return pl.pallas_call(
        functools.partial(_rmsnorm_kernel, eps=NORM_EPS),
        grid=(m // tm,),
        in_specs=[pl.BlockSpec((tm, d), lambda i: (i, 0)), pl.BlockSpec((1, d), lambda i: (0, 0))],
        out_specs=pl.BlockSpec((tm, d), lambda i: (i, 0)),
        out_shape=jax.ShapeDtypeStruct((m, d), out_dtype),
        name="rmsnorm",
        compiler_params=_params("parallel"),
    )(x, g.reshape(1, d))


def _act(x, kind):
    if kind is None:
        return x
    if kind == "relu2":
        r = jnp.maximum(x, 0.0)
        return r * r
    if kind == "sigmoid":
        return _sigmoid(x)
    raise ValueError(kind)


def _mm_kernel(*refs, nk, w_nt, act, has_mul, mul_act, has_res, has_scale, has_norm, has_side, norm_dim):
    refs = list(refs)
    x_ref, w_ref = refs[0], refs[1]
    pos = 2

    def take(flag):
        nonlocal pos
        if not flag:
            return None
        pos += 1
        return refs[pos - 1]

    mul_ref, res_ref, ss_in_ref, g_ref, side_in_ref = (take(f) for f in (has_mul, has_res, has_scale, has_norm, has_side))
    o_ref = take(True)
    xg_ref, ss_out_ref, side_out_ref = take(has_norm), take(has_norm), take(has_side)
    if has_side:
        side_out_ref[...] = side_in_ref[...].astype(BF16)
    acc_ref = None if nk == 1 else (o_ref if o_ref.dtype == F32 else refs[pos])

    if w_nt:
        part = lax.dot_general(x_ref[...], w_ref[...], (((1,), (1,)), ((), ())), preferred_element_type=F32)
    else:
        part = jnp.dot(x_ref[...], w_ref[...], preferred_element_type=F32)

    def finish(acc):
        if has_scale:
            acc = acc * lax.rsqrt(ss_in_ref[:, 0:1] * (1.0 / norm_dim) + NORM_EPS)
        out = _act(acc, act)
        if has_mul:
            out = out * _act(mul_ref[...].astype(F32), mul_act)
        if has_res:
            out = res_ref[...].astype(F32) + out
        o_ref[...] = out.astype(o_ref.dtype)
        if has_norm:
            xg_ref[...] = (out * g_ref[...]).astype(BF16)
            row_ss = jnp.broadcast_to(jnp.sum(out * out, axis=1, keepdims=True), ss_out_ref.shape)
            first = pl.program_id(1) == 0

            @pl.when(first)
            def _():
                ss_out_ref[...] = row_ss

            @pl.when(jnp.logical_not(first))
            def _():
                ss_out_ref[...] += row_ss

    if nk == 1:
        finish(part)
    else:
        k = pl.program_id(2)

        @pl.when(k == 0)
        def _():
            acc_ref[...] = part

        @pl.when((k > 0) & (k < nk - 1))
        def _():
            acc_ref[...] += part

        @pl.when(k == nk - 1)
        def _():
            finish(acc_ref[...] + part)


def _mm(x, w, *, name, out_dtype, tm, tn, tk=None, w_nt=False, w_row0=0, n=None, act=None, mul=None, mul_act=None,
        mul_col0=0, res=None, res_col0=0, side=None, row_ss=None, next_norm_g=None):
    m, kdim = x.shape
    n = w.shape[1] if n is None else n
    tk = kdim if tk is None else tk
    tm, tn = min(tm, m), min(tn, n)
    nk = kdim // tk
    assert m % tm == 0 and n % tn == 0 and kdim % tk == 0 and mul_col0 % tn == 0 and res_col0 % tn == 0
    if w_nt:
        assert w.shape[1] == kdim and w_row0 % BF16_ROWS == 0 and tn % BF16_ROWS == 0 and w.dtype == BF16
        w_spec = pl.BlockSpec((pl.Element(tn), pl.Element(tk)),
                              lambda i, j, k: (pl.multiple_of(w_row0 + j * tn, BF16_ROWS), k * tk))
    else:
        assert w.shape[0] == kdim and w_row0 == 0
        w_spec = pl.BlockSpec((tk, tn), lambda i, j, k: (k, j))
    in_specs = [pl.BlockSpec((tm, tk), lambda i, j, k: (i, k)), w_spec]
    args = [x, w]
    if mul is not None:
        off = mul_col0 // tn
        in_specs.append(pl.BlockSpec((tm, tn), lambda i, j, k, off=off: (i, j + off)))
        args.append(mul)
    if res is not None:
        off = res_col0 // tn
        in_specs.append(pl.BlockSpec((tm, tn), lambda i, j, k, off=off: (i, j + off)))
        args.append(res)
    if row_ss is not None:
        in_specs.append(pl.BlockSpec((tm, LANES), lambda i, j, k: (i, 0)))
        args.append(row_ss)
    grid = (m // tm, n // tn, nk)
    out_specs = [pl.BlockSpec((tm, tn), lambda i, j, k: (i, j))]
    out_shape = [jax.ShapeDtypeStruct((m, n), out_dtype)]
    if next_norm_g is not None:
        assert nk == 1 and out_dtype == F32
        in_specs.append(pl.BlockSpec((1, tn), lambda i, j, k: (0, j)))
        args.append(next_norm_g.reshape(1, n))
        out_specs += [pl.BlockSpec((tm, tn), lambda i, j, k: (i, j)), pl.BlockSpec((tm, LANES), lambda i, j, k: (i, 0))]
        out_shape += [jax.ShapeDtypeStruct((m, n), BF16), jax.ShapeDtypeStruct((m, LANES), F32)]
    if side is not None:
        s_in, s_out, s_shape = _side_specs(side, lambda i, j, k: (i * grid[1] + j) * nk + k, grid[0] * grid[1] * nk)
        in_specs.append(s_in)
        args.append(side[0])
        out_specs.append(s_out)
        out_shape.append(s_shape)
    outs = pl.pallas_call(
        functools.partial(_mm_kernel, nk=nk, w_nt=w_nt, act=act, has_mul=mul is not None, mul_act=mul_act,
                          has_res=res is not None, has_scale=row_ss is not None, has_norm=next_norm_g is not None,
                          has_side=side is not None, norm_dim=kdim),
        grid=grid,
        in_specs=in_specs,
        out_specs=out_specs,
        out_shape=out_shape,
        scratch_shapes=[pltpu.VMEM((tm, tn), F32)] if nk > 1 and out_dtype != F32 else [],
        name=name,
        compiler_params=_params(*(("parallel", "parallel") if len(out_shape) == 1 else ("arbitrary", "arbitrary")),
                                "arbitrary"),
    )(*args)
    return outs[0] if len(outs) == 1 else tuple(outs)


def _proj_small_kernel(x_ref, wdt_ref, wlo_ref, o_ref):
    x = x_ref[...]
    nt = (((1,), (1,)), ((), ()))
    dt = lax.dot_general(x, wdt_ref[...], nt, preferred_element_type=F32)
    lo = lax.dot_general(x, wlo_ref[...], nt, preferred_element_type=F32)
    o_ref[:, 0:G_RANK_PAD] = lo[:, W_RANK + A_RANK:]
    o_ref[:, G_RANK_PAD:G_RANK_PAD + SSD_HEADS] = dt
    o_ref[:, G_RANK_PAD + SSD_HEADS:] = lo[:, :W_RANK + A_RANK]


def _proj_small(x, wt_dt, wt_lo, *, dt_row0, lo_row0, tm=512):
    m, kdim = x.shape
    win = lambda rows, row0: pl.BlockSpec((pl.Element(rows), pl.Element(kdim)), lambda i: (row0, 0))
    return pl.pallas_call(
        _proj_small_kernel,
        grid=(m // tm,),
        in_specs=[pl.BlockSpec((tm, kdim), lambda i: (i, 0)), win(SSD_HEADS, dt_row0), win(SMALL_LO_COLS, lo_row0)],
        out_specs=pl.BlockSpec((tm, SMALL_COLS), lambda i: (i, 0)),
        out_shape=jax.ShapeDtypeStruct((m, SMALL_COLS), F32),
        name="proj_small",
        compiler_params=_params("parallel"),
    )(x, wt_dt, wt_lo)


def _ple_embed_kernel(p_ref, w_ref, g_ref, o_ref):
    e = jnp.dot(p_ref[...], w_ref[...], preferred_element_type=F32)
    y = e * lax.rsqrt(jnp.mean(e * e, axis=-1, keepdims=True) + NORM_EPS)
    o_ref[...] = (y * g_ref[...]).astype(o_ref.dtype)


def _ple_embed(p, w, g, tm=256):
    m, kdim = p.shape
    n = w.shape[1]
    return pl.pallas_call(
        _ple_embed_kernel,
        grid=(m // tm,),
        in_specs=[pl.BlockSpec((tm, kdim), lambda i: (i, 0)), pl.BlockSpec((kdim, n), lambda i: (0, 0)),
                  pl.BlockSpec((1, n), lambda i: (0, 0))],
        out_specs=pl.BlockSpec((tm, n), lambda i: (i, 0)),
        out_shape=jax.ShapeDtypeStruct((m, n), F32),
        name="ple_embed",
        compiler_params=_params("parallel"),
    )(p, w, g.reshape(1, n))


def _ssd_kernel(xs_ref, bm_ref, cm_ref, dt_ref, z_ref, wxs_ref, wbm_ref, wcm_ref, bxs_ref, bbm_ref, bcm_ref,
                dtb_ref, alog_ref, d_ref, ng_ref, side_in_ref, o_ref, side_out_ref,
                st_ref, hxs_ref, hbm_ref, hcm_ref, axs_ref, abm_ref, acm_ref, *, n_chunks):
    tb = n_chunks * CHUNK
    hist = SUBLANES
    side_out_ref[...] = side_in_ref[...].astype(BF16)

    @pl.when(pl.program_id(2) == 0)
    def _():
        st_ref[...] = jnp.zeros_like(st_ref)
        for h_ref in (hxs_ref, hbm_ref, hcm_ref):
            h_ref[0:hist, :] = jnp.zeros((hist, h_ref.shape[1]), F32)

    for raw_ref, h_ref, w_ref, b_ref, act_ref in ((xs_ref, hxs_ref, wxs_ref, bxs_ref, axs_ref),
                                                  (bm_ref, hbm_ref, wbm_ref, bbm_ref, abm_ref),
                                                  (cm_ref, hcm_ref, wcm_ref, bcm_ref, acm_ref)):
        h_ref[hist:hist + tb, :] = raw_ref[0]
        hv = h_ref[...]
        acc = b_ref[...] + w_ref[SSD_CONV - 1:SSD_CONV, :] * hv[hist:]
        for j in range(SSD_CONV - 1):
            acc = acc + w_ref[j:j + 1, :] * pltpu.roll(hv, SSD_CONV - 1 - j, 0)[hist:]
        act_ref[...] = _silu(acc)
        h_ref[0:hist, :] = raw_ref[0, tb - hist:tb, :]

    gc = SSD_GROUP_COLS
    g = pl.program_id(1)
    expand = (lax.broadcasted_iota(jnp.int32, (SSD_HEADS, gc), 0)
              == (lax.broadcasted_iota(jnp.int32, (SSD_HEADS, gc), 1) >> LOG2_CHUNK) + g * SSD_HPG).astype(BF16)
    li = lax.broadcasted_iota(jnp.int32, (CHUNK, gc), 0)
    si = lax.broadcasted_iota(jnp.int32, (CHUNK, gc), 1) & (CHUNK - 1)
    eye_t = li == si
    causal_t = li >= si
    top = lax.broadcasted_iota(jnp.int32, (CHUNK, LANES), 1) < SSD_HEAD_DIM
    d_row = d_ref[0]
    ng_row = ng_ref[...]
    trow = lax.broadcasted_iota(jnp.int32, (tb, tb), 0)
    tcol = lax.broadcasted_iota(jnp.int32, (tb, tb), 1)
    tri_blk = (((trow >> LOG2_CHUNK) == (tcol >> LOG2_CHUNK)) & (trow >= tcol)).astype(BF16)
    dt_all = _softplus(dt_ref[0] + dtb_ref[...])
    acs_all = _cumsum_rows(tri_blk, dt_all * (-LOG2E * jnp.exp(alog_ref[...])))

    def expand_heads(q):
        q0, q1, q2 = _split3(q)
        de = lambda r: jnp.dot(r, expand, preferred_element_type=F32)
        return de(q0) + de(q1) + de(q2)

    st = st_ref[...]
    for c in range(n_chunks):
        sl = slice(c * CHUNK, (c + 1) * CHUNK)
        xs = axs_ref[sl, :]
        bm = abm_ref[sl, :]
        cm = acm_ref[sl, :]
        dt_exp = expand_heads(dt_all[sl, :])
        acs = expand_heads(acs_all[sl, :])
        rowpart = jnp.sum(jnp.where(eye_t, acs, 0.0), axis=0, keepdims=True)
        decay = jnp.exp2(jnp.where(causal_t, acs - rowpart, -jnp.inf))
        xdt = xs * dt_exp
        cb2 = _dot_nt(cm, jnp.concatenate([bm, bm], axis=0))
        last = acs[CHUNK - 1:CHUNK, :]
        y_off = _dot(cm, st) * jnp.exp2(acs)
        parts = []
        for j in range(gc // LANES):
            lanes = slice(j * LANES, (j + 1) * LANES)
            parts.append(_dot(cb2 * decay[:, lanes], _stack_heads(xdt[:, lanes], top)))
        y = jnp.concatenate(parts, axis=1) + y_off + d_row * xs
        st = st * jnp.exp2(last) + _dot_tn(bm, xdt * jnp.exp2(last - acs))
        y = y * _silu(z_ref[0, sl, :])
        y = y * lax.rsqrt(jnp.mean(y * y, axis=-1, keepdims=True) + SSD_NORM_EPS)
        o_ref[0, sl, :] = (y * ng_row).astype(o_ref.dtype)
    st_ref[...] = st


def _ssd(zx, small, conv_w, conv_b, dt_bias, a_log, d_skip, norm_g, side, *, tb=256):
    bsz, t, _ = zx.shape
    gc = SSD_GROUP_COLS
    xs_blk0 = SSD_D_INNER // gc
    bm_blk0 = 2 * SSD_D_INNER // SSD_STATE
    cm_blk0 = bm_blk0 + SSD_GROUPS
    wbm_blk0 = SSD_D_INNER // SSD_STATE
    wcm_blk0 = wbm_blk0 + SSD_GROUPS
    conv_b = conv_b.reshape(1, SSD_CONV_DIM)
    per_group = lambda v: jnp.repeat(v, SSD_HEAD_DIM).reshape(SSD_GROUPS, 1, gc)
    seq = lambda w, f: pl.BlockSpec((1, tb, w), lambda b, g, ti: (b, ti, f(g)))
    chan = lambda r, w, f: pl.BlockSpec((r, w), lambda b, g, ti: (0, f(g)))
    grp = pl.BlockSpec((1, 1, gc), lambda b, g, ti: (g, 0, 0))
    nt_steps = t // tb
    s_in, s_out, s_shape = _side_specs(side, lambda b, g, ti: (b * SSD_GROUPS + g) * nt_steps + ti,
                                       bsz * SSD_GROUPS * nt_steps)
    return pl.pallas_call(
        functools.partial(_ssd_kernel, n_chunks=tb // CHUNK),
        grid=(bsz, SSD_GROUPS, t // tb),
        in_specs=[seq(gc, lambda g: xs_blk0 + g), seq(SSD_STATE, lambda g: bm_blk0 + g),
                  seq(SSD_STATE, lambda g: cm_blk0 + g), seq(SSD_HEADS, lambda g: SMALL_DT_BLK),
                  seq(gc, lambda g: g),
                  chan(SSD_CONV, gc, lambda g: g), chan(SSD_CONV, SSD_STATE, lambda g: wbm_blk0 + g),
                  chan(SSD_CONV, SSD_STATE, lambda g: wcm_blk0 + g),
                  chan(1, gc, lambda g: g), chan(1, SSD_STATE, lambda g: wbm_blk0 + g),
                  chan(1, SSD_STATE, lambda g: wcm_blk0 + g),
                  chan(1, SSD_HEADS, lambda g: 0), chan(1, SSD_HEADS, lambda g: 0), grp,
                  chan(1, gc, lambda g: g), s_in],
        out_specs=[seq(gc, lambda g: g), s_out],
        out_shape=[jax.ShapeDtypeStruct((bsz, t, SSD_D_INNER), BF16), s_shape],
        scratch_shapes=[pltpu.VMEM((SSD_STATE, gc), F32),
                        pltpu.VMEM((SUBLANES + tb, gc), F32), pltpu.VMEM((SUBLANES + tb, SSD_STATE), F32),
                        pltpu.VMEM((SUBLANES + tb, SSD_STATE), F32),
                        pltpu.VMEM((tb, gc), F32), pltpu.VMEM((tb, SSD_STATE), F32), pltpu.VMEM((tb, SSD_STATE), F32)],
        name="ssd_scan",
        compiler_params=_params("arbitrary", "arbitrary", "arbitrary"),
    )(zx, zx, zx, small, zx, conv_w, conv_w, conv_w, conv_b, conv_b, conv_b, dt_bias.reshape(1, SSD_HEADS),
      a_log.reshape(1, SSD_HEADS), per_group(d_skip), norm_g.reshape(1, SSD_D_INNER), side[0])


def _wkv_kernel(r_ref, k_ref, v_ref, glo_ref, wlo_ref, alo_ref,
                mur_ref, muk_ref, muv_ref, mug_ref, muw_ref, mua_ref,
                w2_ref, a2_ref, g2_ref, w0_ref, a0_ref, kk_ref, ka_ref, rk_ref, lng_ref, lnb_ref, side_in_ref,
                o_ref, side_out_ref,
                s_ref, prev_ref, plo_ref, rr_ref, kr_ref, vv_ref, lw_ref, kn_ref, ag_ref, gg_ref,
                *, n_chunks, n_pairs):
    tb = n_chunks * CHUNK
    glo_w = glo_ref.shape[2]
    side_out_ref[...] = side_in_ref[...].astype(BF16)

    @pl.when(pl.program_id(2) == 0)
    def _():
        s_ref[...] = jnp.zeros_like(s_ref)
        prev_ref[...] = jnp.zeros_like(prev_ref)
        plo_ref[...] = jnp.zeros_like(plo_ref)

    first_row = lax.broadcasted_iota(jnp.int32, (tb, 1), 0) == 0

    def lerp(x, prev_row, mu):
        shifted = jnp.where(first_row, prev_row, pltpu.roll(x, 1, 0))
        return x + (shifted - x) * mu

    r_raw, k_raw, v_raw = r_ref[0], k_ref[0], v_ref[0]
    glo_raw, wlo_raw, alo_raw = glo_ref[0], wlo_ref[0], alo_ref[0]
    r = lerp(r_raw, prev_ref[0:1, :], mur_ref[...])
    k = lerp(k_raw, prev_ref[1:2, :], muk_ref[...])
    v = lerp(v_raw, prev_ref[2:3, :], muv_ref[...])
    g_lo = lerp(glo_raw, plo_ref[0:1, 0:glo_w], mug_ref[...])
    w_lo = lerp(wlo_raw, plo_ref[0:1, glo_w:glo_w + W_RANK], muw_ref[...])
    a_lo = lerp(alo_raw, plo_ref[0:1, glo_w + W_RANK:], mua_ref[...])
    prev_ref[0:1, :] = r_raw[tb - 1:tb, :]
    prev_ref[1:2, :] = k_raw[tb - 1:tb, :]
    prev_ref[2:3, :] = v_raw[tb - 1:tb, :]
    plo_ref[0:1, 0:glo_w] = glo_raw[tb - 1:tb, :]
    plo_ref[0:1, glo_w:glo_w + W_RANK] = wlo_raw[tb - 1:tb, :]
    plo_ref[0:1, glo_w + W_RANK:] = alo_raw[tb - 1:tb, :]

    lw = (-DECAY_SCALE * LOG2E) * _sigmoid(w0_ref[...] + jnp.dot(jnp.tanh(w_lo).astype(BF16), w2_ref[...],
                                                                 preferred_element_type=F32))
    a_gate = _sigmoid(a0_ref[...] + jnp.dot(a_lo.astype(BF16), a2_ref[...], preferred_element_type=F32))
    gg_ref[...] = jnp.dot(_sigmoid(g_lo).astype(BF16), g2_ref[...], preferred_element_type=F32)
    top_blk = lax.broadcasted_iota(jnp.int32, (tb, LANES), 1) < RWKV_HEAD_DIM
    kk = k * kk_ref[...]
    for j in range(n_pairs):
        lanes = slice(j * LANES, (j + 1) * LANES)
        kj = kk[:, lanes]
        kn_ref[:, lanes] = kj * lax.rsqrt(jnp.maximum(_head_sum(kj * kj, top_blk), L2_EPS * L2_EPS))
    rr_ref[...] = r
    kr_ref[...] = k * (1.0 + (a_gate - 1.0) * ka_ref[...])
    vv_ref[...] = v
    lw_ref[...] = lw
    ag_ref[...] = a_gate

    row = lax.broadcasted_iota(jnp.int32, (LANES, LANES), 0)
    col = lax.broadcasted_iota(jnp.int32, (LANES, LANES), 1)
    same = (row >> LOG2_CHUNK) == (col >> LOG2_CHUNK)
    trow = lax.broadcasted_iota(jnp.int32, (tb, tb), 0)
    tcol = lax.broadcasted_iota(jnp.int32, (tb, tb), 1)
    tri_blk = ((trow >> LOG2_CHUNK) == (tcol >> LOG2_CHUNK)) & (trow >= tcol)
    top = lax.broadcasted_iota(jnp.int32, (CHUNK, LANES), 1) < RWKV_HEAD_DIM
    inv_n = 1.0 / RWKV_HEAD_DIM
    chains = [(j, c) for j in range(n_pairs) for c in range(n_chunks)]

    def piece(ref, j, c):
        return ref[c * CHUNK:(c + 1) * CHUNK, j * LANES:(j + 1) * LANES]

    cum_all = _cumsum_rows(tri_blk.astype(BF16), lw)

    lane_t = lax.broadcasted_iota(jnp.int32, (CHUNK, LANES), 1) & (CHUNK - 1)
    row_t = lax.broadcasted_iota(jnp.int32, (CHUNK, LANES), 0)
    strict_c = row_t > lane_t
    incl_c = row_t >= lane_t
    eye_c = (row_t == lane_t).astype(F32)
    strict2 = jnp.concatenate([strict_c, incl_c], axis=0)

    def bd(x):
        return _stack_heads(x.astype(BF16), top)

    nb = range(len(chains))
    a_c, r_c, b_bd, k_bd, v_c, bh_c, kh_c, wtot = [], [], [], [], [], [], [], []
    for j, c in chains:
        cum = cum_all[c * CHUNK:(c + 1) * CHUNK, j * LANES:(j + 1) * LANES]
        lwc = piece(lw_ref, j, c)
        kc = piece(kr_ref, j, c)
        knc = piece(kn_ref, j, c)
        b = knc * piece(ag_ref, j, c)
        tot = cum[CHUNK - 1:CHUNK, :]
        e_neg = jnp.exp2(-cum)
        e_end = jnp.exp2(tot - cum)
        a_c.append((-knc * jnp.exp2(cum - lwc)).astype(BF16))
        r_c.append(piece(rr_ref, j, c) * jnp.exp2(cum))
        b_bd.append(bd(b * e_neg))
        k_bd.append(bd(kc * e_neg))
        v_c.append(piece(vv_ref, j, c).astype(BF16))
        bh_c.append((b * e_end).astype(BF16))
        kh_c.append((kc * e_end).astype(BF16))
        wtot.append(jnp.exp2(tot))
    ar = [jnp.concatenate([a_c[i], r_c[i].astype(BF16)], axis=0) for i in nb]
    sb = [jnp.where(strict2, _dot_nt(ar[i], b_bd[i]), 0.0) for i in nb]
    sk = [jnp.where(strict2, _dot_nt(ar[i], k_bd[i]), 0.0) for i in nb]
    n = [x[:CHUNK] for x in sb]
    rb = [x[CHUNK:].astype(BF16) for x in sb]
    v_bd = [bd(x) for x in v_c]
    tinv = [eye_c + x for x in n]
    p_bd = [bd(x) for x in n]
    npow = [_dot(n[i], p_bd[i]) for i in nb]
    for it in range(1, 6):
        p_bd = [bd(x) for x in npow]
        if it < 5:
            both = [_dot(jnp.concatenate([npow[i], tinv[i]], axis=0), p_bd[i]) for i in nb]
            npow = [x[:CHUNK] for x in both]
            tinv = [tinv[i] + both[i][CHUNK:] for i in nb]
        else:
            tinv = [tinv[i] + _dot(tinv[i], p_bd[i]) for i in nb]
    akv = [_dot(sk[i][:CHUNK], v_bd[i]) for i in nb]
    pq = [_dot(tinv[i], jnp.concatenate([bd(a_c[i]), bd(akv[i])], axis=1)) for i in nb]
    pq_bd = [jnp.concatenate([bd(x[:, :LANES]), bd(x[:, LANES:])], axis=1) for x in pq]
    rpq = [_dot(rb[i], pq_bd[i]) for i in nb]
    r_eff = [(r_c[i] + rpq[i][:, :LANES]).astype(BF16) for i in nb]
    y0 = [rpq[i][:, LANES:] + _dot(sk[i][CHUNK:], v_bd[i]) for i in nb]
    g_eff = [jnp.where(same, _dot_tn(pq[i][:, :LANES], bh_c[i]), 0.0).astype(BF16) for i in nb]
    s1 = [jnp.where(same, _dot_tn(jnp.concatenate([pq[i][:, LANES:].astype(BF16), v_c[i]], axis=0),
                                  jnp.concatenate([bh_c[i], kh_c[i]], axis=0)), 0.0) for i in nb]

    for j in range(n_pairs):
        lanes = slice(j * LANES, (j + 1) * LANES)
        rk_row = rk_ref[:, lanes]
        lng_row = lng_ref[:, lanes]
        lnb_row = lnb_ref[:, lanes]
        s = s_ref[j]
        for c in range(n_chunks):
            i = j * n_chunks + c
            y = _dot_nt(r_eff[i], s) + y0[i]
            s = s * wtot[i] + _dot(s, g_eff[i]) + s1[i]
            mean = _head_sum(y, top) * inv_n
            yc = y - mean
            var = _head_sum(yc * yc, top) * inv_n
            yn = yc * lax.rsqrt(var + RWKV_GN_EPS) * lng_row + lnb_row
            bonus = _head_sum(piece(rr_ref, j, c) * piece(kr_ref, j, c) * rk_row, top) * piece(vv_ref, j, c)
            o_ref[0, c * CHUNK:(c + 1) * CHUNK, lanes] = ((yn + bonus) * piece(gg_ref, j, c)).astype(o_ref.dtype)
        s_ref[j] = s


def _wkv(rkv, small, mu, w2, a2, g2p, w0, a0, k_k, k_a, r_k, ln_g, ln_b, side, *, tb=256, pairs=8):
    bsz, t, d3 = rkv.shape
    d = d3 // 3
    wl = pairs * LANES
    nh = d // wl
    mu_rkv = mu[:3 * d].reshape(1, 3 * d)
    mu_w = mu[3 * d:3 * d + W_RANK].reshape(1, W_RANK)
    mu_a = mu[3 * d + W_RANK:3 * d + W_RANK + A_RANK].reshape(1, A_RANK)
    mu_g = jnp.pad(mu[3 * d + W_RANK + A_RANK:], (0, G_RANK_PAD - G_RANK)).reshape(1, G_RANK_PAD)
    seq = lambda w, f: pl.BlockSpec((1, tb, w), lambda b, h, ti: (b, ti, f(h)))
    chan = lambda r, w, f: pl.BlockSpec((r, w), lambda b, h, ti: (0, f(h)))
    vec = lambda q: q.reshape(1, d)
    blk = pltpu.VMEM((tb, wl), F32)
    nt_steps = t // tb
    s_in, s_out, s_shape = _side_specs(side, lambda b, h, ti: (b * nh + h) * nt_steps + ti, bsz * nh * nt_steps)
    return pl.pallas_call(
        functools.partial(_wkv_kernel, n_chunks=tb // CHUNK, n_pairs=pairs),
        grid=(bsz, nh, t // tb),
        in_specs=[seq(wl, lambda h: h), seq(wl, lambda h: nh + h), seq(wl, lambda h: 2 * nh + h),
                  seq(G_RANK_PAD, lambda h: 0), seq(W_RANK, lambda h: SMALL_WLO_BLK),
                  seq(A_RANK, lambda h: SMALL_ALO_BLK),
                  chan(1, wl, lambda h: h), chan(1, wl, lambda h: nh + h), chan(1, wl, lambda h: 2 * nh + h),
                  chan(1, G_RANK_PAD, lambda h: 0), chan(1, W_RANK, lambda h: 0), chan(1, A_RANK, lambda h: 0),
                  chan(W_RANK, wl, lambda h: h), chan(A_RANK, wl, lambda h: h), chan(G_RANK_PAD, wl, lambda h: h)]
                 + [chan(1, wl, lambda h: h)] * 7 + [s_in],
        out_specs=[seq(wl, lambda h: h), s_out],
        out_shape=[jax.ShapeDtypeStruct((bsz, t, d), BF16), s_shape],
        scratch_shapes=[pltpu.VMEM((pairs, LANES, LANES), F32), pltpu.VMEM((SUBLANES, wl), F32),
                        pltpu.VMEM((SUBLANES, G_RANK_PAD + W_RANK + A_RANK), F32)] + [blk] * 7,
        name="wkv7_scan",
        compiler_params=_params("arbitrary", "arbitrary", "arbitrary"),
    )(rkv, rkv, rkv, small, small, small, mu_rkv, mu_rkv, mu_rkv, mu_g, mu_w, mu_a,
      w2, a2, g2p, vec(w0), vec(a0), vec(k_k), vec(k_a), vec(r_k), vec(ln_g), vec(ln_b), side[0])


def kernel(x, p, norm_mix_g, w_in, ssd_conv_w, ssd_conv_b, ssd_dt_bias, ssd_a_log, ssd_d, ssd_norm_g, rwkv_mu, rwkv_w0, rwkv_w2, rwkv_a0, rwkv_a2, rwkv_g2, rwkv_k_k, rwkv_k_a, rwkv_r_k, rwkv_ln_g, rwkv_ln_b, w_branch_ssd, w_branch_rwkv, w_out, norm_ffn_g, w_ff1, w_ff2, norm_ple_g, w_ple_gate, w_ple_proj, ple_post_g, final_norm_g):
    bsz, t, d = x.shape
    m = bsz * t
    depth = w_in.shape[0]
    xf = x.reshape(m, d)
    for i in range(depth):
        c_dt = SSD_D_INNER + SSD_CONV_DIM
        c_r = c_dt + SSD_HEADS
        c_wlo = c_r + 3 * d
        c_glo = c_wlo + W_RANK + A_RANK
        c_gate = c_glo + G_RANK
        w_in_t = jnp.swapaxes(w_in[i], 0, 1)
        w_head = w_in_t[:c_r].astype(BF16)
        g2p = jnp.pad(rwkv_g2[i], ((0, G_RANK_PAD - G_RANK), (0, 0))).astype(BF16)
        whole = lambda w: (w, 0, w.shape[0])

        h = _rmsnorm(xf, norm_mix_g[i], BF16)
        proj = functools.partial(_mm, h, w_nt=True, out_dtype=F32, **TILE_WIDE)
        zx, w_rkv = proj(w_head, name="proj_zx", n=c_dt, side=(w_in_t, c_r, 3 * d))
        rkv, w_tail = proj(w_rkv, name="proj_rkv", n=3 * d, side=(w_in_t, c_wlo, w_in.shape[2] - c_wlo))
        gates, w_bssd_b = proj(w_tail, name="proj_gates", w_row0=c_gate - c_wlo, n=2 * d, side=whole(w_branch_ssd[i]))
        small = _proj_small(h, w_head, w_tail, dt_row0=c_dt, lo_row0=0).reshape(bsz, t, -1)
        zx = zx.reshape(bsz, t, -1)
        rkv = rkv.reshape(bsz, t, -1)

        u_ssd, w_ff1_b = _ssd(zx, small, ssd_conv_w[i], ssd_conv_b[i], ssd_dt_bias[i], ssd_a_log[i], ssd_d[i],
                              ssd_norm_g[i], whole(w_ff1[i]))
        u_rwkv, w_ff2_b = _wkv(rkv, small, rwkv_mu[i], rwkv_w2[i].astype(BF16), rwkv_a2[i].astype(BF16), g2p,
                               rwkv_w0[i], rwkv_a0[i], rwkv_k_k[i], rwkv_k_a[i], rwkv_r_k[i], rwkv_ln_g[i],
                               rwkv_ln_b[i], whole(w_ff2[i]))

        part, w_brwkv_b = _mm(u_ssd.reshape(m, SSD_D_INNER), w_bssd_b, name="branch_ssd", out_dtype=F32, **TILE_K2D,
                              mul=gates, mul_act="sigmoid", mul_col0=0, side=whole(w_branch_rwkv[i]))
        merged, w_out_b = _mm(u_rwkv.reshape(m, d), w_brwkv_b, name="branch_rwkv", out_dtype=BF16, **TILE_EPI,
                              mul=gates, mul_act="sigmoid", mul_col0=d, res=part, side=whole(w_out[i]))
        xf, xg, ss = _mm(merged, w_out_b, name="out_proj", out_dtype=F32, **TILE_EPI, res=xf,
                         next_norm_g=norm_ffn_g[i])

        ff, w_ple_b = _mm(xg, w_ff1_b, name="ffn_up", out_dtype=BF16, **TILE_WIDE, act="relu2", row_ss=ss,
                          side=whole(w_ple_gate[i]))
        xf, xg, ss = _mm(ff, w_ff2_b, name="ffn_down", out_dtype=F32, **TILE_K4D, res=xf,
                         next_norm_g=norm_ple_g[i])

        e = _ple_embed(p[i].reshape(m, PLE_DIM).astype(BF16), w_ple_proj[i].astype(BF16), ple_post_g[i])
        xf = _mm(xg, w_ple_b, name="ple_gate", out_dtype=F32, **TILE_EPI, act="sigmoid", mul=e, res=xf,
                 row_ss=ss)
    return _rmsnorm(xf, final_norm_g, F32).reshape(bsz, t, d)
```

```python
import functools
import math

import jax
import jax.numpy as jnp
from jax import lax
from jax.experimental import pallas as pl
from jax.experimental.pallas import tpu as pltpu

F32 = jnp.float32
BF16 = jnp.bfloat16

CHUNK = 64
LOG2_CHUNK = 6
SSD_D_INNER = 8192
SSD_HEAD_DIM = 64
SSD_HEADS = 128
SSD_GROUPS = 8
SSD_HPG = 16
SSD_STATE = 128
SSD_CONV = 4
SSD_GROUP_COLS = SSD_D_INNER // SSD_GROUPS
SSD_CONV_DIM = SSD_D_INNER + 2 * SSD_GROUPS * SSD_STATE
SSD_NORM_EPS = 1e-5
RWKV_HEAD_DIM = 64
W_RANK = 128
A_RANK = 128
G_RANK = 480
G_RANK_PAD = 512
DECAY_SCALE = math.exp(-0.5)
RWKV_GN_EPS = RWKV_HEAD_DIM * 1e-5
L2_EPS = 1e-12
LOG2E = math.log2(math.e)
NORM_EPS = 1e-6
PLE_DIM = 256

SMALL_COLS = G_RANK_PAD + SSD_HEADS + W_RANK + A_RANK
SMALL_LO_COLS = W_RANK + A_RANK + G_RANK_PAD

LANES = 128
SUBLANES = 8
BF16_ROWS = 16
VMEM_BYTES = 64 * 1024 * 1024
VMEM_LIMIT_BYTES = VMEM_BYTES - 8 * 1024 * 1024

SMALL_DT_BLK = G_RANK_PAD // LANES
SMALL_WLO_BLK = SMALL_DT_BLK + 1
SMALL_ALO_BLK = SMALL_DT_BLK + 2

TILE_WIDE = dict(tm=1024, tn=1024)
TILE_EPI = dict(tm=1024, tn=512)
TILE_K2D = dict(tm=1024, tn=256)
TILE_K4D = dict(tm=512, tn=256)


def _params(*sem):
    return pltpu.CompilerParams(dimension_semantics=sem, vmem_limit_bytes=VMEM_LIMIT_BYTES)


def _split3(x):
    hi = x.astype(BF16)
    r1 = x - hi.astype(F32)
    mid = r1.astype(BF16)
    lo = (r1 - mid.astype(F32)).astype(BF16)
    return hi, mid, lo


def _dot(a, b, dims=(((1,), (0,)), ((), ()))):
    return lax.dot_general(a.astype(BF16), b.astype(BF16), dims, preferred_element_type=F32)


def _dot_nt(a, b):
    return _dot(a, b, (((1,), (1,)), ((), ())))


def _dot_tn(a, b):
    return _dot(a, b, (((0,), (0,)), ((), ())))


def _cumsum_rows(tri_bf16, x):
    x0, x1, x2 = _split3(x)
    d = lambda q: jnp.dot(tri_bf16, q, preferred_element_type=F32)
    return d(x0) + d(x1) + d(x2)


def _sigmoid(x):
    return 0.5 * jnp.tanh(0.5 * x) + 0.5


def _silu(x):
    h = 0.5 * x
    return h + h * jnp.tanh(h)


def _softplus(x):
    return jnp.maximum(x, 0.0) + jnp.log1p(jnp.exp(-jnp.abs(x)))


def _stack_heads(x, top):
    return jnp.concatenate([jnp.where(top, x, 0.0), jnp.where(top, 0.0, x)], axis=0)


def _head_sum(x, top):
    s0 = jnp.sum(jnp.where(top, x, 0.0), axis=-1, keepdims=True)
    s1 = jnp.sum(jnp.where(top, 0.0, x), axis=-1, keepdims=True)
    return jnp.where(top, s0, s1)


def _side_specs(side, step_of, n_steps):
    src, row0, nrows = side
    cols = src.shape[1]
    assert src.ndim == 2 and nrows % BF16_ROWS == 0 and row0 % SUBLANES == 0
    rs = next(r for r in range(BF16_ROWS, nrows + 1, BF16_ROWS) if nrows % r == 0 and nrows // r <= n_steps)
    nslabs = nrows // rs
    slab = lambda *g: jnp.minimum(step_of(*g), nslabs - 1)
    in_spec = pl.BlockSpec((pl.Element(rs), pl.Element(cols)),
                           lambda *g: (pl.multiple_of(row0 + slab(*g) * rs, SUBLANES), 0))
    out_spec = pl.BlockSpec((rs, cols), lambda *g: (slab(*g), 0))
    return in_spec, out_spec, jax.ShapeDtypeStruct((nrows, cols), BF16)


def _rmsnorm_kernel(x_ref, g_ref, o_ref, *, eps):
    x = x_ref[...]
    y = x * lax.rsqrt(jnp.mean(x * x, axis=-1, keepdims=True) + eps)
    o_ref[...] = (y * g_ref[...]).astype(o_ref.dtype)


def _rmsnorm(x, g, out_dtype, tm=256):
    m, d = x.shape
    return pl.pallas_call(
        functools.partial(_rmsnorm_kernel, eps=NORM_EPS),
        grid=(m // tm,),
        in_specs=[pl.BlockSpec((tm, d), lambda i: (i, 0)), pl.BlockSpec((1, d), lambda i: (0, 0))],
        out_specs=pl.BlockSpec((tm, d), lambda i: (i, 0)),
        out_shape=jax.ShapeDtypeStruct((m, d), out_dtype),
        name="rmsnorm",
        compiler_params=_params("parallel"),
    )(x, g.reshape(1, d))


def _act(x, kind):
    if kind is None:
        return x
    if kind == "relu2":
        r = jnp.maximum(x, 0.0)
        return r * r
    if kind == "sigmoid":
        return _sigmoid(x)
    raise ValueError(kind)


def _mm_kernel(*refs, nk, w_nt, act, has_mul, mul_act, has_res, has_scale, has_norm, has_side, norm_dim):
    refs = list(refs)
    x_ref, w_ref = refs[0], refs[1]
    pos = 2

    def take(flag):
        nonlocal pos
        if not flag:
            return None
        pos += 1
        return refs[pos - 1]

    mul_ref, res_ref, ss_in_ref, g_ref, side_in_ref = (take(f) for f in (has_mul, has_res, has_scale, has_norm, has_side))
    o_ref = take(True)
    xg_ref, ss_out_ref, side_out_ref = take(has_norm), take(has_norm), take(has_side)
    if has_side:
        side_out_ref[...] = side_in_ref[...].astype(BF16)
    acc_ref = None if nk == 1 else (o_ref if o_ref.dtype == F32 else refs[pos])

    if w_nt:
        part = lax.dot_general(x_ref[...], w_ref[...], (((1,), (1,)), ((), ())), preferred_element_type=F32)
    else:
        part = jnp.dot(x_ref[...], w_ref[...], preferred_element_type=F32)

    def finish(acc):
        if has_scale:
            acc = acc * lax.rsqrt(ss_in_ref[:, 0:1] * (1.0 / norm_dim) + NORM_EPS)
        out = _act(acc, act)
        if has_mul:
            out = out * _act(mul_ref[...].astype(F32), mul_act)
        if has_res:
            out = res_ref[...].astype(F32) + out
        o_ref[...] = out.astype(o_ref.dtype)
        if has_norm:
            xg_ref[...] = (out * g_ref[...]).astype(BF16)
            row_ss = jnp.broadcast_to(jnp.sum(out * out, axis=1, keepdims=True), ss_out_ref.shape)
            first = pl.program_id(1) == 0

            @pl.when(first)
            def _():
                ss_out_ref[...] = row_ss

            @pl.when(jnp.logical_not(first))
            def _():
                ss_out_ref[...] += row_ss

    if nk == 1:
        finish(part)
    else:
        k = pl.program_id(2)

        @pl.when(k == 0)
        def _():
            acc_ref[...] = part

        @pl.when((k > 0) & (k < nk - 1))
        def _():
            acc_ref[...] += part

        @pl.when(k == nk - 1)
        def _():
            finish(acc_ref[...] + part)


def _mm(x, w, *, name, out_dtype, tm, tn, tk=None, w_nt=False, w_row0=0, n=None, act=None, mul=None, mul_act=None,
        mul_col0=0, res=None, res_col0=0, side=None, row_ss=None, next_norm_g=None):
    m, kdim = x.shape
    n = w.shape[1] if n is None else n
    tk = kdim if tk is None else tk
    tm, tn = min(tm, m), min(tn, n)
    nk = kdim // tk
    assert m % tm == 0 and n % tn == 0 and kdim % tk == 0 and mul_col0 % tn == 0 and res_col0 % tn == 0
    if w_nt:
        assert w.shape[1] == kdim and w_row0 % BF16_ROWS == 0 and tn % BF16_ROWS == 0 and w.dtype == BF16
        w_spec = pl.BlockSpec((pl.Element(tn), pl.Element(tk)),
                              lambda i, j, k: (pl.multiple_of(w_row0 + j * tn, BF16_ROWS), k * tk))
    else:
        assert w.shape[0] == kdim and w_row0 == 0
        w_spec = pl.BlockSpec((tk, tn), lambda i, j, k: (k, j))
    in_specs = [pl.BlockSpec((tm, tk), lambda i, j, k: (i, k)), w_spec]
    args = [x, w]
    if mul is not None:
        off = mul_col0 // tn
        in_specs.append(pl.BlockSpec((tm, tn), lambda i, j, k, off=off: (i, j + off)))
        args.append(mul)
    if res is not None:
        off = res_col0 // tn
        in_specs.append(pl.BlockSpec((tm, tn), lambda i, j, k, off=off: (i, j + off)))
        args.append(res)
    if row_ss is not None:
        in_specs.append(pl.BlockSpec((tm, LANES), lambda i, j, k: (i, 0)))
        args.append(row_ss)
    grid = (m // tm, n // tn, nk)
    out_specs = [pl.BlockSpec((tm, tn), lambda i, j, k: (i, j))]
    out_shape = [jax.ShapeDtypeStruct((m, n), out_dtype)]
    if next_norm_g is not None:
        assert nk == 1 and out_dtype == F32
        in_specs.append(pl.BlockSpec((1, tn), lambda i, j, k: (0, j)))
        args.append(next_norm_g.reshape(1, n))
        out_specs += [pl.BlockSpec((tm, tn), lambda i, j, k: (i, j)), pl.BlockSpec((tm, LANES), lambda i, j, k: (i, 0))]
        out_shape += [jax.ShapeDtypeStruct((m, n), BF16), jax.ShapeDtypeStruct((m, LANES), F32)]
    if side is not None:
        s_in, s_out, s_shape = _side_specs(side, lambda i, j, k: (i * grid[1] + j) * nk + k, grid[0] * grid[1] * nk)
        in_specs.append(s_in)
        args.append(side[0])
        out_specs.append(s_out)
        out_shape.append(s_shape)
    outs = pl.pallas_call(
        functools.partial(_mm_kernel, nk=nk, w_nt=w_nt, act=act, has_mul=mul is not None, mul_act=mul_act,
                          has_res=res is not None, has_scale=row_ss is not None, has_norm=next_norm_g is not None,
                          has_side=side is not None, norm_dim=kdim),
        grid=grid,
        in_specs=in_specs,
        out_specs=out_specs,
        out_shape=out_shape,
        scratch_shapes=[pltpu.VMEM((tm, tn), F32)] if nk > 1 and out_dtype != F32 else [],
        name=name,
        compiler_params=_params(*(("parallel", "parallel") if len(out_shape) == 1 else ("arbitrary", "arbitrary")),
                                "arbitrary"),
    )(*args)
    return outs[0] if len(outs) == 1 else tuple(outs)


def _proj_small_kernel(x_ref, wdt_ref, wlo_ref, o_ref):
    x = x_ref[...]
    nt = (((1,), (1,)), ((), ()))
    dt = lax.dot_general(x, wdt_ref[...], nt, preferred_element_type=F32)
    lo = lax.dot_general(x, wlo_ref[...], nt, preferred_element_type=F32)
    o_ref[:, 0:G_RANK_PAD] = lo[:, W_RANK + A_RANK:]
    o_ref[:, G_RANK_PAD:G_RANK_PAD + SSD_HEADS] = dt
    o_ref[:, G_RANK_PAD + SSD_HEADS:] = lo[:, :W_RANK + A_RANK]


def _proj_small(x, wt_dt, wt_lo, *, dt_row0, lo_row0, tm=512):
    m, kdim = x.shape
    win = lambda rows, row0: pl.BlockSpec((pl.Element(rows), pl.Element(kdim)), lambda i: (row0, 0))
    return pl.pallas_call(
        _proj_small_kernel,
        grid=(m // tm,),
        in_specs=[pl.BlockSpec((tm, kdim), lambda i: (i, 0)), win(SSD_HEADS, dt_row0), win(SMALL_LO_COLS, lo_row0)],
        out_specs=pl.BlockSpec((tm, SMALL_COLS), lambda i: (i, 0)),
        out_shape=jax.ShapeDtypeStruct((m, SMALL_COLS), F32),
        name="proj_small",
        compiler_params=_params("parallel"),
    )(x, wt_dt, wt_lo)


def _ple_embed_kernel(p_ref, w_ref, g_ref, o_ref):
    e = jnp.dot(p_ref[...], w_ref[...], preferred_element_type=F32)
    y = e * lax.rsqrt(jnp.mean(e * e, axis=-1, keepdims=True) + NORM_EPS)
    o_ref[...] = (y * g_ref[...]).astype(o_ref.dtype)


def _ple_embed(p, w, g, tm=256):
    m, kdim = p.shape
    n = w.shape[1]
    return pl.pallas_call(
        _ple_embed_kernel,
        grid=(m // tm,),
        in_specs=[pl.BlockSpec((tm, kdim), lambda i: (i, 0)), pl.BlockSpec((kdim, n), lambda i: (0, 0)),
                  pl.BlockSpec((1, n), lambda i: (0, 0))],
        out_specs=pl.BlockSpec((tm, n), lambda i: (i, 0)),
        out_shape=jax.ShapeDtypeStruct((m, n), F32),
        name="ple_embed",
        compiler_params=_params("parallel"),
    )(p, w, g.reshape(1, n))


def _ssd_kernel(xs_ref, bm_ref, cm_ref, dt_ref, z_ref, wxs_ref, wbm_ref, wcm_ref, bxs_ref, bbm_ref, bcm_ref,
                dtb_ref, alog_ref, d_ref, ng_ref, side_in_ref, o_ref, side_out_ref,
                st_ref, hxs_ref, hbm_ref, hcm_ref, axs_ref, abm_ref, acm_ref, *, n_chunks):
    tb = n_chunks * CHUNK
    hist = SUBLANES
    side_out_ref[...] = side_in_ref[...].astype(BF16)

    @pl.when(pl.program_id(2) == 0)
    def _():
        st_ref[...] = jnp.zeros_like(st_ref)
        for h_ref in (hxs_ref, hbm_ref, hcm_ref):
            h_ref[0:hist, :] = jnp.zeros((hist, h_ref.shape[1]), F32)

    for raw_ref, h_ref, w_ref, b_ref, act_ref in ((xs_ref, hxs_ref, wxs_ref, bxs_ref, axs_ref),
                                                  (bm_ref, hbm_ref, wbm_ref, bbm_ref, abm_ref),
                                                  (cm_ref, hcm_ref, wcm_ref, bcm_ref, acm_ref)):
        h_ref[hist:hist + tb, :] = raw_ref[0]
        hv = h_ref[...]
        acc = b_ref[...] + w_ref[SSD_CONV - 1:SSD_CONV, :] * hv[hist:]
        for j in range(SSD_CONV - 1):
            acc = acc + w_ref[j:j + 1, :] * pltpu.roll(hv, SSD_CONV - 1 - j, 0)[hist:]
        act_ref[...] = _silu(acc)
        h_ref[0:hist, :] = raw_ref[0, tb - hist:tb, :]

    gc = SSD_GROUP_COLS
    g = pl.program_id(1)
    expand = (lax.broadcasted_iota(jnp.int32, (SSD_HEADS, gc), 0)
              == (lax.broadcasted_iota(jnp.int32, (SSD_HEADS, gc), 1) >> LOG2_CHUNK) + g * SSD_HPG).astype(BF16)
    li = lax.broadcasted_iota(jnp.int32, (CHUNK, gc), 0)
    si = lax.broadcasted_iota(jnp.int32, (CHUNK, gc), 1) & (CHUNK - 1)
    eye_t = li == si
    causal_t = li >= si
    top = lax.broadcasted_iota(jnp.int32, (CHUNK, LANES), 1) < SSD_HEAD_DIM
    d_row = d_ref[0]
    ng_row = ng_ref[...]
    trow = lax.broadcasted_iota(jnp.int32, (tb, tb), 0)
    tcol = lax.broadcasted_iota(jnp.int32, (tb, tb), 1)
    tri_blk = (((trow >> LOG2_CHUNK) == (tcol >> LOG2_CHUNK)) & (trow >= tcol)).astype(BF16)
    dt_all = _softplus(dt_ref[0] + dtb_ref[...])
    acs_all = _cumsum_rows(tri_blk, dt_all * (-LOG2E * jnp.exp(alog_ref[...])))

    def expand_heads(q):
        q0, q1, q2 = _split3(q)
        de = lambda r: jnp.dot(r, expand, preferred_element_type=F32)
        return de(q0) + de(q1) + de(q2)

    st = st_ref[...]
    for c in range(n_chunks):
        sl = slice(c * CHUNK, (c + 1) * CHUNK)
        xs = axs_ref[sl, :]
        bm = abm_ref[sl, :]
        cm = acm_ref[sl, :]
        dt_exp = expand_heads(dt_all[sl, :])
        acs = expand_heads(acs_all[sl, :])
        rowpart = jnp.sum(jnp.where(eye_t, acs, 0.0), axis=0, keepdims=True)
        decay = jnp.exp2(jnp.where(causal_t, acs - rowpart, -jnp.inf))
        xdt = xs * dt_exp
        cb2 = _dot_nt(cm, jnp.concatenate([bm, bm], axis=0))
        last = acs[CHUNK - 1:CHUNK, :]
        y_off = _dot(cm, st) * jnp.exp2(acs)
        parts = []
        for j in range(gc // LANES):
            lanes = slice(j * LANES, (j + 1) * LANES)
            parts.append(_dot(cb2 * decay[:, lanes], _stack_heads(xdt[:, lanes], top)))
        y = jnp.concatenate(parts, axis=1) + y_off + d_row * xs
        st = st * jnp.exp2(last) + _dot_tn(bm, xdt * jnp.exp2(last - acs))
        y = y * _silu(z_ref[0, sl, :])
        y = y * lax.rsqrt(jnp.mean(y * y, axis=-1, keepdims=True) + SSD_NORM_EPS)
        o_ref[0, sl, :] = (y * ng_row).astype(o_ref.dtype)
    st_ref[...] = st


def _ssd(zx, small, conv_w, conv_b, dt_bias, a_log, d_skip, norm_g, side, *, tb=256):
    bsz, t, _ = zx.shape
    gc = SSD_GROUP_COLS
    xs_blk0 = SSD_D_INNER // gc
    bm_blk0 = 2 * SSD_D_INNER // SSD_STATE
    cm_blk0 = bm_blk0 + SSD_GROUPS
    wbm_blk0 = SSD_D_INNER // SSD_STATE
    wcm_blk0 = wbm_blk0 + SSD_GROUPS
    conv_b = conv_b.reshape(1, SSD_CONV_DIM)
    per_group = lambda v: jnp.repeat(v, SSD_HEAD_DIM).reshape(SSD_GROUPS, 1, gc)
    seq = lambda w, f: pl.BlockSpec((1, tb, w), lambda b, g, ti: (b, ti, f(g)))
    chan = lambda r, w, f: pl.BlockSpec((r, w), lambda b, g, ti: (0, f(g)))
    grp = pl.BlockSpec((1, 1, gc), lambda b, g, ti: (g, 0, 0))
    nt_steps = t // tb
    s_in, s_out, s_shape = _side_specs(side, lambda b, g, ti: (b * SSD_GROUPS + g) * nt_steps + ti,
                                       bsz * SSD_GROUPS * nt_steps)
    return pl.pallas_call(
        functools.partial(_ssd_kernel, n_chunks=tb // CHUNK),
        grid=(bsz, SSD_GROUPS, t // tb),
        in_specs=[seq(gc, lambda g: xs_blk0 + g), seq(SSD_STATE, lambda g: bm_blk0 + g),
                  seq(SSD_STATE, lambda g: cm_blk0 + g), seq(SSD_HEADS, lambda g: SMALL_DT_BLK),
                  seq(gc, lambda g: g),
                  chan(SSD_CONV, gc, lambda g: g), chan(SSD_CONV, SSD_STATE, lambda g: wbm_blk0 + g),
                  chan(SSD_CONV, SSD_STATE, lambda g: wcm_blk0 + g),
                  chan(1, gc, lambda g: g), chan(1, SSD_STATE, lambda g: wbm_blk0 + g),
                  chan(1, SSD_STATE, lambda g: wcm_blk0 + g),
                  chan(1, SSD_HEADS, lambda g: 0), chan(1, SSD_HEADS, lambda g: 0), grp,
                  chan(1, gc, lambda g: g), s_in],
        out_specs=[seq(gc, lambda g: g), s_out],
        out_shape=[jax.ShapeDtypeStruct((bsz, t, SSD_D_INNER), BF16), s_shape],
        scratch_shapes=[pltpu.VMEM((SSD_STATE, gc), F32),
                        pltpu.VMEM((SUBLANES + tb, gc), F32), pltpu.VMEM((SUBLANES + tb, SSD_STATE), F32),
                        pltpu.VMEM((SUBLANES + tb, SSD_STATE), F32),
                        pltpu.VMEM((tb, gc), F32), pltpu.VMEM((tb, SSD_STATE), F32), pltpu.VMEM((tb, SSD_STATE), F32)],
        name="ssd_scan",
        compiler_params=_params("arbitrary", "arbitrary", "arbitrary"),
    )(zx, zx, zx, small, zx, conv_w, conv_w, conv_w, conv_b, conv_b, conv_b, dt_bias.reshape(1, SSD_HEADS),
      a_log.reshape(1, SSD_HEADS), per_group(d_skip), norm_g.reshape(1, SSD_D_INNER), side[0])


def _wkv_kernel(r_ref, k_ref, v_ref, glo_ref, wlo_ref, alo_ref,
                mur_ref, muk_ref, muv_ref, mug_ref, muw_ref, mua_ref,
                w2_ref, a2_ref, g2_ref, w0_ref, a0_ref, kk_ref, ka_ref, rk_ref, lng_ref, lnb_ref, side_in_ref,
                o_ref, side_out_ref,
                s_ref, prev_ref, plo_ref, rr_ref, kr_ref, vv_ref, lw_ref, kn_ref, ag_ref, gg_ref,
                *, n_chunks, n_pairs):
    tb = n_chunks * CHUNK
    glo_w = glo_ref.shape[2]
    side_out_ref[...] = side_in_ref[...].astype(BF16)

    @pl.when(pl.program_id(2) == 0)
    def _():
        s_ref[...] = jnp.zeros_like(s_ref)
        prev_ref[...] = jnp.zeros_like(prev_ref)
        plo_ref[...] = jnp.zeros_like(plo_ref)

    first_row = lax.broadcasted_iota(jnp.int32, (tb, 1), 0) == 0

    def lerp(x, prev_row, mu):
        shifted = jnp.where(first_row, prev_row, pltpu.roll(x, 1, 0))
        return x + (shifted - x) * mu

    r_raw, k_raw, v_raw = r_ref[0], k_ref[0], v_ref[0]
    glo_raw, wlo_raw, alo_raw = glo_ref[0], wlo_ref[0], alo_ref[0]
    r = lerp(r_raw, prev_ref[0:1, :], mur_ref[...])
    k = lerp(k_raw, prev_ref[1:2, :], muk_ref[...])
    v = lerp(v_raw, prev_ref[2:3, :], muv_ref[...])
    g_lo = lerp(glo_raw, plo_ref[0:1, 0:glo_w], mug_ref[...])
    w_lo = lerp(wlo_raw, plo_ref[0:1, glo_w:glo_w + W_RANK], muw_ref[...])
    a_lo = lerp(alo_raw, plo_ref[0:1, glo_w + W_RANK:], mua_ref[...])
    prev_ref[0:1, :] = r_raw[tb - 1:tb, :]
    prev_ref[1:2, :] = k_raw[tb - 1:tb, :]
    prev_ref[2:3, :] = v_raw[tb - 1:tb, :]
    plo_ref[0:1, 0:glo_w] = glo_raw[tb - 1:tb, :]
    plo_ref[0:1, glo_w:glo_w + W_RANK] = wlo_raw[tb - 1:tb, :]
    plo_ref[0:1, glo_w + W_RANK:] = alo_raw[tb - 1:tb, :]

    lw = (-DECAY_SCALE * LOG2E) * _sigmoid(w0_ref[...] + jnp.dot(jnp.tanh(w_lo).astype(BF16), w2_ref[...],
                                                                 preferred_element_type=F32))
    a_gate = _sigmoid(a0_ref[...] + jnp.dot(a_lo.astype(BF16), a2_ref[...], preferred_element_type=F32))
    gg_ref[...] = jnp.dot(_sigmoid(g_lo).astype(BF16), g2_ref[...], preferred_element_type=F32)
    top_blk = lax.broadcasted_iota(jnp.int32, (tb, LANES), 1) < RWKV_HEAD_DIM
    kk = k * kk_ref[...]
    for j in range(n_pairs):
        lanes = slice(j * LANES, (j + 1) * LANES)
        kj = kk[:, lanes]
        kn_ref[:, lanes] = kj * lax.rsqrt(jnp.maximum(_head_sum(kj * kj, top_blk), L2_EPS * L2_EPS))
    rr_ref[...] = r
    kr_ref[...] = k * (1.0 + (a_gate - 1.0) * ka_ref[...])
    vv_ref[...] = v
    lw_ref[...] = lw
    ag_ref[...] = a_gate

    row = lax.broadcasted_iota(jnp.int32, (LANES, LANES), 0)
    col = lax.broadcasted_iota(jnp.int32, (LANES, LANES), 1)
    same = (row >> LOG2_CHUNK) == (col >> LOG2_CHUNK)
    trow = lax.broadcasted_iota(jnp.int32, (tb, tb), 0)
    tcol = lax.broadcasted_iota(jnp.int32, (tb, tb), 1)
    tri_blk = ((trow >> LOG2_CHUNK) == (tcol >> LOG2_CHUNK)) & (trow >= tcol)
    top = lax.broadcasted_iota(jnp.int32, (CHUNK, LANES), 1) < RWKV_HEAD_DIM
    inv_n = 1.0 / RWKV_HEAD_DIM
    chains = [(j, c) for j in range(n_pairs) for c in range(n_chunks)]

    def piece(ref, j, c):
        return ref[c * CHUNK:(c + 1) * CHUNK, j * LANES:(j + 1) * LANES]

    cum_all = _cumsum_rows(tri_blk.astype(BF16), lw)

    lane_t = lax.broadcasted_iota(jnp.int32, (CHUNK, LANES), 1) & (CHUNK - 1)
    row_t = lax.broadcasted_iota(jnp.int32, (CHUNK, LANES), 0)
    strict_c = row_t > lane_t
    incl_c = row_t >= lane_t
    eye_c = (row_t == lane_t).astype(F32)
    strict2 = jnp.concatenate([strict_c, incl_c], axis=0)

    def bd(x):
        return _stack_heads(x.astype(BF16), top)

    nb = range(len(chains))
    a_c, r_c, b_bd, k_bd, v_c, bh_c, kh_c, wtot = [], [], [], [], [], [], [], []
    for j, c in chains:
        cum = cum_all[c * CHUNK:(c + 1) * CHUNK, j * LANES:(j + 1) * LANES]
        lwc = piece(lw_ref, j, c)
        kc = piece(kr_ref, j, c)
        knc = piece(kn_ref, j, c)
        b = knc * piece(ag_ref, j, c)
        tot = cum[CHUNK - 1:CHUNK, :]
        e_neg = jnp.exp2(-cum)
        e_end = jnp.exp2(tot - cum)
        a_c.append((-knc * jnp.exp2(cum - lwc)).astype(BF16))
        r_c.append(piece(rr_ref, j, c) * jnp.exp2(cum))
        b_bd.append(bd(b * e_neg))
        k_bd.append(bd(kc * e_neg))
        v_c.append(piece(vv_ref, j, c).astype(BF16))
        bh_c.append((b * e_end).astype(BF16))
        kh_c.append((kc * e_end).astype(BF16))
        wtot.append(jnp.exp2(tot))
    ar = [jnp.concatenate([a_c[i], r_c[i].astype(BF16)], axis=0) for i in nb]
    sb = [jnp.where(strict2, _dot_nt(ar[i], b_bd[i]), 0.0) for i in nb]
    sk = [jnp.where(strict2, _dot_nt(ar[i], k_bd[i]), 0.0) for i in nb]
    n = [x[:CHUNK] for x in sb]
    rb = [x[CHUNK:].astype(BF16) for x in sb]
    v_bd = [bd(x) for x in v_c]
    tinv = [eye_c + x for x in n]
    p_bd = [bd(x) for x in n]
    npow = [_dot(n[i], p_bd[i]) for i in nb]
    for it in range(1, 6):
        p_bd = [bd(x) for x in npow]
        if it < 5:
            both = [_dot(jnp.concatenate([npow[i], tinv[i]], axis=0), p_bd[i]) for i in nb]
            npow = [x[:CHUNK] for x in both]
            tinv = [tinv[i] + both[i][CHUNK:] for i in nb]
        else:
            tinv = [tinv[i] + _dot(tinv[i], p_bd[i]) for i in nb]
    akv = [_dot(sk[i][:CHUNK], v_bd[i]) for i in nb]
    pq = [_dot(tinv[i], jnp.concatenate([bd(a_c[i]), bd(akv[i])], axis=1)) for i in nb]
    pq_bd = [jnp.concatenate([bd(x[:, :LANES]), bd(x[:, LANES:])], axis=1) for x in pq]
    rpq = [_dot(rb[i], pq_bd[i]) for i in nb]
    r_eff = [(r_c[i] + rpq[i][:, :LANES]).astype(BF16) for i in nb]
    y0 = [rpq[i][:, LANES:] + _dot(sk[i][CHUNK:], v_bd[i]) for i in nb]
    g_eff = [jnp.where(same, _dot_tn(pq[i][:, :LANES], bh_c[i]), 0.0).astype(BF16) for i in nb]
    s1 = [jnp.where(same, _dot_tn(jnp.concatenate([pq[i][:, LANES:].astype(BF16), v_c[i]], axis=0),
                                  jnp.concatenate([bh_c[i], kh_c[i]], axis=0)), 0.0) for i in nb]

    for j in range(n_pairs):
        lanes = slice(j * LANES, (j + 1) * LANES)
        rk_row = rk_ref[:, lanes]
        lng_row = lng_ref[:, lanes]
        lnb_row = lnb_ref[:, lanes]
        s = s_ref[j]
        for c in range(n_chunks):
            i = j * n_chunks + c
            y = _dot_nt(r_eff[i], s) + y0[i]
            s = s * wtot[i] + _dot(s, g_eff[i]) + s1[i]
            mean = _head_sum(y, top) * inv_n
            yc = y - mean
            var = _head_sum(yc * yc, top) * inv_n
            yn = yc * lax.rsqrt(var + RWKV_GN_EPS) * lng_row + lnb_row
            bonus = _head_sum(piece(rr_ref, j, c) * piece(kr_ref, j, c) * rk_row, top) * piece(vv_ref, j, c)
            o_ref[0, c * CHUNK:(c + 1) * CHUNK, lanes] = ((yn + bonus) * piece(gg_ref, j, c)).astype(o_ref.dtype)
        s_ref[j] = s


def _wkv(rkv, small, mu, w2, a2, g2p, w0, a0, k_k, k_a, r_k, ln_g, ln_b, side, *, tb=64, pairs=32):
    bsz, t, d3 = rkv.shape
    d = d3 // 3
    wl = pairs * LANES
    nh = d // wl
    mu_rkv = mu[:3 * d].reshape(1, 3 * d)
    mu_w = mu[3 * d:3 * d + W_RANK].reshape(1, W_RANK)
    mu_a = mu[3 * d + W_RANK:3 * d + W_RANK + A_RANK].reshape(1, A_RANK)
    mu_g = jnp.pad(mu[3 * d + W_RANK + A_RANK:], (0, G_RANK_PAD - G_RANK)).reshape(1, G_RANK_PAD)
    seq = lambda w, f: pl.BlockSpec((1, tb, w), lambda b, h, ti: (b, ti, f(h)))
    chan = lambda r, w, f: pl.BlockSpec((r, w), lambda b, h, ti: (0, f(h)))
    vec = lambda q: q.reshape(1, d)
    blk = pltpu.VMEM((tb, wl), F32)
    nt_steps = t // tb
    s_in, s_out, s_shape = _side_specs(side, lambda b, h, ti: (b * nh + h) * nt_steps + ti, bsz * nh * nt_steps)
    return pl.pallas_call(
        functools.partial(_wkv_kernel, n_chunks=tb // CHUNK, n_pairs=pairs),
        grid=(bsz, nh, t // tb),
        in_specs=[seq(wl, lambda h: h), seq(wl, lambda h: nh + h), seq(wl, lambda h: 2 * nh + h),
                  seq(G_RANK_PAD, lambda h: 0), seq(W_RANK, lambda h: SMALL_WLO_BLK),
                  seq(A_RANK, lambda h: SMALL_ALO_BLK),
                  chan(1, wl, lambda h: h), chan(1, wl, lambda h: nh + h), chan(1, wl, lambda h: 2 * nh + h),
                  chan(1, G_RANK_PAD, lambda h: 0), chan(1, W_RANK, lambda h: 0), chan(1, A_RANK, lambda h: 0),
                  chan(W_RANK, wl, lambda h: h), chan(A_RANK, wl, lambda h: h), chan(G_RANK_PAD, wl, lambda h: h)]
                 + [chan(1, wl, lambda h: h)] * 7 + [s_in],
        out_specs=[seq(wl, lambda h: h), s_out],
        out_shape=[jax.ShapeDtypeStruct((bsz, t, d), BF16), s_shape],
        scratch_shapes=[pltpu.VMEM((pairs, LANES, LANES), F32), pltpu.VMEM((SUBLANES, wl), F32),
                        pltpu.VMEM((SUBLANES, G_RANK_PAD + W_RANK + A_RANK), F32)] + [blk] * 7,
        name="wkv7_scan",
        compiler_params=_params("arbitrary", "arbitrary", "arbitrary"),
    )(rkv, rkv, rkv, small, small, small, mu_rkv, mu_rkv, mu_rkv, mu_g, mu_w, mu_a,
      w2, a2, g2p, vec(w0), vec(a0), vec(k_k), vec(k_a), vec(r_k), vec(ln_g), vec(ln_b), side[0])


def kernel(x, p, norm_mix_g, w_in, ssd_conv_w, ssd_conv_b, ssd_dt_bias, ssd_a_log, ssd_d, ssd_norm_g, rwkv_mu, rwkv_w0, rwkv_w2, rwkv_a0, rwkv_a2, rwkv_g2, rwkv_k_k, rwkv_k_a, rwkv_r_k, rwkv_ln_g, rwkv_ln_b, w_branch_ssd, w_branch_rwkv, w_out, norm_ffn_g, w_ff1, w_ff2, norm_ple_g, w_ple_gate, w_ple_proj, ple_post_g, final_norm_g):
    bsz, t, d = x.shape
    m = bsz * t
    depth = w_in.shape[0]
    xf = x.reshape(m, d)
    for i in range(depth):
        c_dt = SSD_D_INNER + SSD_CONV_DIM
        c_r = c_dt + SSD_HEADS
        c_wlo = c_r + 3 * d
        c_glo = c_wlo + W_RANK + A_RANK
        c_gate = c_glo + G_RANK
        w_in_t = jnp.swapaxes(w_in[i], 0, 1)
        w_head = w_in_t[:c_r].astype(BF16)
        g2p = jnp.pad(rwkv_g2[i], ((0, G_RANK_PAD - G_RANK), (0, 0))).astype(BF16)
        whole = lambda w: (w, 0, w.shape[0])

        h = _rmsnorm(xf, norm_mix_g[i], BF16)
        proj = functools.partial(_mm, h, w_nt=True, out_dtype=F32, **TILE_WIDE)
        zx, w_rkv = proj(w_head, name="proj_zx", n=c_dt, side=(w_in_t, c_r, 3 * d))
        rkv, w_tail = proj(w_rkv, name="proj_rkv", n=3 * d, side=(w_in_t, c_wlo, w_in.shape[2] - c_wlo))
        gates, w_bssd_b = proj(w_tail, name="proj_gates", w_row0=c_gate - c_wlo, n=2 * d, side=whole(w_branch_ssd[i]))
        small = _proj_small(h, w_head, w_tail, dt_row0=c_dt, lo_row0=0).reshape(bsz, t, -1)
        zx = zx.reshape(bsz, t, -1)
        rkv = rkv.reshape(bsz, t, -1)

        u_ssd, w_ff1_b = _ssd(zx, small, ssd_conv_w[i], ssd_conv_b[i], ssd_dt_bias[i], ssd_a_log[i], ssd_d[i],
                              ssd_norm_g[i], whole(w_ff1[i]))
        u_rwkv, w_ff2_b = _wkv(rkv, small, rwkv_mu[i], rwkv_w2[i].astype(BF16), rwkv_a2[i].astype(BF16), g2p,
                               rwkv_w0[i], rwkv_a0[i], rwkv_k_k[i], rwkv_k_a[i], rwkv_r_k[i], rwkv_ln_g[i],
                               rwkv_ln_b[i], whole(w_ff2[i]))

        part, w_brwkv_b = _mm(u_ssd.reshape(m, SSD_D_INNER), w_bssd_b, name="branch_ssd", out_dtype=F32, **TILE_K2D,
                              mul=gates, mul_act="sigmoid", mul_col0=0, side=whole(w_branch_rwkv[i]))
        merged, w_out_b = _mm(u_rwkv.reshape(m, d), w_brwkv_b, name="branch_rwkv", out_dtype=BF16, **TILE_EPI,
                              mul=gates, mul_act="sigmoid", mul_col0=d, res=part, side=whole(w_out[i]))
        xf, xg, ss = _mm(merged, w_out_b, name="out_proj", out_dtype=F32, **TILE_EPI, res=xf,
                         next_norm_g=norm_ffn_g[i])

        ff, w_ple_b = _mm(xg, w_ff1_b, name="ffn_up", out_dtype=BF16, **TILE_WIDE, act="relu2", row_ss=ss,
                          side=whole(w_ple_gate[i]))
        xf, xg, ss = _mm(ff, w_ff2_b, name="ffn_down", out_dtype=F32, **TILE_K4D, res=xf,
                         next_norm_g=norm_ple_g[i])

        e = _ple_embed(p[i].reshape(m, PLE_DIM).astype(BF16), w_ple_proj[i].astype(BF16), ple_post_g[i])
        xf = _mm(xg, w_ple_b, name="ple_gate", out_dtype=F32, **TILE_EPI, act="sigmoid", mul=e, res=xf,
                 row_ss=ss)
    return _rmsnorm(xf, final_norm_g, F32).reshape(bsz, t, d)
```

```python
import functools
import math

import jax
import jax.numpy as jnp
from jax import lax
from jax.experimental import pallas as pl
from jax.experimental.pallas import tpu as pltpu

F32 = jnp.float32
BF16 = jnp.bfloat16

CHUNK = 64
LOG2_CHUNK = 6
SSD_D_INNER = 8192
SSD_HEAD_DIM = 64
SSD_HEADS = 128
SSD_GROUPS = 8
SSD_HPG = 16
SSD_STATE = 128
SSD_CONV = 4
SSD_GROUP_COLS = SSD_D_INNER // SSD_GROUPS
SSD_CONV_DIM = SSD_D_INNER + 2 * SSD_GROUPS * SSD_STATE
SSD_NORM_EPS = 1e-5
RWKV_HEAD_DIM = 64
W_RANK = 128
A_RANK = 128
G_RANK = 480
G_RANK_PAD = 512
DECAY_SCALE = math.exp(-0.5)
RWKV_GN_EPS = RWKV_HEAD_DIM * 1e-5
L2_EPS = 1e-12
LOG2E = math.log2(math.e)
NORM_EPS = 1e-6
PLE_DIM = 256

SMALL_COLS = G_RANK_PAD + SSD_HEADS + W_RANK + A_RANK
SMALL_LO_COLS = W_RANK + A_RANK + G_RANK_PAD

LANES = 128
SUBLANES = 8
BF16_ROWS = 16
VMEM_BYTES = 64 * 1024 * 1024
VMEM_LIMIT_BYTES = VMEM_BYTES - 8 * 1024 * 1024

SMALL_DT_BLK = G_RANK_PAD // LANES
SMALL_WLO_BLK = SMALL_DT_BLK + 1
SMALL_ALO_BLK = SMALL_DT_BLK + 2

TILE_WIDE = dict(tm=1024, tn=1024)
TILE_EPI = dict(tm=1024, tn=512)
TILE_K2D = dict(tm=1024, tn=256)
TILE_K4D = dict(tm=512, tn=256)


def _params(*sem):
    return pltpu.CompilerParams(dimension_semantics=sem, vmem_limit_bytes=VMEM_LIMIT_BYTES)


def _split3(x):
    hi = x.astype(BF16)
    r1 = x - hi.astype(F32)
    mid = r1.astype(BF16)
    lo = (r1 - mid.astype(F32)).astype(BF16)
    return hi, mid, lo


def _dot(a, b, dims=(((1,), (0,)), ((), ()))):
    return lax.dot_general(a.astype(BF16), b.astype(BF16), dims, preferred_element_type=F32)


def _dot_nt(a, b):
    return _dot(a, b, (((1,), (1,)), ((), ())))


def _dot_tn(a, b):
    return _dot(a, b, (((0,), (0,)), ((), ())))


def _cumsum_rows(tri_bf16, x):
    x0, x1, x2 = _split3(x)
    d = lambda q: jnp.dot(tri_bf16, q, preferred_element_type=F32)
    return d(x0) + d(x1) + d(x2)


def _sigmoid(x):
    return 0.5 * jnp.tanh(0.5 * x) + 0.5


def _silu(x):
    h = 0.5 * x
    return h + h * jnp.tanh(h)


def _softplus(x):
    return jnp.maximum(x, 0.0) + jnp.log1p(jnp.exp(-jnp.abs(x)))


def _stack_heads(x, top):
    return jnp.concatenate([jnp.where(top, x, 0.0), jnp.where(top, 0.0, x)], axis=0)


def _head_sum(x, top):
    s0 = jnp.sum(jnp.where(top, x, 0.0), axis=-1, keepdims=True)
    s1 = jnp.sum(jnp.where(top, 0.0, x), axis=-1, keepdims=True)
    return jnp.where(top, s0, s1)


def _side_specs(side, step_of, n_steps):
    src, row0, nrows = side
    cols = src.shape[1]
    assert src.ndim == 2 and nrows % BF16_ROWS == 0 and row0 % SUBLANES == 0
    rs = next(r for r in range(BF16_ROWS, nrows + 1, BF16_ROWS) if nrows % r == 0 and nrows // r <= n_steps)
    nslabs = nrows // rs
    slab = lambda *g: jnp.minimum(step_of(*g), nslabs - 1)
    in_spec = pl.BlockSpec((pl.Element(rs), pl.Element(cols)),
                           lambda *g: (pl.multiple_of(row0 + slab(*g) * rs, SUBLANES), 0))
    out_spec = pl.BlockSpec((rs, cols), lambda *g: (slab(*g), 0))
    return in_spec, out_spec, jax.ShapeDtypeStruct((nrows, cols), BF16)


def _rmsnorm_kernel(x_ref, g_ref, o_ref, *, eps):
    x = x_ref[...]
    y = x * lax.rsqrt(jnp.mean(x * x, axis=-1, keepdims=True) + eps)
    o_ref[...] = (y * g_ref[...]).astype(o_ref.dtype)


def _rmsnorm(x, g, out_dtype, tm=256):
    m, d = x.shape
    return pl.pallas_call(
        functools.partial(_rmsnorm_kernel, eps=NORM_EPS),
        grid=(m // tm,),
        in_specs=[pl.BlockSpec((tm, d), lambda i: (i, 0)), pl.BlockSpec((1, d), lambda i: (0, 0))],
        out_specs=pl.BlockSpec((tm, d), lambda i: (i, 0)),
        out_shape=jax.ShapeDtypeStruct((m, d), out_dtype),
        name="rmsnorm",
        compiler_params=_params("parallel"),
    )(x, g.reshape(1, d))


def _act(x, kind):
    if kind is None:
        return x
    if kind == "relu2":
        r = jnp.maximum(x, 0.0)
        return r * r
    if kind == "sigmoid":
        return _sigmoid(x)
    raise ValueError(kind)


def _mm_kernel(*refs, nk, w_nt, act, has_mul, mul_act, has_res, has_scale, has_norm, has_side, has_lerp, norm_dim,
               seq_tiles):
    refs = list(refs)
    x_ref, w_ref = refs[0], refs[1]
    pos = 2

    def take(flag):
        nonlocal pos
        if not flag:
            return None
        pos += 1
        return refs[pos - 1]

    mul_ref, res_ref, ss_in_ref, mu_ref = take(has_mul), take(has_res), take(has_scale), take(has_lerp)
    g_ref, side_in_ref = take(has_norm), take(has_side)
    o_ref = take(True)
    xg_ref, ss_out_ref, side_out_ref = take(has_norm), take(has_norm), take(has_side)
    carry_ref = refs[-1] if has_lerp else None
    if has_side:
        side_out_ref[...] = side_in_ref[...].astype(BF16)
    acc_ref = None if nk == 1 else (o_ref if o_ref.dtype == F32 else refs[pos])

    if w_nt:
        part = lax.dot_general(x_ref[...], w_ref[...], (((1,), (1,)), ((), ())), preferred_element_type=F32)
    else:
        part = jnp.dot(x_ref[...], w_ref[...], preferred_element_type=F32)

    def finish(acc):
        if has_scale:
            acc = acc * lax.rsqrt(ss_in_ref[:, 0:1] * (1.0 / norm_dim) + NORM_EPS)
        if has_lerp:
            i, j = pl.program_id(0), pl.program_id(1)
            prev = jnp.where(i % seq_tiles == 0, 0.0, carry_ref[j])
            shifted = pltpu.roll(jnp.concatenate([prev, acc], axis=0), 1, 0)[SUBLANES:]
            carry_ref[j] = acc[acc.shape[0] - SUBLANES:]
            acc = acc + (shifted - acc) * mu_ref[...]
        out = _act(acc, act)
        if has_mul:
            out = out * _act(mul_ref[...].astype(F32), mul_act)
        if has_res:
            out = res_ref[...].astype(F32) + out
        o_ref[...] = out.astype(o_ref.dtype)
        if has_norm:
            xg_ref[...] = (out * g_ref[...]).astype(BF16)
            row_ss = jnp.broadcast_to(jnp.sum(out * out, axis=1, keepdims=True), ss_out_ref.shape)
            first = pl.program_id(1) == 0

            @pl.when(first)
            def _():
                ss_out_ref[...] = row_ss

            @pl.when(jnp.logical_not(first))
            def _():
                ss_out_ref[...] += row_ss

    if nk == 1:
        finish(part)
    else:
        k = pl.program_id(2)

        @pl.when(k == 0)
        def _():
            acc_ref[...] = part

        @pl.when((k > 0) & (k < nk - 1))
        def _():
            acc_ref[...] += part

        @pl.when(k == nk - 1)
        def _():
            finish(acc_ref[...] + part)


def _mm(x, w, *, name, out_dtype, tm, tn, tk=None, w_nt=False, w_row0=0, n=None, act=None, mul=None, mul_act=None,
        mul_col0=0, res=None, res_col0=0, side=None, row_ss=None, next_norm_g=None, lerp_mu=None, seq_rows=None):
    m, kdim = x.shape
    n = w.shape[1] if n is None else n
    tk = kdim if tk is None else tk
    tm, tn = min(tm, m), min(tn, n)
    nk = kdim // tk
    assert m % tm == 0 and n % tn == 0 and kdim % tk == 0 and mul_col0 % tn == 0 and res_col0 % tn == 0
    if w_nt:
        assert w.shape[1] == kdim and w_row0 % BF16_ROWS == 0 and tn % BF16_ROWS == 0 and w.dtype == BF16
        w_spec = pl.BlockSpec((pl.Element(tn), pl.Element(tk)),
                              lambda i, j, k: (pl.multiple_of(w_row0 + j * tn, BF16_ROWS), k * tk))
    else:
        assert w.shape[0] == kdim and w_row0 == 0
        w_spec = pl.BlockSpec((tk, tn), lambda i, j, k: (k, j))
    in_specs = [pl.BlockSpec((tm, tk), lambda i, j, k: (i, k)), w_spec]
    args = [x, w]
    if mul is not None:
        off = mul_col0 // tn
        in_specs.append(pl.BlockSpec((tm, tn), lambda i, j, k, off=off: (i, j + off)))
        args.append(mul)
    if res is not None:
        off = res_col0 // tn
        in_specs.append(pl.BlockSpec((tm, tn), lambda i, j, k, off=off: (i, j + off)))
        args.append(res)
    if row_ss is not None:
        in_specs.append(pl.BlockSpec((tm, LANES), lambda i, j, k: (i, 0)))
        args.append(row_ss)
    if lerp_mu is not None:
        assert nk == 1 and seq_rows % tm == 0
        in_specs.append(pl.BlockSpec((1, tn), lambda i, j, k: (0, j)))
        args.append(lerp_mu.reshape(1, n))
    grid = (m // tm, n // tn, nk)
    out_specs = [pl.BlockSpec((tm, tn), lambda i, j, k: (i, j))]
    out_shape = [jax.ShapeDtypeStruct((m, n), out_dtype)]
    if next_norm_g is not None:
        assert nk == 1 and out_dtype == F32
        in_specs.append(pl.BlockSpec((1, tn), lambda i, j, k: (0, j)))
        args.append(next_norm_g.reshape(1, n))
        out_specs += [pl.BlockSpec((tm, tn), lambda i, j, k: (i, j)), pl.BlockSpec((tm, LANES), lambda i, j, k: (i, 0))]
        out_shape += [jax.ShapeDtypeStruct((m, n), BF16), jax.ShapeDtypeStruct((m, LANES), F32)]
    if side is not None:
        s_in, s_out, s_shape = _side_specs(side, lambda i, j, k: (i * grid[1] + j) * nk + k, grid[0] * grid[1] * nk)
        in_specs.append(s_in)
        args.append(side[0])
        out_specs.append(s_out)
        out_shape.append(s_shape)
    outs = pl.pallas_call(
        functools.partial(_mm_kernel, nk=nk, w_nt=w_nt, act=act, has_mul=mul is not None, mul_act=mul_act,
                          has_res=res is not None, has_scale=row_ss is not None, has_norm=next_norm_g is not None,
                          has_side=side is not None, has_lerp=lerp_mu is not None, norm_dim=kdim,
                          seq_tiles=None if lerp_mu is None else seq_rows // tm),
        grid=grid,
        in_specs=in_specs,
        out_specs=out_specs,
        out_shape=out_shape,
        scratch_shapes=([pltpu.VMEM((tm, tn), F32)] if nk > 1 and out_dtype != F32 else [])
        + ([pltpu.VMEM((n // tn, SUBLANES, tn), F32)] if lerp_mu is not None else []),
        name=name,
        compiler_params=_params(*(("parallel", "parallel") if len(out_shape) == 1 and lerp_mu is None
                                  else ("arbitrary", "arbitrary")), "arbitrary"),
    )(*args)
    return outs[0] if len(outs) == 1 else tuple(outs)


def _proj_small_kernel(x_ref, wdt_ref, wlo_ref, mu_ref, o_ref, carry_ref, *, seq_tiles):
    x = x_ref[...]
    nt = (((1,), (1,)), ((), ()))
    dt = lax.dot_general(x, wdt_ref[...], nt, preferred_element_type=F32)
    lo = lax.dot_general(x, wlo_ref[...], nt, preferred_element_type=F32)
    prev = jnp.where(pl.program_id(0) % seq_tiles == 0, 0.0, carry_ref[...])
    shifted = pltpu.roll(jnp.concatenate([prev, lo], axis=0), 1, 0)[SUBLANES:]
    carry_ref[...] = lo[lo.shape[0] - SUBLANES:]
    lo = lo + (shifted - lo) * mu_ref[...]
    o_ref[:, 0:G_RANK_PAD] = lo[:, W_RANK + A_RANK:]
    o_ref[:, G_RANK_PAD:G_RANK_PAD + SSD_HEADS] = dt
    o_ref[:, G_RANK_PAD + SSD_HEADS:] = lo[:, :W_RANK + A_RANK]


def _proj_small(x, wt_dt, wt_lo, mu_lo, *, dt_row0, lo_row0, seq_rows, tm=512):
    m, kdim = x.shape
    win = lambda rows, row0: pl.BlockSpec((pl.Element(rows), pl.Element(kdim)), lambda i: (row0, 0))
    return pl.pallas_call(
        functools.partial(_proj_small_kernel, seq_tiles=seq_rows // tm),
        grid=(m // tm,),
        in_specs=[pl.BlockSpec((tm, kdim), lambda i: (i, 0)), win(SSD_HEADS, dt_row0), win(SMALL_LO_COLS, lo_row0),
                  pl.BlockSpec((1, SMALL_LO_COLS), lambda i: (0, 0))],
        out_specs=pl.BlockSpec((tm, SMALL_COLS), lambda i: (i, 0)),
        out_shape=jax.ShapeDtypeStruct((m, SMALL_COLS), F32),
        scratch_shapes=[pltpu.VMEM((SUBLANES, SMALL_LO_COLS), F32)],
        name="proj_small",
        compiler_params=_params("arbitrary"),
    )(x, wt_dt, wt_lo, mu_lo.reshape(1, SMALL_LO_COLS))


def _ple_embed_kernel(p_ref, w_ref, g_ref, o_ref):
    e = jnp.dot(p_ref[...], w_ref[...], preferred_element_type=F32)
    y = e * lax.rsqrt(jnp.mean(e * e, axis=-1, keepdims=True) + NORM_EPS)
    o_ref[...] = (y * g_ref[...]).astype(o_ref.dtype)


def _ple_embed(p, w, g, tm=256):
    m, kdim = p.shape
    n = w.shape[1]
    return pl.pallas_call(
        _ple_embed_kernel,
        grid=(m // tm,),
        in_specs=[pl.BlockSpec((tm, kdim), lambda i: (i, 0)), pl.BlockSpec((kdim, n), lambda i: (0, 0)),
                  pl.BlockSpec((1, n), lambda i: (0, 0))],
        out_specs=pl.BlockSpec((tm, n), lambda i: (i, 0)),
        out_shape=jax.ShapeDtypeStruct((m, n), F32),
        name="ple_embed",
        compiler_params=_params("parallel"),
    )(p, w, g.reshape(1, n))


def _ssd_kernel(xs_ref, bm_ref, cm_ref, dt_ref, z_ref, wxs_ref, wbm_ref, wcm_ref, bxs_ref, bbm_ref, bcm_ref,
                dtb_ref, alog_ref, d_ref, ng_ref, side_in_ref, o_ref, side_out_ref,
                st_ref, hxs_ref, hbm_ref, hcm_ref, axs_ref, abm_ref, acm_ref, *, n_chunks):
    tb = n_chunks * CHUNK
    hist = SUBLANES
    side_out_ref[...] = side_in_ref[...].astype(BF16)

    @pl.when(pl.program_id(2) == 0)
    def _():
        st_ref[...] = jnp.zeros_like(st_ref)
        for h_ref in (hxs_ref, hbm_ref, hcm_ref):
            h_ref[0:hist, :] = jnp.zeros((hist, h_ref.shape[1]), F32)

    for raw_ref, h_ref, w_ref, b_ref, act_ref in ((xs_ref, hxs_ref, wxs_ref, bxs_ref, axs_ref),
                                                  (bm_ref, hbm_ref, wbm_ref, bbm_ref, abm_ref),
                                                  (cm_ref, hcm_ref, wcm_ref, bcm_ref, acm_ref)):
        h_ref[hist:hist + tb, :] = raw_ref[0]
        hv = h_ref[...]
        acc = b_ref[...] + w_ref[SSD_CONV - 1:SSD_CONV, :] * hv[hist:]
        for j in range(SSD_CONV - 1):
            acc = acc + w_ref[j:j + 1, :] * pltpu.roll(hv, SSD_CONV - 1 - j, 0)[hist:]
        act_ref[...] = _silu(acc)
        h_ref[0:hist, :] = raw_ref[0, tb - hist:tb, :]

    gc = SSD_GROUP_COLS
    g = pl.program_id(1)
    expand = (lax.broadcasted_iota(jnp.int32, (SSD_HEADS, gc), 0)
              == (lax.broadcasted_iota(jnp.int32, (SSD_HEADS, gc), 1) >> LOG2_CHUNK) + g * SSD_HPG).astype(BF16)
    li = lax.broadcasted_iota(jnp.int32, (CHUNK, gc), 0)
    si = lax.broadcasted_iota(jnp.int32, (CHUNK, gc), 1) & (CHUNK - 1)
    eye_t = li == si
    causal_t = li >= si
    top = lax.broadcasted_iota(jnp.int32, (CHUNK, LANES), 1) < SSD_HEAD_DIM
    d_row = d_ref[0]
    ng_row = ng_ref[...]
    trow = lax.broadcasted_iota(jnp.int32, (tb, tb), 0)
    tcol = lax.broadcasted_iota(jnp.int32, (tb, tb), 1)
    tri_blk = (((trow >> LOG2_CHUNK) == (tcol >> LOG2_CHUNK)) & (trow >= tcol)).astype(BF16)
    dt_all = _softplus(dt_ref[0] + dtb_ref[...])
    acs_all = _cumsum_rows(tri_blk, dt_all * (-LOG2E * jnp.exp(alog_ref[...])))

    def expand_heads(q):
        q0, q1, q2 = _split3(q)
        de = lambda r: jnp.dot(r, expand, preferred_element_type=F32)
        return de(q0) + de(q1) + de(q2)

    st = st_ref[...]
    for c in range(n_chunks):
        sl = slice(c * CHUNK, (c + 1) * CHUNK)
        xs = axs_ref[sl, :]
        bm = abm_ref[sl, :]
        cm = acm_ref[sl, :]
        dt_exp = expand_heads(dt_all[sl, :])
        acs = expand_heads(acs_all[sl, :])
        rowpart = jnp.sum(jnp.where(eye_t, acs, 0.0), axis=0, keepdims=True)
        decay = jnp.exp2(jnp.where(causal_t, acs - rowpart, -jnp.inf))
        xdt = xs * dt_exp
        cb2 = _dot_nt(cm, jnp.concatenate([bm, bm], axis=0))
        last = acs[CHUNK - 1:CHUNK, :]
        y_off = _dot(cm, st) * jnp.exp2(acs)
        parts = []
        for j in range(gc // LANES):
            lanes = slice(j * LANES, (j + 1) * LANES)
            parts.append(_dot(cb2 * decay[:, lanes], _stack_heads(xdt[:, lanes], top)))
        y = jnp.concatenate(parts, axis=1) + y_off + d_row * xs
        st = st * jnp.exp2(last) + _dot_tn(bm, xdt * jnp.exp2(last - acs))
        y = y * _silu(z_ref[0, sl, :])
        y = y * lax.rsqrt(jnp.mean(y * y, axis=-1, keepdims=True) + SSD_NORM_EPS)
        o_ref[0, sl, :] = (y * ng_row).astype(o_ref.dtype)
    st_ref[...] = st


def _ssd(zx, small, conv_w, conv_b, dt_bias, a_log, d_skip, norm_g, side, *, tb=256):
    bsz, t, _ = zx.shape
    gc = SSD_GROUP_COLS
    xs_blk0 = SSD_D_INNER // gc
    bm_blk0 = 2 * SSD_D_INNER // SSD_STATE
    cm_blk0 = bm_blk0 + SSD_GROUPS
    wbm_blk0 = SSD_D_INNER // SSD_STATE
    wcm_blk0 = wbm_blk0 + SSD_GROUPS
    conv_b = conv_b.reshape(1, SSD_CONV_DIM)
    per_group = lambda v: jnp.repeat(v, SSD_HEAD_DIM).reshape(SSD_GROUPS, 1, gc)
    seq = lambda w, f: pl.BlockSpec((1, tb, w), lambda b, g, ti: (b, ti, f(g)))
    chan = lambda r, w, f: pl.BlockSpec((r, w), lambda b, g, ti: (0, f(g)))
    grp = pl.BlockSpec((1, 1, gc), lambda b, g, ti: (g, 0, 0))
    nt_steps = t // tb
    s_in, s_out, s_shape = _side_specs(side, lambda b, g, ti: (b * SSD_GROUPS + g) * nt_steps + ti,
                                       bsz * SSD_GROUPS * nt_steps)
    return pl.pallas_call(
        functools.partial(_ssd_kernel, n_chunks=tb // CHUNK),
        grid=(bsz, SSD_GROUPS, t // tb),
        in_specs=[seq(gc, lambda g: xs_blk0 + g), seq(SSD_STATE, lambda g: bm_blk0 + g),
                  seq(SSD_STATE, lambda g: cm_blk0 + g), seq(SSD_HEADS, lambda g: SMALL_DT_BLK),
                  seq(gc, lambda g: g),
                  chan(SSD_CONV, gc, lambda g: g), chan(SSD_CONV, SSD_STATE, lambda g: wbm_blk0 + g),
                  chan(SSD_CONV, SSD_STATE, lambda g: wcm_blk0 + g),
                  chan(1, gc, lambda g: g), chan(1, SSD_STATE, lambda g: wbm_blk0 + g),
                  chan(1, SSD_STATE, lambda g: wcm_blk0 + g),
                  chan(1, SSD_HEADS, lambda g: 0), chan(1, SSD_HEADS, lambda g: 0), grp,
                  chan(1, gc, lambda g: g), s_in],
        out_specs=[seq(gc, lambda g: g), s_out],
        out_shape=[jax.ShapeDtypeStruct((bsz, t, SSD_D_INNER), BF16), s_shape],
        scratch_shapes=[pltpu.VMEM((SSD_STATE, gc), F32),
                        pltpu.VMEM((SUBLANES + tb, gc), F32), pltpu.VMEM((SUBLANES + tb, SSD_STATE), F32),
                        pltpu.VMEM((SUBLANES + tb, SSD_STATE), F32),
                        pltpu.VMEM((tb, gc), F32), pltpu.VMEM((tb, SSD_STATE), F32), pltpu.VMEM((tb, SSD_STATE), F32)],
        name="ssd_scan",
        compiler_params=_params("arbitrary", "arbitrary", "arbitrary"),
    )(zx, zx, zx, small, zx, conv_w, conv_w, conv_w, conv_b, conv_b, conv_b, dt_bias.reshape(1, SSD_HEADS),
      a_log.reshape(1, SSD_HEADS), per_group(d_skip), norm_g.reshape(1, SSD_D_INNER), side[0])


def _wkv_kernel(r_ref, k_ref, v_ref, glo_ref, wlo_ref, alo_ref,
                w2_ref, a2_ref, g2_ref, w0_ref, a0_ref, kk_ref, ka_ref, rk_ref, lng_ref, lnb_ref, side_in_ref,
                o_ref, side_out_ref,
                s_ref, rr_ref, kr_ref, vv_ref, lw_ref, kn_ref, ag_ref, gg_ref,
                *, n_chunks, n_pairs):
    tb = n_chunks * CHUNK
    side_out_ref[...] = side_in_ref[...].astype(BF16)

    @pl.when(pl.program_id(2) == 0)
    def _():
        s_ref[...] = jnp.zeros_like(s_ref)

    r, k, v = r_ref[0], k_ref[0], v_ref[0]
    g_lo, w_lo, a_lo = glo_ref[0], wlo_ref[0], alo_ref[0]
    lw = (-DECAY_SCALE * LOG2E) * _sigmoid(w0_ref[...] + jnp.dot(jnp.tanh(w_lo).astype(BF16), w2_ref[...],
                                                                 preferred_element_type=F32))
    a_gate = _sigmoid(a0_ref[...] + jnp.dot(a_lo.astype(BF16), a2_ref[...], preferred_element_type=F32))
    gg_ref[...] = jnp.dot(_sigmoid(g_lo).astype(BF16), g2_ref[...], preferred_element_type=F32)
    top_blk = lax.broadcasted_iota(jnp.int32, (tb, LANES), 1) < RWKV_HEAD_DIM
    kk = k * kk_ref[...]
    for j in range(n_pairs):
        lanes = slice(j * LANES, (j + 1) * LANES)
        kj = kk[:, lanes]
        kn_ref[:, lanes] = kj * lax.rsqrt(jnp.maximum(_head_sum(kj * kj, top_blk), L2_EPS * L2_EPS))
    rr_ref[...] = r
    kr_ref[...] = k * (1.0 + (a_gate - 1.0) * ka_ref[...])
    vv_ref[...] = v
    lw_ref[...] = lw
    ag_ref[...] = a_gate

    row = lax.broadcasted_iota(jnp.int32, (LANES, LANES), 0)
    col = lax.broadcasted_iota(jnp.int32, (LANES, LANES), 1)
    same = (row >> LOG2_CHUNK) == (col >> LOG2_CHUNK)
    trow = lax.broadcasted_iota(jnp.int32, (tb, tb), 0)
    tcol = lax.broadcasted_iota(jnp.int32, (tb, tb), 1)
    tri_blk = ((trow >> LOG2_CHUNK) == (tcol >> LOG2_CHUNK)) & (trow >= tcol)
    top = lax.broadcasted_iota(jnp.int32, (CHUNK, LANES), 1) < RWKV_HEAD_DIM
    inv_n = 1.0 / RWKV_HEAD_DIM
    chains = [(j, c) for j in range(n_pairs) for c in range(n_chunks)]

    def piece(ref, j, c):
        return ref[c * CHUNK:(c + 1) * CHUNK, j * LANES:(j + 1) * LANES]

    cum_all = _cumsum_rows(tri_blk.astype(BF16), lw)

    lane_t = lax.broadcasted_iota(jnp.int32, (CHUNK, LANES), 1) & (CHUNK - 1)
    row_t = lax.broadcasted_iota(jnp.int32, (CHUNK, LANES), 0)
    strict_c = row_t > lane_t
    incl_c = row_t >= lane_t
    eye_c = (row_t == lane_t).astype(F32)
    strict2 = jnp.concatenate([strict_c, incl_c], axis=0)

    def bd(x):
        return _stack_heads(x.astype(BF16), top)

    nb = range(len(chains))
    a_c, r_c, b_bd, k_bd, v_c, bh_c, kh_c, wtot = [], [], [], [], [], [], [], []
    for j, c in chains:
        cum = cum_all[c * CHUNK:(c + 1) * CHUNK, j * LANES:(j + 1) * LANES]
        lwc = piece(lw_ref, j, c)
        kc = piece(kr_ref, j, c)
        knc = piece(kn_ref, j, c)
        b = knc * piece(ag_ref, j, c)
        tot = cum[CHUNK - 1:CHUNK, :]
        e_neg = jnp.exp2(-cum)
        e_end = jnp.exp2(tot - cum)
        a_c.append((-knc * jnp.exp2(cum - lwc)).astype(BF16))
        r_c.append(piece(rr_ref, j, c) * jnp.exp2(cum))
        b_bd.append(bd(b * e_neg))
        k_bd.append(bd(kc * e_neg))
        v_c.append(piece(vv_ref, j, c).astype(BF16))
        bh_c.append((b * e_end).astype(BF16))
        kh_c.append((kc * e_end).astype(BF16))
        wtot.append(jnp.exp2(tot))
    ar = [jnp.concatenate([a_c[i], r_c[i].astype(BF16)], axis=0) for i in nb]
    sb = [jnp.where(strict2, _dot_nt(ar[i], b_bd[i]), 0.0) for i in nb]
    sk = [jnp.where(strict2, _dot_nt(ar[i], k_bd[i]), 0.0) for i in nb]
    n = [x[:CHUNK] for x in sb]
    rb = [x[CHUNK:].astype(BF16) for x in sb]
    v_bd = [bd(x) for x in v_c]
    tinv = [eye_c + x for x in n]
    p_bd = [bd(x) for x in n]
    npow = [_dot(n[i], p_bd[i]) for i in nb]
    for it in range(1, 6):
        p_bd = [bd(x) for x in npow]
        if it < 5:
            both = [_dot(jnp.concatenate([npow[i], tinv[i]], axis=0), p_bd[i]) for i in nb]
            npow = [x[:CHUNK] for x in both]
            tinv = [tinv[i] + both[i][CHUNK:] for i in nb]
        else:
            tinv = [tinv[i] + _dot(tinv[i], p_bd[i]) for i in nb]
    akv = [_dot(sk[i][:CHUNK], v_bd[i]) for i in nb]
    pq = [_dot(tinv[i], jnp.concatenate([bd(a_c[i]), bd(akv[i])], axis=1)) for i in nb]
    pq_bd = [jnp.concatenate([bd(x[:, :LANES]), bd(x[:, LANES:])], axis=1) for x in pq]
    rpq = [_dot(rb[i], pq_bd[i]) for i in nb]
    r_eff = [(r_c[i] + rpq[i][:, :LANES]).astype(BF16) for i in nb]
    y0 = [rpq[i][:, LANES:] + _dot(sk[i][CHUNK:], v_bd[i]) for i in nb]
    g_eff = [jnp.where(same, _dot_tn(pq[i][:, :LANES], bh_c[i]), 0.0).astype(BF16) for i in nb]
    s1 = [jnp.where(same, _dot_tn(jnp.concatenate([pq[i][:, LANES:].astype(BF16), v_c[i]], axis=0),
                                  jnp.concatenate([bh_c[i], kh_c[i]], axis=0)), 0.0) for i in nb]

    for j in range(n_pairs):
        lanes = slice(j * LANES, (j + 1) * LANES)
        rk_row = rk_ref[:, lanes]
        lng_row = lng_ref[:, lanes]
        lnb_row = lnb_ref[:, lanes]
        s = s_ref[j]
        for c in range(n_chunks):
            i = j * n_chunks + c
            y = _dot_nt(r_eff[i], s) + y0[i]
            s = s * wtot[i] + _dot(s, g_eff[i]) + s1[i]
            mean = _head_sum(y, top) * inv_n
            yc = y - mean
            var = _head_sum(yc * yc, top) * inv_n
            yn = yc * lax.rsqrt(var + RWKV_GN_EPS) * lng_row + lnb_row
            bonus = _head_sum(piece(rr_ref, j, c) * piece(kr_ref, j, c) * rk_row, top) * piece(vv_ref, j, c)
            o_ref[0, c * CHUNK:(c + 1) * CHUNK, lanes] = ((yn + bonus) * piece(gg_ref, j, c)).astype(o_ref.dtype)
        s_ref[j] = s


def _wkv(rkv, small, w2, a2, g2p, w0, a0, k_k, k_a, r_k, ln_g, ln_b, side, *, tb=64, pairs=32):
    bsz, t, d3 = rkv.shape
    d = d3 // 3
    wl = pairs * LANES
    nh = d // wl
    seq = lambda w, f: pl.BlockSpec((1, tb, w), lambda b, h, ti: (b, ti, f(h)))
    chan = lambda r, w, f: pl.BlockSpec((r, w), lambda b, h, ti: (0, f(h)))
    vec = lambda q: q.reshape(1, d)
    blk = pltpu.VMEM((tb, wl), F32)
    nt_steps = t // tb
    s_in, s_out, s_shape = _side_specs(side, lambda b, h, ti: (b * nh + h) * nt_steps + ti, bsz * nh * nt_steps)
    return pl.pallas_call(
        functools.partial(_wkv_kernel, n_chunks=tb // CHUNK, n_pairs=pairs),
        grid=(bsz, nh, t // tb),
        in_specs=[seq(wl, lambda h: h), seq(wl, lambda h: nh + h), seq(wl, lambda h: 2 * nh + h),
                  seq(G_RANK_PAD, lambda h: 0), seq(W_RANK, lambda h: SMALL_WLO_BLK),
                  seq(A_RANK, lambda h: SMALL_ALO_BLK),
                  chan(W_RANK, wl, lambda h: h), chan(A_RANK, wl, lambda h: h), chan(G_RANK_PAD, wl, lambda h: h)]
                 + [chan(1, wl, lambda h: h)] * 7 + [s_in],
        out_specs=[seq(wl, lambda h: h), s_out],
        out_shape=[jax.ShapeDtypeStruct((bsz, t, d), BF16), s_shape],
        scratch_shapes=[pltpu.VMEM((pairs, LANES, LANES), F32)] + [blk] * 7,
        name="wkv7_scan",
        compiler_params=_params("arbitrary", "arbitrary", "arbitrary"),
    )(rkv, rkv, rkv, small, small, small,
      w2, a2, g2p, vec(w0), vec(a0), vec(k_k), vec(k_a), vec(r_k), vec(ln_g), vec(ln_b), side[0])


def kernel(x, p, norm_mix_g, w_in, ssd_conv_w, ssd_conv_b, ssd_dt_bias, ssd_a_log, ssd_d, ssd_norm_g, rwkv_mu, rwkv_w0, rwkv_w2, rwkv_a0, rwkv_a2, rwkv_g2, rwkv_k_k, rwkv_k_a, rwkv_r_k, rwkv_ln_g, rwkv_ln_b, w_branch_ssd, w_branch_rwkv, w_out, norm_ffn_g, w_ff1, w_ff2, norm_ple_g, w_ple_gate, w_ple_proj, ple_post_g, final_norm_g):
    bsz, t, d = x.shape
    m = bsz * t
    depth = w_in.shape[0]
    xf = x.reshape(m, d)
    for i in range(depth):
        c_dt = SSD_D_INNER + SSD_CONV_DIM
        c_r = c_dt + SSD_HEADS
        c_wlo = c_r + 3 * d
        c_glo = c_wlo + W_RANK + A_RANK
        c_gate = c_glo + G_RANK
        w_in_t = jnp.swapaxes(w_in[i], 0, 1)
        w_head = w_in_t[:c_r].astype(BF16)
        g2p = jnp.pad(rwkv_g2[i], ((0, G_RANK_PAD - G_RANK), (0, 0))).astype(BF16)
        whole = lambda w: (w, 0, w.shape[0])

        h = _rmsnorm(xf, norm_mix_g[i], BF16)
        proj = functools.partial(_mm, h, w_nt=True, out_dtype=F32, **TILE_WIDE)
        zx, w_rkv = proj(w_head, name="proj_zx", n=c_dt, side=(w_in_t, c_r, 3 * d))
        rkv, w_tail = proj(w_rkv, name="proj_rkv", n=3 * d, lerp_mu=rwkv_mu[i][:3 * d], seq_rows=t,
                           side=(w_in_t, c_wlo, w_in.shape[2] - c_wlo))
        gates, w_bssd_b = proj(w_tail, name="proj_gates", w_row0=c_gate - c_wlo, n=2 * d, side=whole(w_branch_ssd[i]))
        mu_lo = jnp.pad(rwkv_mu[i][3 * d:], (0, G_RANK_PAD - G_RANK))
        small = _proj_small(h, w_head, w_tail, mu_lo, dt_row0=c_dt, lo_row0=0, seq_rows=t).reshape(bsz, t, -1)
        zx = zx.reshape(bsz, t, -1)
        rkv = rkv.reshape(bsz, t, -1)

        u_ssd, w_ff1_b = _ssd(zx, small, ssd_conv_w[i], ssd_conv_b[i], ssd_dt_bias[i], ssd_a_log[i], ssd_d[i],
                              ssd_norm_g[i], whole(w_ff1[i]))
        u_rwkv, w_ff2_b = _wkv(rkv, small, rwkv_w2[i].astype(BF16), rwkv_a2[i].astype(BF16), g2p,
                               rwkv_w0[i], rwkv_a0[i], rwkv_k_k[i], rwkv_k_a[i], rwkv_r_k[i], rwkv_ln_g[i],
                               rwkv_ln_b[i], whole(w_ff2[i]))

        part, w_brwkv_b = _mm(u_ssd.reshape(m, SSD_D_INNER), w_bssd_b, name="branch_ssd", out_dtype=F32, **TILE_K2D,
                              mul=gates, mul_act="sigmoid", mul_col0=0, side=whole(w_branch_rwkv[i]))
        merged, w_out_b = _mm(u_rwkv.reshape(m, d), w_brwkv_b, name="branch_rwkv", out_dtype=BF16, **TILE_EPI,
                              mul=gates, mul_act="sigmoid", mul_col0=d, res=part, side=whole(w_out[i]))
        xf, xg, ss = _mm(merged, w_out_b, name="out_proj", out_dtype=F32, **TILE_EPI, res=xf,
                         next_norm_g=norm_ffn_g[i])

        ff, w_ple_b = _mm(xg, w_ff1_b, name="ffn_up", out_dtype=BF16, **TILE_WIDE, act="relu2", row_ss=ss,
                          side=whole(w_ple_gate[i]))
        xf, xg, ss = _mm(ff, w_ff2_b, name="ffn_down", out_dtype=F32, **TILE_K4D, res=xf,
                         next_norm_g=norm_ple_g[i])

        e = _ple_embed(p[i].reshape(m, PLE_DIM).astype(BF16), w_ple_proj[i].astype(BF16), ple_post_g[i])
        xf = _mm(xg, w_ple_b, name="ple_gate", out_dtype=F32, **TILE_EPI, act="sigmoid", mul=e, res=xf,
                 row_ss=ss)
    return _rmsnorm(xf, final_norm_g, F32).reshape(bsz, t, d)
```

```python
import functools
import math

import jax
import jax.numpy as jnp
from jax import lax
from jax.experimental import pallas as pl
from jax.experimental.pallas import tpu as pltpu

F32 = jnp.float32
BF16 = jnp.bfloat16

CHUNK = 64
LOG2_CHUNK = 6
SSD_D_INNER = 8192
SSD_HEAD_DIM = 64
SSD_HEADS = 128
SSD_GROUPS = 8
SSD_HPG = 16
SSD_STATE = 128
SSD_CONV = 4
SSD_GROUP_COLS = SSD_D_INNER // SSD_GROUPS
SSD_CONV_DIM = SSD_D_INNER + 2 * SSD_GROUPS * SSD_STATE
SSD_NORM_EPS = 1e-5
RWKV_HEAD_DIM = 64
W_RANK = 128
A_RANK = 128
G_RANK = 480
G_RANK_PAD = 512
DECAY_SCALE = math.exp(-0.5)
RWKV_GN_EPS = RWKV_HEAD_DIM * 1e-5
L2_EPS = 1e-12
LOG2E = math.log2(math.e)
NORM_EPS = 1e-6
PLE_DIM = 256

SMALL_COLS = G_RANK_PAD + SSD_HEADS + W_RANK + A_RANK
SMALL_LO_COLS = W_RANK + A_RANK + G_RANK_PAD

LANES = 128
SUBLANES = 8
BF16_ROWS = 16
VMEM_BYTES = 64 * 1024 * 1024
VMEM_LIMIT_BYTES = VMEM_BYTES - 8 * 1024 * 1024

SMALL_DT_BLK = G_RANK_PAD // LANES
SMALL_WLO_BLK = SMALL_DT_BLK + 1
SMALL_ALO_BLK = SMALL_DT_BLK + 2

TILE_WIDE = dict(tm=1024, tn=1024)
TILE_EPI = dict(tm=1024, tn=512)
TILE_K2D = dict(tm=1024, tn=256)
TILE_K4D = dict(tm=512, tn=256)


def _params(*sem):
    return pltpu.CompilerParams(dimension_semantics=sem, vmem_limit_bytes=VMEM_LIMIT_BYTES)


def _split3(x):
    hi = x.astype(BF16)
    r1 = x - hi.astype(F32)
    mid = r1.astype(BF16)
    lo = (r1 - mid.astype(F32)).astype(BF16)
    return hi, mid, lo


def _dot(a, b, dims=(((1,), (0,)), ((), ()))):
    return lax.dot_general(a.astype(BF16), b.astype(BF16), dims, preferred_element_type=F32)


def _dot_nt(a, b):
    return _dot(a, b, (((1,), (1,)), ((), ())))


def _dot_tn(a, b):
    return _dot(a, b, (((0,), (0,)), ((), ())))


def _cumsum_rows(tri_bf16, x):
    x0, x1, x2 = _split3(x)
    d = lambda q: jnp.dot(tri_bf16, q, preferred_element_type=F32)
    return d(x0) + d(x1) + d(x2)


def _sigmoid(x):
    return 0.5 * jnp.tanh(0.5 * x) + 0.5


def _silu(x):
    h = 0.5 * x
    return h + h * jnp.tanh(h)


def _softplus(x):
    return jnp.maximum(x, 0.0) + jnp.log1p(jnp.exp(-jnp.abs(x)))


def _stack_heads(x, top):
    return jnp.concatenate([jnp.where(top, x, 0.0), jnp.where(top, 0.0, x)], axis=0)


def _head_sum(x, top):
    s0 = jnp.sum(jnp.where(top, x, 0.0), axis=-1, keepdims=True)
    s1 = jnp.sum(jnp.where(top, 0.0, x), axis=-1, keepdims=True)
    return jnp.where(top, s0, s1)


def _side_specs(side, step_of, n_steps):
    src, row0, nrows = side
    cols = src.shape[1]
    assert src.ndim == 2 and nrows % BF16_ROWS == 0 and row0 % SUBLANES == 0
    rs = next(r for r in range(BF16_ROWS, nrows + 1, BF16_ROWS) if nrows % r == 0 and nrows // r <= n_steps)
    nslabs = nrows // rs
    slab = lambda *g: jnp.minimum(step_of(*g), nslabs - 1)
    in_spec = pl.BlockSpec((pl.Element(rs), pl.Element(cols)),
                           lambda *g: (pl.multiple_of(row0 + slab(*g) * rs, SUBLANES), 0))
    out_spec = pl.BlockSpec((rs, cols), lambda *g: (slab(*g), 0))
    return in_spec, out_spec, jax.ShapeDtypeStruct((nrows, cols), BF16)


def _rmsnorm_kernel(x_ref, g_ref, o_ref, *, eps):
    x = x_ref[...]
    y = x * lax.rsqrt(jnp.mean(x * x, axis=-1, keepdims=True) + eps)
    o_ref[...] = (y * g_ref[...]).astype(o_ref.dtype)


def _rmsnorm(x, g, out_dtype, tm=512):
    m, d = x.shape
    return pl.pallas_call(
        functools.partial(_rmsnorm_kernel, eps=NORM_EPS),
        grid=(m // tm,),
        in_specs=[pl.BlockSpec((tm, d), lambda i: (i, 0)), pl.BlockSpec((1, d), lambda i: (0, 0))],
        out_specs=pl.BlockSpec((tm, d), lambda i: (i, 0)),
        out_shape=jax.ShapeDtypeStruct((m, d), out_dtype),
        name="rmsnorm",
        compiler_params=_params("parallel"),
    )(x, g.reshape(1, d))


def _act(x, kind):
    if kind is None:
        return x
    if kind == "relu2":
        r = jnp.maximum(x, 0.0)
        return r * r
    if kind == "sigmoid":
        return _sigmoid(x)
    raise ValueError(kind)


def _mm_kernel(*refs, nk, w_nt, act, has_mul, mul_act, has_res, has_scale, has_norm, has_side, has_lerp, norm_dim,
               seq_tiles):
    refs = list(refs)
    x_ref, w_ref = refs[0], refs[1]
    pos = 2

    def take(flag):
        nonlocal pos
        if not flag:
            return None
        pos += 1
        return refs[pos - 1]

    mul_ref, res_ref, ss_in_ref, mu_ref = take(has_mul), take(has_res), take(has_scale), take(has_lerp)
    g_ref, side_in_ref = take(has_norm), take(has_side)
    o_ref = take(True)
    xg_ref, ss_out_ref, side_out_ref = take(has_norm), take(has_norm), take(has_side)
    carry_ref = refs[-1] if has_lerp else None
    if has_side:
        side_out_ref[...] = side_in_ref[...].astype(BF16)
    acc_ref = None if nk == 1 else (o_ref if o_ref.dtype == F32 else refs[pos])

    if w_nt:
        part = lax.dot_general(x_ref[...], w_ref[...], (((1,), (1,)), ((), ())), preferred_element_type=F32)
    else:
        part = jnp.dot(x_ref[...], w_ref[...], preferred_element_type=F32)

    def finish(acc):
        if has_scale:
            acc = acc * lax.rsqrt(ss_in_ref[:, 0:1] * (1.0 / norm_dim) + NORM_EPS)
        if has_lerp:
            i, j = pl.program_id(0), pl.program_id(1)
            prev = jnp.where(i % seq_tiles == 0, 0.0, carry_ref[j])
            shifted = pltpu.roll(jnp.concatenate([prev, acc], axis=0), 1, 0)[SUBLANES:]
            carry_ref[j] = acc[acc.shape[0] - SUBLANES:]
            acc = acc + (shifted - acc) * mu_ref[...]
        out = _act(acc, act)
        if has_mul:
            out = out * _act(mul_ref[...].astype(F32), mul_act)
        if has_res:
            out = res_ref[...].astype(F32) + out
        o_ref[...] = out.astype(o_ref.dtype)
        if has_norm:
            xg_ref[...] = (out * g_ref[...]).astype(BF16)
            row_ss = jnp.broadcast_to(jnp.sum(out * out, axis=1, keepdims=True), ss_out_ref.shape)
            first = pl.program_id(1) == 0

            @pl.when(first)
            def _():
                ss_out_ref[...] = row_ss

            @pl.when(jnp.logical_not(first))
            def _():
                ss_out_ref[...] += row_ss

    if nk == 1:
        finish(part)
    else:
        k = pl.program_id(2)

        @pl.when(k == 0)
        def _():
            acc_ref[...] = part

        @pl.when((k > 0) & (k < nk - 1))
        def _():
            acc_ref[...] += part

        @pl.when(k == nk - 1)
        def _():
            finish(acc_ref[...] + part)


def _mm(x, w, *, name, out_dtype, tm, tn, tk=None, w_nt=False, w_row0=0, n=None, act=None, mul=None, mul_act=None,
        mul_col0=0, res=None, res_col0=0, side=None, row_ss=None, next_norm_g=None, lerp_mu=None, seq_rows=None):
    m, kdim = x.shape
    n = w.shape[1] if n is None else n
    tk = kdim if tk is None else tk
    tm, tn = min(tm, m), min(tn, n)
    nk = kdim // tk
    assert m % tm == 0 and n % tn == 0 and kdim % tk == 0 and mul_col0 % tn == 0 and res_col0 % tn == 0
    if w_nt:
        assert w.shape[1] == kdim and w_row0 % BF16_ROWS == 0 and tn % BF16_ROWS == 0 and w.dtype == BF16
        w_spec = pl.BlockSpec((pl.Element(tn), pl.Element(tk)),
                              lambda i, j, k: (pl.multiple_of(w_row0 + j * tn, BF16_ROWS), k * tk))
    else:
        assert w.shape[0] == kdim and w_row0 == 0
        w_spec = pl.BlockSpec((tk, tn), lambda i, j, k: (k, j))
    in_specs = [pl.BlockSpec((tm, tk), lambda i, j, k: (i, k)), w_spec]
    args = [x, w]
    if mul is not None:
        off = mul_col0 // tn
        in_specs.append(pl.BlockSpec((tm, tn), lambda i, j, k, off=off: (i, j + off)))
        args.append(mul)
    if res is not None:
        off = res_col0 // tn
        in_specs.append(pl.BlockSpec((tm, tn), lambda i, j, k, off=off: (i, j + off)))
        args.append(res)
    if row_ss is not None:
        in_specs.append(pl.BlockSpec((tm, LANES), lambda i, j, k: (i, 0)))
        args.append(row_ss)
    if lerp_mu is not None:
        assert nk == 1 and seq_rows % tm == 0
        in_specs.append(pl.BlockSpec((1, tn), lambda i, j, k: (0, j)))
        args.append(lerp_mu.reshape(1, n))
    grid = (m // tm, n // tn, nk)
    out_specs = [pl.BlockSpec((tm, tn), lambda i, j, k: (i, j))]
    out_shape = [jax.ShapeDtypeStruct((m, n), out_dtype)]
    if next_norm_g is not None:
        assert nk == 1 and out_dtype == F32
        in_specs.append(pl.BlockSpec((1, tn), lambda i, j, k: (0, j)))
        args.append(next_norm_g.reshape(1, n))
        out_specs += [pl.BlockSpec((tm, tn), lambda i, j, k: (i, j)), pl.BlockSpec((tm, LANES), lambda i, j, k: (i, 0))]
        out_shape += [jax.ShapeDtypeStruct((m, n), BF16), jax.ShapeDtypeStruct((m, LANES), F32)]
    if side is not None:
        s_in, s_out, s_shape = _side_specs(side, lambda i, j, k: (i * grid[1] + j) * nk + k, grid[0] * grid[1] * nk)
        in_specs.append(s_in)
        args.append(side[0])
        out_specs.append(s_out)
        out_shape.append(s_shape)
    outs = pl.pallas_call(
        functools.partial(_mm_kernel, nk=nk, w_nt=w_nt, act=act, has_mul=mul is not None, mul_act=mul_act,
                          has_res=res is not None, has_scale=row_ss is not None, has_norm=next_norm_g is not None,
                          has_side=side is not None, has_lerp=lerp_mu is not None, norm_dim=kdim,
                          seq_tiles=None if lerp_mu is None else seq_rows // tm),
        grid=grid,
        in_specs=in_specs,
        out_specs=out_specs,
        out_shape=out_shape,
        scratch_shapes=([pltpu.VMEM((tm, tn), F32)] if nk > 1 and out_dtype != F32 else [])
        + ([pltpu.VMEM((n // tn, SUBLANES, tn), F32)] if lerp_mu is not None else []),
        name=name,
        compiler_params=_params(*(("parallel", "parallel") if len(out_shape) == 1 and lerp_mu is None
                                  else ("arbitrary", "arbitrary")), "arbitrary"),
    )(*args)
    return outs[0] if len(outs) == 1 else tuple(outs)


def _proj_small_kernel(x_ref, wdt_ref, wlo_ref, mu_ref, o_ref, carry_ref, *, seq_tiles):
    x = x_ref[...]
    nt = (((1,), (1,)), ((), ()))
    dt = lax.dot_general(x, wdt_ref[...], nt, preferred_element_type=F32)
    lo = lax.dot_general(x, wlo_ref[...], nt, preferred_element_type=F32)
    prev = jnp.where(pl.program_id(0) % seq_tiles == 0, 0.0, carry_ref[...])
    shifted = pltpu.roll(jnp.concatenate([prev, lo], axis=0), 1, 0)[SUBLANES:]
    carry_ref[...] = lo[lo.shape[0] - SUBLANES:]
    lo = lo + (shifted - lo) * mu_ref[...]
    o_ref[:, 0:G_RANK_PAD] = lo[:, W_RANK + A_RANK:]
    o_ref[:, G_RANK_PAD:G_RANK_PAD + SSD_HEADS] = dt
    o_ref[:, G_RANK_PAD + SSD_HEADS:] = lo[:, :W_RANK + A_RANK]


def _proj_small(x, wt_dt, wt_lo, mu_lo, *, dt_row0, lo_row0, seq_rows, tm=512):
    m, kdim = x.shape
    win = lambda rows, row0: pl.BlockSpec((pl.Element(rows), pl.Element(kdim)), lambda i: (row0, 0))
    return pl.pallas_call(
        functools.partial(_proj_small_kernel, seq_tiles=seq_rows // tm),
        grid=(m // tm,),
        in_specs=[pl.BlockSpec((tm, kdim), lambda i: (i, 0)), win(SSD_HEADS, dt_row0), win(SMALL_LO_COLS, lo_row0),
                  pl.BlockSpec((1, SMALL_LO_COLS), lambda i: (0, 0))],
        out_specs=pl.BlockSpec((tm, SMALL_COLS), lambda i: (i, 0)),
        out_shape=jax.ShapeDtypeStruct((m, SMALL_COLS), F32),
        scratch_shapes=[pltpu.VMEM((SUBLANES, SMALL_LO_COLS), F32)],
        name="proj_small",
        compiler_params=_params("arbitrary"),
    )(x, wt_dt, wt_lo, mu_lo.reshape(1, SMALL_LO_COLS))


def _ple_embed_kernel(p_ref, w_ref, g_ref, o_ref):
    e = jnp.dot(p_ref[...], w_ref[...], preferred_element_type=F32)
    y = e * lax.rsqrt(jnp.mean(e * e, axis=-1, keepdims=True) + NORM_EPS)
    o_ref[...] = (y * g_ref[...]).astype(o_ref.dtype)


def _ple_embed(p, w, g, tm=256):
    m, kdim = p.shape
    n = w.shape[1]
    return pl.pallas_call(
        _ple_embed_kernel,
        grid=(m // tm,),
        in_specs=[pl.BlockSpec((tm, kdim), lambda i: (i, 0)), pl.BlockSpec((kdim, n), lambda i: (0, 0)),
                  pl.BlockSpec((1, n), lambda i: (0, 0))],
        out_specs=pl.BlockSpec((tm, n), lambda i: (i, 0)),
        out_shape=jax.ShapeDtypeStruct((m, n), F32),
        name="ple_embed",
        compiler_params=_params("parallel"),
    )(p, w, g.reshape(1, n))


def _ssd_kernel(xs_ref, bm_ref, cm_ref, dt_ref, z_ref, wxs_ref, wbm_ref, wcm_ref, bxs_ref, bbm_ref, bcm_ref,
                dtb_ref, alog_ref, d_ref, ng_ref, side_in_ref, o_ref, side_out_ref,
                st_ref, hxs_ref, hbm_ref, hcm_ref, axs_ref, abm_ref, acm_ref, *, n_chunks):
    tb = n_chunks * CHUNK
    hist = SUBLANES
    side_out_ref[...] = side_in_ref[...].astype(BF16)

    @pl.when(pl.program_id(2) == 0)
    def _():
        st_ref[...] = jnp.zeros_like(st_ref)
        for h_ref in (hxs_ref, hbm_ref, hcm_ref):
            h_ref[0:hist, :] = jnp.zeros((hist, h_ref.shape[1]), F32)

    for raw_ref, h_ref, w_ref, b_ref, act_ref in ((xs_ref, hxs_ref, wxs_ref, bxs_ref, axs_ref),
                                                  (bm_ref, hbm_ref, wbm_ref, bbm_ref, abm_ref),
                                                  (cm_ref, hcm_ref, wcm_ref, bcm_ref, acm_ref)):
        h_ref[hist:hist + tb, :] = raw_ref[0]
        hv = h_ref[...]
        acc = b_ref[...] + w_ref[SSD_CONV - 1:SSD_CONV, :] * hv[hist:]
        for j in range(SSD_CONV - 1):
            acc = acc + w_ref[j:j + 1, :] * pltpu.roll(hv, SSD_CONV - 1 - j, 0)[hist:]
        act_ref[...] = _silu(acc)
        h_ref[0:hist, :] = raw_ref[0, tb - hist:tb, :]

    gc = SSD_GROUP_COLS
    g = pl.program_id(1)
    expand = (lax.broadcasted_iota(jnp.int32, (SSD_HEADS, gc), 0)
              == (lax.broadcasted_iota(jnp.int32, (SSD_HEADS, gc), 1) >> LOG2_CHUNK) + g * SSD_HPG).astype(BF16)
    li = lax.broadcasted_iota(jnp.int32, (CHUNK, gc), 0)
    si = lax.broadcasted_iota(jnp.int32, (CHUNK, gc), 1) & (CHUNK - 1)
    eye_t = li == si
    causal_t = li >= si
    top = lax.broadcasted_iota(jnp.int32, (CHUNK, LANES), 1) < SSD_HEAD_DIM
    d_row = d_ref[0]
    ng_row = ng_ref[...]
    trow = lax.broadcasted_iota(jnp.int32, (tb, tb), 0)
    tcol = lax.broadcasted_iota(jnp.int32, (tb, tb), 1)
    tri_blk = (((trow >> LOG2_CHUNK) == (tcol >> LOG2_CHUNK)) & (trow >= tcol)).astype(BF16)
    dt_all = _softplus(dt_ref[0] + dtb_ref[...])
    acs_all = _cumsum_rows(tri_blk, dt_all * (-LOG2E * jnp.exp(alog_ref[...])))

    def expand_heads(q, passes=3):
        out = None
        for term in _split3(q)[:passes]:
            part = jnp.dot(term, expand, preferred_element_type=F32)
            out = part if out is None else out + part
        return out

    st = st_ref[...]
    for c in range(n_chunks):
        sl = slice(c * CHUNK, (c + 1) * CHUNK)
        xs = axs_ref[sl, :]
        bm = abm_ref[sl, :]
        cm = acm_ref[sl, :]
        dt_exp = expand_heads(dt_all[sl, :], passes=2)
        acs = expand_heads(acs_all[sl, :])
        rowpart = jnp.sum(jnp.where(eye_t, acs, 0.0), axis=0, keepdims=True)
        decay = jnp.exp2(jnp.where(causal_t, acs - rowpart, -jnp.inf))
        xdt = xs * dt_exp
        cb2 = _dot_nt(cm, jnp.concatenate([bm, bm], axis=0))
        last = acs[CHUNK - 1:CHUNK, :]
        y_off = _dot(cm, st) * jnp.exp2(acs)
        parts = []
        for j in range(gc // LANES):
            lanes = slice(j * LANES, (j + 1) * LANES)
            parts.append(_dot(cb2 * decay[:, lanes], _stack_heads(xdt[:, lanes], top)))
        y = jnp.concatenate(parts, axis=1) + y_off + d_row * xs
        st = st * jnp.exp2(last) + _dot_tn(bm, xdt * jnp.exp2(last - acs))
        y = y * _silu(z_ref[0, sl, :])
        y = y * lax.rsqrt(jnp.mean(y * y, axis=-1, keepdims=True) + SSD_NORM_EPS)
        o_ref[0, sl, :] = (y * ng_row).astype(o_ref.dtype)
    st_ref[...] = st


def _ssd(zx, small, conv_w, conv_b, dt_bias, a_log, d_skip, norm_g, side, *, tb=256):
    bsz, t, _ = zx.shape
    gc = SSD_GROUP_COLS
    xs_blk0 = SSD_D_INNER // gc
    bm_blk0 = 2 * SSD_D_INNER // SSD_STATE
    cm_blk0 = bm_blk0 + SSD_GROUPS
    wbm_blk0 = SSD_D_INNER // SSD_STATE
    wcm_blk0 = wbm_blk0 + SSD_GROUPS
    conv_b = conv_b.reshape(1, SSD_CONV_DIM)
    per_group = lambda v: jnp.repeat(v, SSD_HEAD_DIM).reshape(SSD_GROUPS, 1, gc)
    seq = lambda w, f: pl.BlockSpec((1, tb, w), lambda b, g, ti: (b, ti, f(g)))
    chan = lambda r, w, f: pl.BlockSpec((r, w), lambda b, g, ti: (0, f(g)))
    grp = pl.BlockSpec((1, 1, gc), lambda b, g, ti: (g, 0, 0))
    nt_steps = t // tb
    s_in, s_out, s_shape = _side_specs(side, lambda b, g, ti: (b * SSD_GROUPS + g) * nt_steps + ti,
                                       bsz * SSD_GROUPS * nt_steps)
    return pl.pallas_call(
        functools.partial(_ssd_kernel, n_chunks=tb // CHUNK),
        grid=(bsz, SSD_GROUPS, t // tb),
        in_specs=[seq(gc, lambda g: xs_blk0 + g), seq(SSD_STATE, lambda g: bm_blk0 + g),
                  seq(SSD_STATE, lambda g: cm_blk0 + g), seq(SSD_HEADS, lambda g: SMALL_DT_BLK),
                  seq(gc, lambda g: g),
                  chan(SSD_CONV, gc, lambda g: g), chan(SSD_CONV, SSD_STATE, lambda g: wbm_blk0 + g),
                  chan(SSD_CONV, SSD_STATE, lambda g: wcm_blk0 + g),
                  chan(1, gc, lambda g: g), chan(1, SSD_STATE, lambda g: wbm_blk0 + g),
                  chan(1, SSD_STATE, lambda g: wcm_blk0 + g),
                  chan(1, SSD_HEADS, lambda g: 0), chan(1, SSD_HEADS, lambda g: 0), grp,
                  chan(1, gc, lambda g: g), s_in],
        out_specs=[seq(gc, lambda g: g), s_out],
        out_shape=[jax.ShapeDtypeStruct((bsz, t, SSD_D_INNER), BF16), s_shape],
        scratch_shapes=[pltpu.VMEM((SSD_STATE, gc), F32),
                        pltpu.VMEM((SUBLANES + tb, gc), F32), pltpu.VMEM((SUBLANES + tb, SSD_STATE), F32),
                        pltpu.VMEM((SUBLANES + tb, SSD_STATE), F32),
                        pltpu.VMEM((tb, gc), F32), pltpu.VMEM((tb, SSD_STATE), F32), pltpu.VMEM((tb, SSD_STATE), F32)],
        name="ssd_scan",
        compiler_params=_params("arbitrary", "arbitrary", "arbitrary"),
    )(zx, zx, zx, small, zx, conv_w, conv_w, conv_w, conv_b, conv_b, conv_b, dt_bias.reshape(1, SSD_HEADS),
      a_log.reshape(1, SSD_HEADS), per_group(d_skip), norm_g.reshape(1, SSD_D_INNER), side[0])


def _wkv_kernel(r_ref, k_ref, v_ref, glo_ref, wlo_ref, alo_ref,
                w2_ref, a2_ref, g2_ref, w0_ref, a0_ref, kk_ref, ka_ref, rk_ref, lng_ref, lnb_ref, side_in_ref,
                o_ref, side_out_ref,
                s_ref, rr_ref, kr_ref, vv_ref, lw_ref, kn_ref, ag_ref, gg_ref,
                *, n_chunks, n_pairs):
    tb = n_chunks * CHUNK
    side_out_ref[...] = side_in_ref[...].astype(BF16)

    @pl.when(pl.program_id(2) == 0)
    def _():
        s_ref[...] = jnp.zeros_like(s_ref)

    r, k, v = r_ref[0], k_ref[0], v_ref[0]
    g_lo, w_lo, a_lo = glo_ref[0], wlo_ref[0], alo_ref[0]
    lw = (-DECAY_SCALE * LOG2E) * _sigmoid(w0_ref[...] + jnp.dot(jnp.tanh(w_lo).astype(BF16), w2_ref[...],
                                                                 preferred_element_type=F32))
    a_gate = _sigmoid(a0_ref[...] + jnp.dot(a_lo.astype(BF16), a2_ref[...], preferred_element_type=F32))
    gg_ref[...] = jnp.dot(_sigmoid(g_lo).astype(BF16), g2_ref[...], preferred_element_type=F32)
    top_blk = lax.broadcasted_iota(jnp.int32, (tb, LANES), 1) < RWKV_HEAD_DIM
    kk = k * kk_ref[...]
    for j in range(n_pairs):
        lanes = slice(j * LANES, (j + 1) * LANES)
        kj = kk[:, lanes]
        kn_ref[:, lanes] = kj * lax.rsqrt(jnp.maximum(_head_sum(kj * kj, top_blk), L2_EPS * L2_EPS))
    rr_ref[...] = r
    kr_ref[...] = k * (1.0 + (a_gate - 1.0) * ka_ref[...])
    vv_ref[...] = v
    lw_ref[...] = lw
    ag_ref[...] = a_gate

    row = lax.broadcasted_iota(jnp.int32, (LANES, LANES), 0)
    col = lax.broadcasted_iota(jnp.int32, (LANES, LANES), 1)
    same = (row >> LOG2_CHUNK) == (col >> LOG2_CHUNK)
    trow = lax.broadcasted_iota(jnp.int32, (tb, tb), 0)
    tcol = lax.broadcasted_iota(jnp.int32, (tb, tb), 1)
    tri_blk = ((trow >> LOG2_CHUNK) == (tcol >> LOG2_CHUNK)) & (trow >= tcol)
    top = lax.broadcasted_iota(jnp.int32, (CHUNK, LANES), 1) < RWKV_HEAD_DIM
    inv_n = 1.0 / RWKV_HEAD_DIM
    chains = [(j, c) for j in range(n_pairs) for c in range(n_chunks)]

    def piece(ref, j, c):
        return ref[c * CHUNK:(c + 1) * CHUNK, j * LANES:(j + 1) * LANES]

    cum_all = _cumsum_rows(tri_blk.astype(BF16), lw)

    lane_t = lax.broadcasted_iota(jnp.int32, (CHUNK, LANES), 1) & (CHUNK - 1)
    row_t = lax.broadcasted_iota(jnp.int32, (CHUNK, LANES), 0)
    strict_c = row_t > lane_t
    incl_c = row_t >= lane_t
    eye_c = (row_t == lane_t).astype(F32)
    strict2 = jnp.concatenate([strict_c, incl_c], axis=0)

    def bd(x):
        return _stack_heads(x.astype(BF16), top)

    nb = range(len(chains))
    a_c, r_c, b_bd, k_bd, v_c, bh_c, kh_c, wtot = [], [], [], [], [], [], [], []
    for j, c in chains:
        cum = cum_all[c * CHUNK:(c + 1) * CHUNK, j * LANES:(j + 1) * LANES]
        lwc = piece(lw_ref, j, c)
        kc = piece(kr_ref, j, c)
        knc = piece(kn_ref, j, c)
        b = knc * piece(ag_ref, j, c)
        tot = cum[CHUNK - 1:CHUNK, :]
        e_neg = jnp.exp2(-cum)
        e_end = jnp.exp2(tot - cum)
        a_c.append((-knc * jnp.exp2(cum - lwc)).astype(BF16))
        r_c.append(piece(rr_ref, j, c) * jnp.exp2(cum))
        b_bd.append(bd(b * e_neg))
        k_bd.append(bd(kc * e_neg))
        v_c.append(piece(vv_ref, j, c).astype(BF16))
        bh_c.append((b * e_end).astype(BF16))
        kh_c.append((kc * e_end).astype(BF16))
        wtot.append(jnp.exp2(tot))
    ar = [jnp.concatenate([a_c[i], r_c[i].astype(BF16)], axis=0) for i in nb]
    sb = [jnp.where(strict2, _dot_nt(ar[i], b_bd[i]), 0.0) for i in nb]
    sk = [jnp.where(strict2, _dot_nt(ar[i], k_bd[i]), 0.0) for i in nb]
    n = [x[:CHUNK] for x in sb]
    rb = [x[CHUNK:].astype(BF16) for x in sb]
    v_bd = [bd(x) for x in v_c]
    tinv = [eye_c + x for x in n]
    p_bd = [bd(x) for x in n]
    npow = [_dot(n[i], p_bd[i]) for i in nb]
    for it in range(1, 6):
        p_bd = [bd(x) for x in npow]
        if it < 5:
            both = [_dot(jnp.concatenate([npow[i], tinv[i]], axis=0), p_bd[i]) for i in nb]
            npow = [x[:CHUNK] for x in both]
            tinv = [tinv[i] + both[i][CHUNK:] for i in nb]
        else:
            tinv = [tinv[i] + _dot(tinv[i], p_bd[i]) for i in nb]
    akv = [_dot(sk[i][:CHUNK], v_bd[i]) for i in nb]
    pq = [_dot(tinv[i], jnp.concatenate([bd(a_c[i]), bd(akv[i])], axis=1)) for i in nb]
    pq_bd = [jnp.concatenate([bd(x[:, :LANES]), bd(x[:, LANES:])], axis=1) for x in pq]
    rpq = [_dot(rb[i], pq_bd[i]) for i in nb]
    r_eff = [(r_c[i] + rpq[i][:, :LANES]).astype(BF16) for i in nb]
    y0 = [rpq[i][:, LANES:] + _dot(sk[i][CHUNK:], v_bd[i]) for i in nb]
    g_eff = [jnp.where(same, _dot_tn(pq[i][:, :LANES], bh_c[i]), 0.0).astype(BF16) for i in nb]
    s1 = [jnp.where(same, _dot_tn(jnp.concatenate([pq[i][:, LANES:].astype(BF16), v_c[i]], axis=0),
                                  jnp.concatenate([bh_c[i], kh_c[i]], axis=0)), 0.0) for i in nb]

    for j in range(n_pairs):
        lanes = slice(j * LANES, (j + 1) * LANES)
        rk_row = rk_ref[:, lanes]
        lng_row = lng_ref[:, lanes]
        lnb_row = lnb_ref[:, lanes]
        s = s_ref[j]
        for c in range(n_chunks):
            i = j * n_chunks + c
            y = _dot_nt(r_eff[i], s) + y0[i]
            s = s * wtot[i] + _dot(s, g_eff[i]) + s1[i]
            mean = _head_sum(y, top) * inv_n
            yc = y - mean
            var = _head_sum(yc * yc, top) * inv_n
            yn = yc * lax.rsqrt(var + RWKV_GN_EPS) * lng_row + lnb_row
            bonus = _head_sum(piece(rr_ref, j, c) * piece(kr_ref, j, c) * rk_row, top) * piece(vv_ref, j, c)
            o_ref[0, c * CHUNK:(c + 1) * CHUNK, lanes] = ((yn + bonus) * piece(gg_ref, j, c)).astype(o_ref.dtype)
        s_ref[j] = s


def _wkv(rkv, small, w2, a2, g2p, w0, a0, k_k, k_a, r_k, ln_g, ln_b, side, *, tb=64, pairs=32):
    bsz, t, d3 = rkv.shape
    d = d3 // 3
    wl = pairs * LANES
    nh = d // wl
    seq = lambda w, f: pl.BlockSpec((1, tb, w), lambda b, h, ti: (b, ti, f(h)))
    chan = lambda r, w, f: pl.BlockSpec((r, w), lambda b, h, ti: (0, f(h)))
    vec = lambda q: q.reshape(1, d)
    blk = pltpu.VMEM((tb, wl), F32)
    nt_steps = t // tb
    s_in, s_out, s_shape = _side_specs(side, lambda b, h, ti: (b * nh + h) * nt_steps + ti, bsz * nh * nt_steps)
    return pl.pallas_call(
        functools.partial(_wkv_kernel, n_chunks=tb // CHUNK, n_pairs=pairs),
        grid=(bsz, nh, t // tb),
        in_specs=[seq(wl, lambda h: h), seq(wl, lambda h: nh + h), seq(wl, lambda h: 2 * nh + h),
                  seq(G_RANK_PAD, lambda h: 0), seq(W_RANK, lambda h: SMALL_WLO_BLK),
                  seq(A_RANK, lambda h: SMALL_ALO_BLK),
                  chan(W_RANK, wl, lambda h: h), chan(A_RANK, wl, lambda h: h), chan(G_RANK_PAD, wl, lambda h: h)]
                 + [chan(1, wl, lambda h: h)] * 7 + [s_in],
        out_specs=[seq(wl, lambda h: h), s_out],
        out_shape=[jax.ShapeDtypeStruct((bsz, t, d), BF16), s_shape],
        scratch_shapes=[pltpu.VMEM((pairs, LANES, LANES), F32)] + [blk] * 7,
        name="wkv7_scan",
        compiler_params=_params("arbitrary", "arbitrary", "arbitrary"),
    )(rkv, rkv, rkv, small, small, small,
      w2, a2, g2p, vec(w0), vec(a0), vec(k_k), vec(k_a), vec(r_k), vec(ln_g), vec(ln_b), side[0])


def kernel(x, p, norm_mix_g, w_in, ssd_conv_w, ssd_conv_b, ssd_dt_bias, ssd_a_log, ssd_d, ssd_norm_g, rwkv_mu, rwkv_w0, rwkv_w2, rwkv_a0, rwkv_a2, rwkv_g2, rwkv_k_k, rwkv_k_a, rwkv_r_k, rwkv_ln_g, rwkv_ln_b, w_branch_ssd, w_branch_rwkv, w_out, norm_ffn_g, w_ff1, w_ff2, norm_ple_g, w_ple_gate, w_ple_proj, ple_post_g, final_norm_g):
    bsz, t, d = x.shape
    m = bsz * t
    depth = w_in.shape[0]
    xf = x.reshape(m, d)
    for i in range(depth):
        c_dt = SSD_D_INNER + SSD_CONV_DIM
        c_r = c_dt + SSD_HEADS
        c_wlo = c_r + 3 * d
        c_glo = c_wlo + W_RANK + A_RANK
        c_gate = c_glo + G_RANK
        w_in_t = jnp.swapaxes(w_in[i], 0, 1)
        w_head = w_in_t[:c_r].astype(BF16)
        g2p = jnp.pad(rwkv_g2[i], ((0, G_RANK_PAD - G_RANK), (0, 0))).astype(BF16)
        whole = lambda w: (w, 0, w.shape[0])

        h = _rmsnorm(xf, norm_mix_g[i], BF16)
        proj = functools.partial(_mm, h, w_nt=True, out_dtype=F32, **TILE_WIDE)
        zx, w_rkv = proj(w_head, name="proj_zx", n=c_dt, side=(w_in_t, c_r, 3 * d))
        rkv, w_tail = proj(w_rkv, name="proj_rkv", n=3 * d, lerp_mu=rwkv_mu[i][:3 * d], seq_rows=t,
                           side=(w_in_t, c_wlo, w_in.shape[2] - c_wlo))
        gates, w_bssd_b = proj(w_tail, name="proj_gates", w_row0=c_gate - c_wlo, n=2 * d, side=whole(w_branch_ssd[i]))
        mu_lo = jnp.pad(rwkv_mu[i][3 * d:], (0, G_RANK_PAD - G_RANK))
        small = _proj_small(h, w_head, w_tail, mu_lo, dt_row0=c_dt, lo_row0=0, seq_rows=t).reshape(bsz, t, -1)
        zx = zx.reshape(bsz, t, -1)
        rkv = rkv.reshape(bsz, t, -1)

        u_ssd, w_ff1_b = _ssd(zx, small, ssd_conv_w[i], ssd_conv_b[i], ssd_dt_bias[i], ssd_a_log[i], ssd_d[i],
                              ssd_norm_g[i], whole(w_ff1[i]))
        u_rwkv, w_ff2_b = _wkv(rkv, small, rwkv_w2[i].astype(BF16), rwkv_a2[i].astype(BF16), g2p,
                               rwkv_w0[i], rwkv_a0[i], rwkv_k_k[i], rwkv_k_a[i], rwkv_r_k[i], rwkv_ln_g[i],
                               rwkv_ln_b[i], whole(w_ff2[i]))

        part, w_brwkv_b = _mm(u_ssd.reshape(m, SSD_D_INNER), w_bssd_b, name="branch_ssd", out_dtype=F32, **TILE_K2D,
                              mul=gates, mul_act="sigmoid", mul_col0=0, side=whole(w_branch_rwkv[i]))
        merged, w_out_b = _mm(u_rwkv.reshape(m, d), w_brwkv_b, name="branch_rwkv", out_dtype=BF16, **TILE_EPI,
                              mul=gates, mul_act="sigmoid", mul_col0=d, res=part, side=whole(w_out[i]))
        xf, xg, ss = _mm(merged, w_out_b, name="out_proj", out_dtype=F32, **TILE_EPI, res=xf,
                         next_norm_g=norm_ffn_g[i])

        ff, w_ple_b = _mm(xg, w_ff1_b, name="ffn_up", out_dtype=BF16, **TILE_WIDE, act="relu2", row_ss=ss,
                          side=whole(w_ple_gate[i]))
        xf, xg, ss = _mm(ff, w_ff2_b, name="ffn_down", out_dtype=F32, **TILE_K4D, res=xf,
                         next_norm_g=norm_ple_g[i])

        e = _ple_embed(p[i].reshape(m, PLE_DIM).astype(BF16), w_ple_proj[i].astype(BF16), ple_post_g[i])
        xf = _mm(xg, w_ple_b, name="ple_gate", out_dtype=F32, **TILE_EPI, act="sigmoid", mul=e, res=xf,
                 row_ss=ss)
    return _rmsnorm(xf, final_norm_g, F32).reshape(bsz, t, d)
```

```python
import functools
import math

import jax
import jax.numpy as jnp
from jax import lax
from jax.experimental import pallas as pl
from jax.experimental.pallas import tpu as pltpu

F32 = jnp.float32
BF16 = jnp.bfloat16

CHUNK = 64
LOG2_CHUNK = 6
SSD_D_INNER = 8192
SSD_HEAD_DIM = 64
SSD_HEADS = 128
SSD_GROUPS = 8
SSD_HPG = 16
SSD_STATE = 128
SSD_CONV = 4
SSD_GROUP_COLS = SSD_D_INNER // SSD_GROUPS
SSD_CONV_DIM = SSD_D_INNER + 2 * SSD_GROUPS * SSD_STATE
SSD_NORM_EPS = 1e-5
RWKV_HEAD_DIM = 64
W_RANK = 128
A_RANK = 128
G_RANK = 480
G_RANK_PAD = 512
DECAY_SCALE = math.exp(-0.5)
RWKV_GN_EPS = RWKV_HEAD_DIM * 1e-5
L2_EPS = 1e-12
LOG2E = math.log2(math.e)
NORM_EPS = 1e-6
PLE_DIM = 256

SMALL_COLS = G_RANK_PAD + SSD_HEADS + W_RANK + A_RANK
SMALL_LO_COLS = W_RANK + A_RANK + G_RANK_PAD

LANES = 128
SUBLANES = 8
BF16_ROWS = 16
VMEM_BYTES = 64 * 1024 * 1024
VMEM_LIMIT_BYTES = VMEM_BYTES - 8 * 1024 * 1024

SMALL_DT_BLK = G_RANK_PAD // LANES
SMALL_WLO_BLK = SMALL_DT_BLK + 1
SMALL_ALO_BLK = SMALL_DT_BLK + 2

TILE_WIDE = dict(tm=1024, tn=1024)
TILE_EPI = dict(tm=1024, tn=512)
TILE_K2D = dict(tm=1024, tn=256)
TILE_K4D = dict(tm=512, tn=256)


def _params(*sem):
    return pltpu.CompilerParams(dimension_semantics=sem, vmem_limit_bytes=VMEM_LIMIT_BYTES)


def _split3(x):
    hi = x.astype(BF16)
    r1 = x - hi.astype(F32)
    mid = r1.astype(BF16)
    lo = (r1 - mid.astype(F32)).astype(BF16)
    return hi, mid, lo


def _dot(a, b, dims=(((1,), (0,)), ((), ()))):
    return lax.dot_general(a.astype(BF16), b.astype(BF16), dims, preferred_element_type=F32)


def _dot_nt(a, b):
    return _dot(a, b, (((1,), (1,)), ((), ())))


def _dot_tn(a, b):
    return _dot(a, b, (((0,), (0,)), ((), ())))


def _cumsum_rows(tri_bf16, x):
    x0, x1, x2 = _split3(x)
    d = lambda q: jnp.dot(tri_bf16, q, preferred_element_type=F32)
    return d(x0) + d(x1) + d(x2)


def _sigmoid(x):
    return 0.5 * jnp.tanh(0.5 * x) + 0.5


def _silu(x):
    h = 0.5 * x
    return h + h * jnp.tanh(h)


def _softplus(x):
    return jnp.maximum(x, 0.0) + jnp.log1p(jnp.exp(-jnp.abs(x)))


def _stack_heads(x, top):
    return jnp.concatenate([jnp.where(top, x, 0.0), jnp.where(top, 0.0, x)], axis=0)


def _head_sum(x, top):
    s0 = jnp.sum(jnp.where(top, x, 0.0), axis=-1, keepdims=True)
    s1 = jnp.sum(jnp.where(top, 0.0, x), axis=-1, keepdims=True)
    return jnp.where(top, s0, s1)


def _side_specs(side, step_of, n_steps):
    src, row0, nrows = side
    cols = src.shape[1]
    assert src.ndim == 2 and nrows % BF16_ROWS == 0 and row0 % SUBLANES == 0
    rs = next(r for r in range(BF16_ROWS, nrows + 1, BF16_ROWS) if nrows % r == 0 and nrows // r <= n_steps)
    nslabs = nrows // rs
    slab = lambda *g: jnp.minimum(step_of(*g), nslabs - 1)
    in_spec = pl.BlockSpec((pl.Element(rs), pl.Element(cols)),
                           lambda *g: (pl.multiple_of(row0 + slab(*g) * rs, SUBLANES), 0))
    out_spec = pl.BlockSpec((rs, cols), lambda *g: (slab(*g), 0))
    return in_spec, out_spec, jax.ShapeDtypeStruct((nrows, cols), BF16)


def _rmsnorm_kernel(x_ref, g_ref, o_ref, *, eps):
    x = x_ref[...]
    y = x * lax.rsqrt(jnp.mean(x * x, axis=-1, keepdims=True) + eps)
    o_ref[...] = (y * g_ref[...]).astype(o_ref.dtype)


def _rmsnorm(x, g, out_dtype, tm=512):
    m, d = x.shape
    return pl.pallas_call(
        functools.partial(_rmsnorm_kernel, eps=NORM_EPS),
        grid=(m // tm,),
        in_specs=[pl.BlockSpec((tm, d), lambda i: (i, 0)), pl.BlockSpec((1, d), lambda i: (0, 0))],
        out_specs=pl.BlockSpec((tm, d), lambda i: (i, 0)),
        out_shape=jax.ShapeDtypeStruct((m, d), out_dtype),
        name="rmsnorm",
        compiler_params=_params("parallel"),
    )(x, g.reshape(1, d))


def _act(x, kind):
    if kind is None:
        return x
    if kind == "relu2":
        r = jnp.maximum(x, 0.0)
        return r * r
    if kind == "sigmoid":
        return _sigmoid(x)
    raise ValueError(kind)


def _mm_kernel(*refs, nk, w_nt, act, has_mul, mul_act, has_res, has_scale, has_norm, has_side, has_lerp, norm_dim,
               seq_tiles):
    refs = list(refs)
    x_ref, w_ref = refs[0], refs[1]
    pos = 2

    def take(flag):
        nonlocal pos
        if not flag:
            return None
        pos += 1
        return refs[pos - 1]

    mul_ref, res_ref, ss_in_ref, mu_ref = take(has_mul), take(has_res), take(has_scale), take(has_lerp)
    g_ref, side_in_ref = take(has_norm), take(has_side)
    o_ref = take(True)
    xg_ref, ss_out_ref, side_out_ref = take(has_norm), take(has_norm), take(has_side)
    carry_ref = refs[-1] if has_lerp else None
    if has_side:
        side_out_ref[...] = side_in_ref[...].astype(BF16)
    acc_ref = None if nk == 1 else (o_ref if o_ref.dtype == F32 else refs[pos])

    if w_nt:
        part = lax.dot_general(x_ref[...], w_ref[...], (((1,), (1,)), ((), ())), preferred_element_type=F32)
    else:
        part = jnp.dot(x_ref[...], w_ref[...], preferred_element_type=F32)

    def finish(acc):
        if has_scale:
            acc = acc * lax.rsqrt(ss_in_ref[:, 0:1] * (1.0 / norm_dim) + NORM_EPS)
        if has_lerp:
            i, j = pl.program_id(0), pl.program_id(1)
            prev = jnp.where(i % seq_tiles == 0, 0.0, carry_ref[j])
            shifted = pltpu.roll(jnp.concatenate([prev, acc], axis=0), 1, 0)[SUBLANES:]
            carry_ref[j] = acc[acc.shape[0] - SUBLANES:]
            acc = acc + (shifted - acc) * mu_ref[...]
        out = _act(acc, act)
        if has_mul:
            out = out * _act(mul_ref[...].astype(F32), mul_act)
        if has_res:
            out = res_ref[...].astype(F32) + out
        o_ref[...] = out.astype(o_ref.dtype)
        if has_norm:
            xg_ref[...] = (out * g_ref[...]).astype(BF16)
            row_ss = jnp.broadcast_to(jnp.sum(out * out, axis=1, keepdims=True), ss_out_ref.shape)
            first = pl.program_id(1) == 0

            @pl.when(first)
            def _():
                ss_out_ref[...] = row_ss

            @pl.when(jnp.logical_not(first))
            def _():
                ss_out_ref[...] += row_ss

    if nk == 1:
        finish(part)
    else:
        k = pl.program_id(2)

        @pl.when(k == 0)
        def _():
            acc_ref[...] = part

        @pl.when((k > 0) & (k < nk - 1))
        def _():
            acc_ref[...] += part

        @pl.when(k == nk - 1)
        def _():
            finish(acc_ref[...] + part)


def _mm(x, w, *, name, out_dtype, tm, tn, tk=None, w_nt=False, w_row0=0, n=None, act=None, mul=None, mul_act=None,
        mul_col0=0, res=None, res_col0=0, side=None, row_ss=None, next_norm_g=None, lerp_mu=None, seq_rows=None):
    m, kdim = x.shape
    n = w.shape[1] if n is None else n
    tk = kdim if tk is None else tk
    tm, tn = min(tm, m), min(tn, n)
    nk = kdim // tk
    assert m % tm == 0 and n % tn == 0 and kdim % tk == 0 and mul_col0 % tn == 0 and res_col0 % tn == 0
    if w_nt:
        assert w.shape[1] == kdim and w_row0 % BF16_ROWS == 0 and tn % BF16_ROWS == 0 and w.dtype == BF16
        w_spec = pl.BlockSpec((pl.Element(tn), pl.Element(tk)),
                              lambda i, j, k: (pl.multiple_of(w_row0 + j * tn, BF16_ROWS), k * tk))
    else:
        assert w.shape[0] == kdim and w_row0 == 0
        w_spec = pl.BlockSpec((tk, tn), lambda i, j, k: (k, j))
    in_specs = [pl.BlockSpec((tm, tk), lambda i, j, k: (i, k)), w_spec]
    args = [x, w]
    if mul is not None:
        off = mul_col0 // tn
        in_specs.append(pl.BlockSpec((tm, tn), lambda i, j, k, off=off: (i, j + off)))
        args.append(mul)
    if res is not None:
        off = res_col0 // tn
        in_specs.append(pl.BlockSpec((tm, tn), lambda i, j, k, off=off: (i, j + off)))
        args.append(res)
    if row_ss is not None:
        in_specs.append(pl.BlockSpec((tm, LANES), lambda i, j, k: (i, 0)))
        args.append(row_ss)
    if lerp_mu is not None:
        assert nk == 1 and seq_rows % tm == 0
        in_specs.append(pl.BlockSpec((1, tn), lambda i, j, k: (0, j)))
        args.append(lerp_mu.reshape(1, n))
    grid = (m // tm, n // tn, nk)
    out_specs = [pl.BlockSpec((tm, tn), lambda i, j, k: (i, j))]
    out_shape = [jax.ShapeDtypeStruct((m, n), out_dtype)]
    if next_norm_g is not None:
        assert nk == 1 and out_dtype == F32
        in_specs.append(pl.BlockSpec((1, tn), lambda i, j, k: (0, j)))
        args.append(next_norm_g.reshape(1, n))
        out_specs += [pl.BlockSpec((tm, tn), lambda i, j, k: (i, j)), pl.BlockSpec((tm, LANES), lambda i, j, k: (i, 0))]
        out_shape += [jax.ShapeDtypeStruct((m, n), BF16), jax.ShapeDtypeStruct((m, LANES), F32)]
    if side is not None:
        s_in, s_out, s_shape = _side_specs(side, lambda i, j, k: (i * grid[1] + j) * nk + k, grid[0] * grid[1] * nk)
        in_specs.append(s_in)
        args.append(side[0])
        out_specs.append(s_out)
        out_shape.append(s_shape)
    outs = pl.pallas_call(
        functools.partial(_mm_kernel, nk=nk, w_nt=w_nt, act=act, has_mul=mul is not None, mul_act=mul_act,
                          has_res=res is not None, has_scale=row_ss is not None, has_norm=next_norm_g is not None,
                          has_side=side is not None, has_lerp=lerp_mu is not None, norm_dim=kdim,
                          seq_tiles=None if lerp_mu is None else seq_rows // tm),
        grid=grid,
        in_specs=in_specs,
        out_specs=out_specs,
        out_shape=out_shape,
        scratch_shapes=([pltpu.VMEM((tm, tn), F32)] if nk > 1 and out_dtype != F32 else [])
        + ([pltpu.VMEM((n // tn, SUBLANES, tn), F32)] if lerp_mu is not None else []),
        name=name,
        compiler_params=_params(*(("parallel", "parallel") if len(out_shape) == 1 and lerp_mu is None
                                  else ("arbitrary", "arbitrary")), "arbitrary"),
    )(*args)
    return outs[0] if len(outs) == 1 else tuple(outs)


def _proj_small_kernel(x_ref, wdt_ref, wlo_ref, mu_ref, o_ref, carry_ref, *, seq_tiles):
    x = x_ref[...]
    nt = (((1,), (1,)), ((), ()))
    dt = lax.dot_general(x, wdt_ref[...], nt, preferred_element_type=F32)
    lo = lax.dot_general(x, wlo_ref[...], nt, preferred_element_type=F32)
    prev = jnp.where(pl.program_id(0) % seq_tiles == 0, 0.0, carry_ref[...])
    shifted = pltpu.roll(jnp.concatenate([prev, lo], axis=0), 1, 0)[SUBLANES:]
    carry_ref[...] = lo[lo.shape[0] - SUBLANES:]
    lo = lo + (shifted - lo) * mu_ref[...]
    o_ref[:, 0:G_RANK_PAD] = lo[:, W_RANK + A_RANK:]
    o_ref[:, G_RANK_PAD:G_RANK_PAD + SSD_HEADS] = dt
    o_ref[:, G_RANK_PAD + SSD_HEADS:] = lo[:, :W_RANK + A_RANK]


def _proj_small(x, wt_dt, wt_lo, mu_lo, *, dt_row0, lo_row0, seq_rows, tm=1024):
    m, kdim = x.shape
    win = lambda rows, row0: pl.BlockSpec((pl.Element(rows), pl.Element(kdim)), lambda i: (row0, 0))
    return pl.pallas_call(
        functools.partial(_proj_small_kernel, seq_tiles=seq_rows // tm),
        grid=(m // tm,),
        in_specs=[pl.BlockSpec((tm, kdim), lambda i: (i, 0)), win(SSD_HEADS, dt_row0), win(SMALL_LO_COLS, lo_row0),
                  pl.BlockSpec((1, SMALL_LO_COLS), lambda i: (0, 0))],
        out_specs=pl.BlockSpec((tm, SMALL_COLS), lambda i: (i, 0)),
        out_shape=jax.ShapeDtypeStruct((m, SMALL_COLS), F32),
        scratch_shapes=[pltpu.VMEM((SUBLANES, SMALL_LO_COLS), F32)],
        name="proj_small",
        compiler_params=_params("arbitrary"),
    )(x, wt_dt, wt_lo, mu_lo.reshape(1, SMALL_LO_COLS))


def _ple_embed_kernel(p_ref, w_ref, g_ref, o_ref):
    e = jnp.dot(p_ref[...], w_ref[...], preferred_element_type=F32)
    y = e * lax.rsqrt(jnp.mean(e * e, axis=-1, keepdims=True) + NORM_EPS)
    o_ref[...] = (y * g_ref[...]).astype(o_ref.dtype)


def _ple_embed(p, w, g, tm=512):
    m, kdim = p.shape
    n = w.shape[1]
    return pl.pallas_call(
        _ple_embed_kernel,
        grid=(m // tm,),
        in_specs=[pl.BlockSpec((tm, kdim), lambda i: (i, 0)), pl.BlockSpec((kdim, n), lambda i: (0, 0)),
                  pl.BlockSpec((1, n), lambda i: (0, 0))],
        out_specs=pl.BlockSpec((tm, n), lambda i: (i, 0)),
        out_shape=jax.ShapeDtypeStruct((m, n), F32),
        name="ple_embed",
        compiler_params=_params("parallel"),
    )(p, w, g.reshape(1, n))


def _ssd_kernel(xs_ref, bm_ref, cm_ref, dt_ref, z_ref, wxs_ref, wbm_ref, wcm_ref, bxs_ref, bbm_ref, bcm_ref,
                dtb_ref, alog_ref, d_ref, ng_ref, side_in_ref, o_ref, side_out_ref,
                st_ref, hxs_ref, hbm_ref, hcm_ref, axs_ref, abm_ref, acm_ref, *, n_chunks):
    tb = n_chunks * CHUNK
    hist = SUBLANES
    side_out_ref[...] = side_in_ref[...].astype(BF16)

    @pl.when(pl.program_id(2) == 0)
    def _():
        st_ref[...] = jnp.zeros_like(st_ref)
        for h_ref in (hxs_ref, hbm_ref, hcm_ref):
            h_ref[0:hist, :] = jnp.zeros((hist, h_ref.shape[1]), F32)

    for raw_ref, h_ref, w_ref, b_ref, act_ref in ((xs_ref, hxs_ref, wxs_ref, bxs_ref, axs_ref),
                                                  (bm_ref, hbm_ref, wbm_ref, bbm_ref, abm_ref),
                                                  (cm_ref, hcm_ref, wcm_ref, bcm_ref, acm_ref)):
        h_ref[hist:hist + tb, :] = raw_ref[0]
        hv = h_ref[...]
        acc = b_ref[...] + w_ref[SSD_CONV - 1:SSD_CONV, :] * hv[hist:]
        for j in range(SSD_CONV - 1):
            acc = acc + w_ref[j:j + 1, :] * pltpu.roll(hv, SSD_CONV - 1 - j, 0)[hist:]
        act_ref[...] = _silu(acc)
        h_ref[0:hist, :] = raw_ref[0, tb - hist:tb, :]

    gc = SSD_GROUP_COLS
    g = pl.program_id(1)
    expand = (lax.broadcasted_iota(jnp.int32, (SSD_HEADS, gc), 0)
              == (lax.broadcasted_iota(jnp.int32, (SSD_HEADS, gc), 1) >> LOG2_CHUNK) + g * SSD_HPG).astype(BF16)
    li = lax.broadcasted_iota(jnp.int32, (CHUNK, gc), 0)
    si = lax.broadcasted_iota(jnp.int32, (CHUNK, gc), 1) & (CHUNK - 1)
    eye_t = li == si
    causal_t = li >= si
    top = lax.broadcasted_iota(jnp.int32, (CHUNK, LANES), 1) < SSD_HEAD_DIM
    d_row = d_ref[0]
    ng_row = ng_ref[...]
    trow = lax.broadcasted_iota(jnp.int32, (tb, tb), 0)
    tcol = lax.broadcasted_iota(jnp.int32, (tb, tb), 1)
    tri_blk = (((trow >> LOG2_CHUNK) == (tcol >> LOG2_CHUNK)) & (trow >= tcol)).astype(BF16)
    dt_all = _softplus(dt_ref[0] + dtb_ref[...])
    acs_all = _cumsum_rows(tri_blk, dt_all * (-LOG2E * jnp.exp(alog_ref[...])))

    def expand_heads(q, passes=3):
        out = None
        for term in _split3(q)[:passes]:
            part = jnp.dot(term, expand, preferred_element_type=F32)
            out = part if out is None else out + part
        return out

    st = st_ref[...]
    for c in range(n_chunks):
        sl = slice(c * CHUNK, (c + 1) * CHUNK)
        xs = axs_ref[sl, :]
        bm = abm_ref[sl, :]
        cm = acm_ref[sl, :]
        dt_exp = expand_heads(dt_all[sl, :], passes=2)
        acs = expand_heads(acs_all[sl, :])
        rowpart = jnp.sum(jnp.where(eye_t, acs, 0.0), axis=0, keepdims=True)
        decay = jnp.exp2(jnp.where(causal_t, acs - rowpart, -jnp.inf))
        xdt = xs * dt_exp
        cb2 = _dot_nt(cm, jnp.concatenate([bm, bm], axis=0))
        last = acs[CHUNK - 1:CHUNK, :]
        y_off = _dot(cm, st) * jnp.exp2(acs)
        parts = []
        for j in range(gc // LANES):
            lanes = slice(j * LANES, (j + 1) * LANES)
            parts.append(_dot(cb2 * decay[:, lanes], _stack_heads(xdt[:, lanes], top)))
        y = jnp.concatenate(parts, axis=1) + y_off + d_row * xs
        st = st * jnp.exp2(last) + _dot_tn(bm, xdt * jnp.exp2(last - acs))
        y = y * _silu(z_ref[0, sl, :])
        y = y * lax.rsqrt(jnp.mean(y * y, axis=-1, keepdims=True) + SSD_NORM_EPS)
        o_ref[0, sl, :] = (y * ng_row).astype(o_ref.dtype)
    st_ref[...] = st


def _ssd(zx, small, conv_w, conv_b, dt_bias, a_log, d_skip, norm_g, side, *, tb=256):
    bsz, t, _ = zx.shape
    gc = SSD_GROUP_COLS
    xs_blk0 = SSD_D_INNER // gc
    bm_blk0 = 2 * SSD_D_INNER // SSD_STATE
    cm_blk0 = bm_blk0 + SSD_GROUPS
    wbm_blk0 = SSD_D_INNER // SSD_STATE
    wcm_blk0 = wbm_blk0 + SSD_GROUPS
    conv_b = conv_b.reshape(1, SSD_CONV_DIM)
    per_group = lambda v: jnp.repeat(v, SSD_HEAD_DIM).reshape(SSD_GROUPS, 1, gc)
    seq = lambda w, f: pl.BlockSpec((1, tb, w), lambda b, g, ti: (b, ti, f(g)))
    chan = lambda r, w, f: pl.BlockSpec((r, w), lambda b, g, ti: (0, f(g)))
    grp = pl.BlockSpec((1, 1, gc), lambda b, g, ti: (g, 0, 0))
    nt_steps = t // tb
    s_in, s_out, s_shape = _side_specs(side, lambda b, g, ti: (b * SSD_GROUPS + g) * nt_steps + ti,
                                       bsz * SSD_GROUPS * nt_steps)
    return pl.pallas_call(
        functools.partial(_ssd_kernel, n_chunks=tb // CHUNK),
        grid=(bsz, SSD_GROUPS, t // tb),
        in_specs=[seq(gc, lambda g: xs_blk0 + g), seq(SSD_STATE, lambda g: bm_blk0 + g),
                  seq(SSD_STATE, lambda g: cm_blk0 + g), seq(SSD_HEADS, lambda g: SMALL_DT_BLK),
                  seq(gc, lambda g: g),
                  chan(SSD_CONV, gc, lambda g: g), chan(SSD_CONV, SSD_STATE, lambda g: wbm_blk0 + g),
                  chan(SSD_CONV, SSD_STATE, lambda g: wcm_blk0 + g),
                  chan(1, gc, lambda g: g), chan(1, SSD_STATE, lambda g: wbm_blk0 + g),
                  chan(1, SSD_STATE, lambda g: wcm_blk0 + g),
                  chan(1, SSD_HEADS, lambda g: 0), chan(1, SSD_HEADS, lambda g: 0), grp,
                  chan(1, gc, lambda g: g), s_in],
        out_specs=[seq(gc, lambda g: g), s_out],
        out_shape=[jax.ShapeDtypeStruct((bsz, t, SSD_D_INNER), BF16), s_shape],
        scratch_shapes=[pltpu.VMEM((SSD_STATE, gc), F32),
                        pltpu.VMEM((SUBLANES + tb, gc), F32), pltpu.VMEM((SUBLANES + tb, SSD_STATE), F32),
                        pltpu.VMEM((SUBLANES + tb, SSD_STATE), F32),
                        pltpu.VMEM((tb, gc), F32), pltpu.VMEM((tb, SSD_STATE), F32), pltpu.VMEM((tb, SSD_STATE), F32)],
        name="ssd_scan",
        compiler_params=_params("arbitrary", "arbitrary", "arbitrary"),
    )(zx, zx, zx, small, zx, conv_w, conv_w, conv_w, conv_b, conv_b, conv_b, dt_bias.reshape(1, SSD_HEADS),
      a_log.reshape(1, SSD_HEADS), per_group(d_skip), norm_g.reshape(1, SSD_D_INNER), side[0])


def _wkv_kernel(r_ref, k_ref, v_ref, glo_ref, wlo_ref, alo_ref,
                w2_ref, a2_ref, g2_ref, w0_ref, a0_ref, kk_ref, ka_ref, rk_ref, lng_ref, lnb_ref, side_in_ref,
                o_ref, side_out_ref,
                s_ref, rr_ref, kr_ref, vv_ref, lw_ref, kn_ref, ag_ref, gg_ref,
                *, n_chunks, n_pairs):
    tb = n_chunks * CHUNK
    side_out_ref[...] = side_in_ref[...].astype(BF16)

    @pl.when(pl.program_id(2) == 0)
    def _():
        s_ref[...] = jnp.zeros_like(s_ref)

    r, k, v = r_ref[0], k_ref[0], v_ref[0]
    g_lo, w_lo, a_lo = glo_ref[0], wlo_ref[0], alo_ref[0]
    lw = (-DECAY_SCALE * LOG2E) * _sigmoid(w0_ref[...] + jnp.dot(jnp.tanh(w_lo).astype(BF16), w2_ref[...],
                                                                 preferred_element_type=F32))
    a_gate = _sigmoid(a0_ref[...] + jnp.dot(a_lo.astype(BF16), a2_ref[...], preferred_element_type=F32))
    gg_ref[...] = jnp.dot(_sigmoid(g_lo).astype(BF16), g2_ref[...], preferred_element_type=F32)
    top_blk = lax.broadcasted_iota(jnp.int32, (tb, LANES), 1) < RWKV_HEAD_DIM
    kk = k * kk_ref[...]
    for j in range(n_pairs):
        lanes = slice(j * LANES, (j + 1) * LANES)
        kj = kk[:, lanes]
        kn_ref[:, lanes] = kj * lax.rsqrt(jnp.maximum(_head_sum(kj * kj, top_blk), L2_EPS * L2_EPS))
    rr_ref[...] = r
    kr_ref[...] = k * (1.0 + (a_gate - 1.0) * ka_ref[...])
    vv_ref[...] = v
    lw_ref[...] = lw
    ag_ref[...] = a_gate

    row = lax.broadcasted_iota(jnp.int32, (LANES, LANES), 0)
    col = lax.broadcasted_iota(jnp.int32, (LANES, LANES), 1)
    same = (row >> LOG2_CHUNK) == (col >> LOG2_CHUNK)
    trow = lax.broadcasted_iota(jnp.int32, (tb, tb), 0)
    tcol = lax.broadcasted_iota(jnp.int32, (tb, tb), 1)
    tri_blk = ((trow >> LOG2_CHUNK) == (tcol >> LOG2_CHUNK)) & (trow >= tcol)
    top = lax.broadcasted_iota(jnp.int32, (CHUNK, LANES), 1) < RWKV_HEAD_DIM
    inv_n = 1.0 / RWKV_HEAD_DIM
    chains = [(j, c) for j in range(n_pairs) for c in range(n_chunks)]

    def piece(ref, j, c):
        return ref[c * CHUNK:(c + 1) * CHUNK, j * LANES:(j + 1) * LANES]

    cum_all = _cumsum_rows(tri_blk.astype(BF16), lw)

    lane_t = lax.broadcasted_iota(jnp.int32, (CHUNK, LANES), 1) & (CHUNK - 1)
    row_t = lax.broadcasted_iota(jnp.int32, (CHUNK, LANES), 0)
    strict_c = row_t > lane_t
    incl_c = row_t >= lane_t
    eye_c = (row_t == lane_t).astype(F32)
    strict2 = jnp.concatenate([strict_c, incl_c], axis=0)

    def bd(x):
        return _stack_heads(x.astype(BF16), top)

    nb = range(len(chains))
    a_c, r_c, b_bd, k_bd, v_c, bh_c, kh_c, wtot = [], [], [], [], [], [], [], []
    for j, c in chains:
        cum = cum_all[c * CHUNK:(c + 1) * CHUNK, j * LANES:(j + 1) * LANES]
        lwc = piece(lw_ref, j, c)
        kc = piece(kr_ref, j, c)
        knc = piece(kn_ref, j, c)
        b = knc * piece(ag_ref, j, c)
        tot = cum[CHUNK - 1:CHUNK, :]
        e_neg = jnp.exp2(-cum)
        e_end = jnp.exp2(tot - cum)
        a_c.append((-knc * jnp.exp2(cum - lwc)).astype(BF16))
        r_c.append(piece(rr_ref, j, c) * jnp.exp2(cum))
        b_bd.append(bd(b * e_neg))
        k_bd.append(bd(kc * e_neg))
        v_c.append(piece(vv_ref, j, c).astype(BF16))
        bh_c.append((b * e_end).astype(BF16))
        kh_c.append((kc * e_end).astype(BF16))
        wtot.append(jnp.exp2(tot))
    ar = [jnp.concatenate([a_c[i], r_c[i].astype(BF16)], axis=0) for i in nb]
    sb = [jnp.where(strict2, _dot_nt(ar[i], b_bd[i]), 0.0) for i in nb]
    sk = [jnp.where(strict2, _dot_nt(ar[i], k_bd[i]), 0.0) for i in nb]
    n = [x[:CHUNK] for x in sb]
    rb = [x[CHUNK:].astype(BF16) for x in sb]
    v_bd = [bd(x) for x in v_c]
    tinv = [eye_c + x for x in n]
    p_bd = [bd(x) for x in n]
    npow = [_dot(n[i], p_bd[i]) for i in nb]
    for it in range(1, 6):
        p_bd = [bd(x) for x in npow]
        if it < 5:
            both = [_dot(jnp.concatenate([npow[i], tinv[i]], axis=0), p_bd[i]) for i in nb]
            npow = [x[:CHUNK] for x in both]
            tinv = [tinv[i] + both[i][CHUNK:] for i in nb]
        else:
            tinv = [tinv[i] + _dot(tinv[i], p_bd[i]) for i in nb]
    akv = [_dot(sk[i][:CHUNK], v_bd[i]) for i in nb]
    pq = [_dot(tinv[i], jnp.concatenate([bd(a_c[i]), bd(akv[i])], axis=1)) for i in nb]
    pq_bd = [jnp.concatenate([bd(x[:, :LANES]), bd(x[:, LANES:])], axis=1) for x in pq]
    rpq = [_dot(rb[i], pq_bd[i]) for i in nb]
    r_eff = [(r_c[i] + rpq[i][:, :LANES]).astype(BF16) for i in nb]
    y0 = [rpq[i][:, LANES:] + _dot(sk[i][CHUNK:], v_bd[i]) for i in nb]
    g_eff = [jnp.where(same, _dot_tn(pq[i][:, :LANES], bh_c[i]), 0.0).astype(BF16) for i in nb]
    s1 = [jnp.where(same, _dot_tn(jnp.concatenate([pq[i][:, LANES:].astype(BF16), v_c[i]], axis=0),
                                  jnp.concatenate([bh_c[i], kh_c[i]], axis=0)), 0.0) for i in nb]

    for j in range(n_pairs):
        lanes = slice(j * LANES, (j + 1) * LANES)
        rk_row = rk_ref[:, lanes]
        lng_row = lng_ref[:, lanes]
        lnb_row = lnb_ref[:, lanes]
        s = s_ref[j]
        for c in range(n_chunks):
            i = j * n_chunks + c
            y = _dot_nt(r_eff[i], s) + y0[i]
            s = s * wtot[i] + _dot(s, g_eff[i]) + s1[i]
            mean = _head_sum(y, top) * inv_n
            yc = y - mean
            var = _head_sum(yc * yc, top) * inv_n
            yn = yc * lax.rsqrt(var + RWKV_GN_EPS) * lng_row + lnb_row
            bonus = _head_sum(piece(rr_ref, j, c) * piece(kr_ref, j, c) * rk_row, top) * piece(vv_ref, j, c)
            o_ref[0, c * CHUNK:(c + 1) * CHUNK, lanes] = ((yn + bonus) * piece(gg_ref, j, c)).astype(o_ref.dtype)
        s_ref[j] = s


def _wkv(rkv, small, w2, a2, g2p, w0, a0, k_k, k_a, r_k, ln_g, ln_b, side, *, tb=64, pairs=32):
    bsz, t, d3 = rkv.shape
    d = d3 // 3
    wl = pairs * LANES
    nh = d // wl
    seq = lambda w, f: pl.BlockSpec((1, tb, w), lambda b, h, ti: (b, ti, f(h)))
    chan = lambda r, w, f: pl.BlockSpec((r, w), lambda b, h, ti: (0, f(h)))
    vec = lambda q: q.reshape(1, d)
    blk = pltpu.VMEM((tb, wl), F32)
    nt_steps = t // tb
    s_in, s_out, s_shape = _side_specs(side, lambda b, h, ti: (b * nh + h) * nt_steps + ti, bsz * nh * nt_steps)
    return pl.pallas_call(
        functools.partial(_wkv_kernel, n_chunks=tb // CHUNK, n_pairs=pairs),
        grid=(bsz, nh, t // tb),
        in_specs=[seq(wl, lambda h: h), seq(wl, lambda h: nh + h), seq(wl, lambda h: 2 * nh + h),
                  seq(G_RANK_PAD, lambda h: 0), seq(W_RANK, lambda h: SMALL_WLO_BLK),
                  seq(A_RANK, lambda h: SMALL_ALO_BLK),
                  chan(W_RANK, wl, lambda h: h), chan(A_RANK, wl, lambda h: h), chan(G_RANK_PAD, wl, lambda h: h)]
                 + [chan(1, wl, lambda h: h)] * 7 + [s_in],
        out_specs=[seq(wl, lambda h: h), s_out],
        out_shape=[jax.ShapeDtypeStruct((bsz, t, d), BF16), s_shape],
        scratch_shapes=[pltpu.VMEM((pairs, LANES, LANES), F32)] + [blk] * 7,
        name="wkv7_scan",
        compiler_params=_params("arbitrary", "arbitrary", "arbitrary"),
    )(rkv, rkv, rkv, small, small, small,
      w2, a2, g2p, vec(w0), vec(a0), vec(k_k), vec(k_a), vec(r_k), vec(ln_g), vec(ln_b), side[0])


def kernel(x, p, norm_mix_g, w_in, ssd_conv_w, ssd_conv_b, ssd_dt_bias, ssd_a_log, ssd_d, ssd_norm_g, rwkv_mu, rwkv_w0, rwkv_w2, rwkv_a0, rwkv_a2, rwkv_g2, rwkv_k_k, rwkv_k_a, rwkv_r_k, rwkv_ln_g, rwkv_ln_b, w_branch_ssd, w_branch_rwkv, w_out, norm_ffn_g, w_ff1, w_ff2, norm_ple_g, w_ple_gate, w_ple_proj, ple_post_g, final_norm_g):
    bsz, t, d = x.shape
    m = bsz * t
    depth = w_in.shape[0]
    xf = x.reshape(m, d)
    for i in range(depth):
        c_dt = SSD_D_INNER + SSD_CONV_DIM
        c_r = c_dt + SSD_HEADS
        c_wlo = c_r + 3 * d
        c_glo = c_wlo + W_RANK + A_RANK
        c_gate = c_glo + G_RANK
        w_in_t = jnp.swapaxes(w_in[i], 0, 1)
        w_head = w_in_t[:c_r].astype(BF16)
        g2p = jnp.pad(rwkv_g2[i], ((0, G_RANK_PAD - G_RANK), (0, 0))).astype(BF16)
        whole = lambda w: (w, 0, w.shape[0])

        h = _rmsnorm(xf, norm_mix_g[i], BF16)
        proj = functools.partial(_mm, h, w_nt=True, out_dtype=F32, **TILE_WIDE)
        zx, w_rkv = proj(w_head, name="proj_zx", n=c_dt, side=(w_in_t, c_r, 3 * d))
        rkv, w_tail = proj(w_rkv, name="proj_rkv", n=3 * d, lerp_mu=rwkv_mu[i][:3 * d], seq_rows=t,
                           side=(w_in_t, c_wlo, w_in.shape[2] - c_wlo))
        gates, w_bssd_b = proj(w_tail, name="proj_gates", w_row0=c_gate - c_wlo, n=2 * d, side=whole(w_branch_ssd[i]))
        mu_lo = jnp.pad(rwkv_mu[i][3 * d:], (0, G_RANK_PAD - G_RANK))
        small = _proj_small(h, w_head, w_tail, mu_lo, dt_row0=c_dt, lo_row0=0, seq_rows=t).reshape(bsz, t, -1)
        zx = zx.reshape(bsz, t, -1)
        rkv = rkv.reshape(bsz, t, -1)

        u_ssd, w_ff1_b = _ssd(zx, small, ssd_conv_w[i], ssd_conv_b[i], ssd_dt_bias[i], ssd_a_log[i], ssd_d[i],
                              ssd_norm_g[i], whole(w_ff1[i]))
        u_rwkv, w_ff2_b = _wkv(rkv, small, rwkv_w2[i].astype(BF16), rwkv_a2[i].astype(BF16), g2p,
                               rwkv_w0[i], rwkv_a0[i], rwkv_k_k[i], rwkv_k_a[i], rwkv_r_k[i], rwkv_ln_g[i],
                               rwkv_ln_b[i], whole(w_ff2[i]))

        part, w_brwkv_b = _mm(u_ssd.reshape(m, SSD_D_INNER), w_bssd_b, name="branch_ssd", out_dtype=F32, **TILE_K2D,
                              mul=gates, mul_act="sigmoid", mul_col0=0, side=whole(w_branch_rwkv[i]))
        merged, w_out_b = _mm(u_rwkv.reshape(m, d), w_brwkv_b, name="branch_rwkv", out_dtype=BF16, **TILE_EPI,
                              mul=gates, mul_act="sigmoid", mul_col0=d, res=part, side=whole(w_out[i]))
        xf, xg, ss = _mm(merged, w_out_b, name="out_proj", out_dtype=F32, **TILE_EPI, res=xf,
                         next_norm_g=norm_ffn_g[i])

        ff, w_ple_b = _mm(xg, w_ff1_b, name="ffn_up", out_dtype=BF16, **TILE_WIDE, act="relu2", row_ss=ss,
                          side=whole(w_ple_gate[i]))
        xf, xg, ss = _mm(ff, w_ff2_b, name="ffn_down", out_dtype=F32, **TILE_K4D, res=xf,
                         next_norm_g=norm_ple_g[i])

        e = _ple_embed(p[i].reshape(m, PLE_DIM).astype(BF16), w_ple_proj[i].astype(BF16), ple_post_g[i])
        xf = _mm(xg, w_ple_b, name="ple_gate", out_dtype=F32, **TILE_EPI, act="sigmoid", mul=e, res=xf,
                 row_ss=ss)
    return _rmsnorm(xf, final_norm_g, F32).reshape(bsz, t, d)
```
